```python
import math
import jax, jax.numpy as jnp
from jax import lax
import numpy as np

D_MODEL = 1024
BATCH = 16
SEQ = 2048
DEPTH = 4

GRID_W = 64
CTX_LEN = 256
HEAD_DIM = 64
A_Q_HEADS = 16
A_KV_HEADS = 4
A_GROUP = A_Q_HEADS // A_KV_HEADS
WINDOW = 128
BLOCK = 128
B_HEADS = D_MODEL // (2 * HEAD_DIM)
N_EXPERTS = 16
N_GROUPS = 4
GROUP_SIZE = N_EXPERTS // N_GROUPS
TOP_K = 2
GROUP_SCORE_K = 2
D_EXPERT = D_MODEL // 2
ROPE_THETA = 10000.0
N_A_LAYERS = (DEPTH + 1) // 2
N_B_LAYERS = DEPTH // 2
DEEPNORM_ALPHA = (2 * DEPTH) ** 0.25
DEEPNORM_BETA = (8 * DEPTH) ** -0.25
LN_EPS = 1e-6
SUBLN_EPS = 1e-5
NEG_INF = -1e30
ATTN_SCALE = HEAD_DIM ** -0.5

kernel_name = "hybrid_swa_diffattn_sharedrouter_moe_dit"

F32 = jnp.float32


def layer_norm(x, g, b):
    xf = x.astype(F32)
    mu = xf.mean(-1, keepdims=True)
    var = jnp.square(xf - mu).mean(-1, keepdims=True)
    return ((xf - mu) * lax.rsqrt(var + LN_EPS) * g.astype(F32) + b.astype(F32)).astype(x.dtype)


def rms_norm(x, g):
    xf = x.astype(F32)
    return (xf * lax.rsqrt(jnp.square(xf).mean(-1, keepdims=True) + SUBLN_EPS) * g.astype(F32)).astype(x.dtype)


def axial_rope_tables(n_tokens):
    n_rows = n_tokens // GRID_W
    rows = jnp.broadcast_to(jnp.arange(n_rows)[:, None], (n_rows, GRID_W)).reshape(-1)
    cols = jnp.broadcast_to(jnp.arange(GRID_W)[None, :], (n_rows, GRID_W)).reshape(-1)
    quarter = HEAD_DIM // 4
    inv_freq = ROPE_THETA ** (-jnp.arange(quarter, dtype=F32) / quarter)
    ang_r = rows.astype(F32)[:, None] * inv_freq
    ang_c = cols.astype(F32)[:, None] * inv_freq
    return (jnp.cos(ang_r), jnp.sin(ang_r), jnp.cos(ang_c), jnp.sin(ang_c))


def _rotate(x, cos, sin):
    x1, x2 = jnp.split(x, 2, axis=-1)
    return jnp.concatenate([x1 * cos - x2 * sin, x2 * cos + x1 * sin], axis=-1)


def apply_axial_rope(x, tables):
    cos_r, sin_r, cos_c, sin_c = tables
    shape = (x.shape[1],) + (1,) * (x.ndim - 3) + (cos_r.shape[-1],)
    xr, xc = jnp.split(x.astype(F32), 2, axis=-1)
    out = jnp.concatenate([_rotate(xr, cos_r.reshape(shape), sin_r.reshape(shape)),
                           _rotate(xc, cos_c.reshape(shape), sin_c.reshape(shape))], axis=-1)
    return out.astype(x.dtype)


def sink_attend(q, parts, sink):
    scores = []
    for k, v, mask in parts:
        s = jnp.einsum('bqhgd,bkhd->bhgqk', q, k).astype(F32) * ATTN_SCALE
        if mask is not None:
            s = jnp.where(mask, s, NEG_INF)
        scores.append(s)
    snk = sink.astype(F32)[None, :, :, None, None]
    m = snk
    for s in scores:
        m = jnp.maximum(m, s.max(-1, keepdims=True))
    probs = [jnp.exp(s - m) for s in scores]
    denom = jnp.exp(snk - m)
    for p in probs:
        denom = denom + p.sum(-1, keepdims=True)
    out = None
    for p, (k, v, mask) in zip(probs, parts):
        o = jnp.einsum('bhgqk,bkhd->bqhgd', (p / denom).astype(v.dtype), v)
        out = o if out is None else out + o
    return out


def window_attention_latent(q, k, v, kc, vc, sink):
    S = q.shape[1]
    nb = S // BLOCK
    span = BLOCK + 2 * WINDOW
    kp = jnp.pad(k, ((0, 0), (WINDOW, WINDOW), (0, 0), (0, 0)))
    vp = jnp.pad(v, ((0, 0), (WINDOW, WINDOW), (0, 0), (0, 0)))
    r = jnp.arange(span)
    qi = jnp.arange(BLOCK)
    band = jnp.abs((r - WINDOW)[None, :] - qi[:, None]) <= WINDOW

    def block(n):
        start = n * BLOCK
        qb = lax.dynamic_slice_in_dim(q, start, BLOCK, axis=1)
        kb = lax.dynamic_slice_in_dim(kp, start, span, axis=1)
        vb = lax.dynamic_slice_in_dim(vp, start, span, axis=1)
        key_pos = start - WINDOW + r
        mask = band & ((key_pos >= 0) & (key_pos < S))[None, :]
        return sink_attend(qb, [(kb, vb, mask), (kc, vc, None)], sink)

    out = lax.map(block, jnp.arange(nb))
    return jnp.moveaxis(out, 0, 1).reshape(q.shape)


def window_gqa_mixer(hl, hc, w_qkv, w_o, sink, tables, with_ctx_out):
    B, S, _ = hl.shape
    C = hc.shape[1]
    q_dim = A_Q_HEADS * HEAD_DIM
    kv_dim = A_KV_HEADS * HEAD_DIM
    ql, kl, vl = jnp.split(hl @ w_qkv, [q_dim, q_dim + kv_dim], axis=-1)
    ql = apply_axial_rope(ql.reshape(B, S, A_KV_HEADS, A_GROUP, HEAD_DIM), tables)
    kl = apply_axial_rope(kl.reshape(B, S, A_KV_HEADS, HEAD_DIM), tables)
    vl = vl.reshape(B, S, A_KV_HEADS, HEAD_DIM)
    kc, vc = jnp.split(hc @ w_qkv[:, q_dim:], 2, axis=-1)
    kc = kc.reshape(B, C, A_KV_HEADS, HEAD_DIM)
    vc = vc.reshape(B, C, A_KV_HEADS, HEAD_DIM)
    sink_hg = sink.reshape(A_KV_HEADS, A_GROUP)
    out_l = window_attention_latent(ql, kl, vl, kc, vc, sink_hg).reshape(B, S, q_dim) @ w_o
    if not with_ctx_out:
        return out_l, None
    qc = (hc @ w_qkv[:, :q_dim]).reshape(B, C, A_KV_HEADS, A_GROUP, HEAD_DIM)
    out_c = sink_attend(qc, [(kc, vc, None)], sink_hg).reshape(B, C, q_dim) @ w_o
    return out_l, out_c


def diff_attend(q, k, v, lam):
    s = jnp.einsum('bqhmd,bkhmd->bhmqk', q, k).astype(F32) * ATTN_SCALE
    p = jax.nn.softmax(s, axis=-1)
    a = p[:, :, 0] - lam * p[:, :, 1]
    return jnp.einsum('bhqk,bkhe->bqhe', a.astype(v.dtype), v)


def diff_mixer(hl, hc, w_qkv, w_o, lam_params, subln_g, lam_init, tables, with_ctx_out):
    B, S, _ = hl.shape
    C = hc.shape[1]
    dq = B_HEADS * 2 * HEAD_DIM
    ql, kl, vl = jnp.split(hl @ w_qkv, [dq, 2 * dq], axis=-1)
    ql = apply_axial_rope(ql.reshape(B, S, B_HEADS, 2, HEAD_DIM), tables)
    kl = apply_axial_rope(kl.reshape(B, S, B_HEADS, 2, HEAD_DIM), tables)
    vl = vl.reshape(B, S, B_HEADS, 2 * HEAD_DIM)
    kc, vc = jnp.split(hc @ w_qkv[:, dq:], 2, axis=-1)
    kc = kc.reshape(B, C, B_HEADS, 2, HEAD_DIM)
    vc = vc.reshape(B, C, B_HEADS, 2 * HEAD_DIM)
    lp = lam_params.astype(F32)
    lam = jnp.exp(jnp.sum(lp[0] * lp[1])) - jnp.exp(jnp.sum(lp[2] * lp[3])) + lam_init
    k_all = jnp.concatenate([kl, kc], axis=1)
    v_all = jnp.concatenate([vl, vc], axis=1)

    def block(n):
        qb = lax.dynamic_slice_in_dim(ql, n * BLOCK, BLOCK, axis=1)
        return diff_attend(qb, k_all, v_all, lam)

    out_l = jnp.moveaxis(lax.map(block, jnp.arange(S // BLOCK)), 0, 1).reshape(B, S, B_HEADS, 2 * HEAD_DIM)

    def finish(o):
        n = o.shape[1]
        return (rms_norm(o, subln_g) * (1.0 - lam_init)).reshape(B, n, dq) @ w_o

    if not with_ctx_out:
        return finish(out_l), None
    qc = (hc @ w_qkv[:, :dq]).reshape(B, C, B_HEADS, 2, HEAD_DIM)
    out_c = diff_attend(qc, kc, vc, lam)
    return finish(out_l), finish(out_c)


def moe_ffn(h, w_router, router_bias, w_gate, w_up, w_down):
    N = h.shape[0]
    s = jax.nn.sigmoid(h.astype(F32) @ w_router.astype(F32))
    biased = s + router_bias.astype(F32)
    bg = biased.reshape(N, N_GROUPS, GROUP_SIZE)
    gscore = lax.top_k(bg, GROUP_SCORE_K)[0].sum(-1)
    gsel = jnp.argmax(gscore, axis=-1)
    in_group = jnp.take_along_axis(bg, gsel[:, None, None], axis=1)[:, 0]
    _, loc = lax.top_k(in_group, TOP_K)
    eid = gsel[:, None] * GROUP_SIZE + loc
    s_sel = jnp.take_along_axis(s, eid, axis=-1)
    w = s_sel / s_sel.sum(-1, keepdims=True)
    gates = (jax.nn.one_hot(eid, N_EXPERTS, dtype=F32) * w[..., None]).sum(1)
    out = jnp.zeros_like(h)
    for e in range(N_EXPERTS):
        y = (jax.nn.silu(h @ w_gate[e]) * (h @ w_up[e])) @ w_down[e]
        out = out + gates[:, e:e + 1].astype(h.dtype) * y
    return out


def setup_inputs(seed: int = 0) -> dict:
    key = jax.random.key(seed)
    ks = jax.random.split(key, 22)

    def nrm(k, shape, scale):
        return jax.random.normal(k, shape, F32) * scale

    D = D_MODEL
    qa = A_Q_HEADS * HEAD_DIM
    kva = A_KV_HEADS * HEAD_DIM
    dq = B_HEADS * 2 * HEAD_DIM
    return {
        "x": nrm(ks[0], (BATCH, SEQ, D), 1.0),
        "c": nrm(ks[1], (BATCH, D), 1.0),
        "ctx": nrm(ks[2], (BATCH, CTX_LEN, D), 1.0),
        "c_ctx": nrm(ks[3], (D,), 1.0),
        "w_ada": nrm(ks[4], (DEPTH, D, 6 * D), 0.5 * D ** -0.5),
        "b_ada": nrm(ks[5], (DEPTH, 6 * D), 0.02),
        "wqkv_a": nrm(ks[6], (N_A_LAYERS, D, qa + 2 * kva), D ** -0.5),
        "wo_a": nrm(ks[7], (N_A_LAYERS, qa, D), qa ** -0.5 * DEEPNORM_BETA),
        "sink_a": nrm(ks[8], (N_A_LAYERS, A_Q_HEADS), 0.5),
        "wqkv_b": nrm(ks[9], (N_B_LAYERS, D, 3 * dq), D ** -0.5),
        "wo_b": nrm(ks[10], (N_B_LAYERS, dq, D), dq ** -0.5 * DEEPNORM_BETA),
        "lambda_b": nrm(ks[11], (N_B_LAYERS, 4, HEAD_DIM), 0.1),
        "subln_b": 1.0 + nrm(ks[12], (N_B_LAYERS, 2 * HEAD_DIM), 0.02),
        "ln_attn_g": 1.0 + nrm(ks[13], (DEPTH, D), 0.02),
        "ln_attn_b": nrm(ks[14], (DEPTH, D), 0.02),
        "ln_ffn_g": 1.0 + nrm(ks[15], (DEPTH, D), 0.02),
        "ln_ffn_b": nrm(ks[16], (DEPTH, D), 0.02),
        "w_router": nrm(ks[17], (D, N_EXPERTS), D ** -0.5),
        "router_bias": nrm(ks[18], (N_EXPERTS,), 0.01),
        "w_gate": nrm(ks[19], (DEPTH, N_EXPERTS, D, D_EXPERT), D ** -0.5),
        "w_up": nrm(ks[20], (DEPTH, N_EXPERTS, D, D_EXPERT), D ** -0.5),
        "w_down": nrm(ks[21], (DEPTH, N_EXPERTS, D_EXPERT, D), D_EXPERT ** -0.5 * DEEPNORM_BETA),
    }


def reference(x, c, ctx, c_ctx, w_ada, b_ada, wqkv_a, wo_a, sink_a, wqkv_b, wo_b, lambda_b, subln_b,
              ln_attn_g, ln_attn_b, ln_ffn_g, ln_ffn_b, w_router, router_bias, w_gate, w_up, w_down):
    B, S, D = x.shape
    C = ctx.shape[1]
    tables = axial_rope_tables(S)
    xl, xc = x, ctx
    silu_c = jax.nn.silu(c)
    silu_cc = jax.nn.silu(c_ctx)
    for i in range(DEPTH):
        last = i == DEPTH - 1
        mod_l = (silu_c @ w_ada[i] + b_ada[i])[:, None, :]
        mod_c = silu_cc @ w_ada[i] + b_ada[i]
        sh1_l, sc1_l, g1_l, sh2_l, sc2_l, g2_l = jnp.split(mod_l, 6, axis=-1)
        sh1_c, sc1_c, g1_c, sh2_c, sc2_c, g2_c = jnp.split(mod_c, 6, axis=-1)

        hl = xl * (1.0 + sc1_l) + sh1_l
        hc = xc * (1.0 + sc1_c) + sh1_c
        j = i // 2
        if i % 2 == 0:
            al, ac = window_gqa_mixer(hl, hc, wqkv_a[j], wo_a[j], sink_a[j], tables, not last)
        else:
            lam_init = 0.8 - 0.6 * math.exp(-0.3 * i)
            al, ac = diff_mixer(hl, hc, wqkv_b[j], wo_b[j], lambda_b[j], subln_b[j], lam_init, tables, not last)
        xl = layer_norm(DEEPNORM_ALPHA * xl + g1_l * al, ln_attn_g[i], ln_attn_b[i])

        hl2 = (xl * (1.0 + sc2_l) + sh2_l).reshape(B * S, D)
        if last:
            yl = moe_ffn(hl2, w_router, router_bias, w_gate[i], w_up[i], w_down[i]).reshape(B, S, D)
        else:
            xc = layer_norm(DEEPNORM_ALPHA * xc + g1_c * ac, ln_attn_g[i], ln_attn_b[i])
            hc2 = (xc * (1.0 + sc2_c) + sh2_c).reshape(B * C, D)
            y = moe_ffn(jnp.concatenate([hl2, hc2], axis=0), w_router, router_bias, w_gate[i], w_up[i], w_down[i])
            yl = y[:B * S].reshape(B, S, D)
            yc = y[B * S:].reshape(B, C, D)
            xc = layer_norm(DEEPNORM_ALPHA * xc + g2_c * yc, ln_ffn_g[i], ln_ffn_b[i])
        xl = layer_norm(DEEPNORM_ALPHA * xl + g2_l * yl, ln_ffn_g[i], ln_ffn_b[i])
    return xl
```

```python
import functools
import math

import jax
import jax.numpy as jnp
from jax import lax
from jax.experimental import pallas as pl
from jax.experimental.pallas import tpu as pltpu

F32 = jnp.float32
BF16 = jnp.bfloat16

HEAD_DIM = 64
A_Q_HEADS = 16
A_KV_HEADS = 4
A_GROUP = A_Q_HEADS // A_KV_HEADS
GRID_W = 64
WINDOW = 128
N_EXPERTS = 16
N_GROUPS = 4
GROUP_SIZE = N_EXPERTS // N_GROUPS
ROPE_THETA = 10000.0
LN_EPS = 1e-6
SUBLN_EPS = 1e-5
NEG_INF = -1e30
ATTN_SCALE = HEAD_DIM ** -0.5

LANES = 128
TM = 256
ADA_ROWS = 24
VMEM_LIMIT = 56 * 1024 * 1024

PAIRS = ((0, 1), (0, 2), (0, 3), (1, 2), (1, 3), (2, 3))
N_BUCKETS = N_GROUPS * len(PAIRS)

NT_DIMS = (((1,), (1,)), ((), ()))


def _cparams(*sem):
    return pltpu.CompilerParams(dimension_semantics=sem, vmem_limit_bytes=VMEM_LIMIT)


def _ada_body(c_ref, w_ref, b_ref, o_ref):
    c = c_ref[...]
    sc = c * jax.nn.sigmoid(c)
    o_ref[0] = jnp.dot(sc, w_ref[0], precision=lax.Precision.HIGHEST,
                       preferred_element_type=F32) + b_ref[0]


def _ada_modulation(cc, w_ada, b_ada):
    depth, d, d6 = w_ada.shape
    tn = 1536
    return pl.pallas_call(
        _ada_body,
        grid=(depth, d6 // tn),
        in_specs=[
            pl.BlockSpec((ADA_ROWS, d), lambda l, n: (0, 0)),
            pl.BlockSpec((1, d, tn), lambda l, n: (l, 0, n)),
            pl.BlockSpec((1, 1, tn), lambda l, n: (l, 0, n)),
        ],
        out_specs=pl.BlockSpec((1, ADA_ROWS, tn), lambda l, n: (l, 0, n)),
        out_shape=jax.ShapeDtypeStruct((depth, ADA_ROWS, d6), F32),
        compiler_params=_cparams("arbitrary", "arbitrary"),
    )(cc, w_ada, b_ada.reshape(depth, 1, d6))


def _tile_maps(mode, tiles_lat, n_batch):
    tpb = tiles_lat + 1
    if mode == "all":
        blk = lambda i: i
        modrow = lambda i: jnp.where(i % tpb == tiles_lat, n_batch, i // tpb)
        pos = lambda i: i % tpb
    elif mode == "lat":
        blk = lambda i: (i // tiles_lat) * tpb + i % tiles_lat
        modrow = lambda i: i // tiles_lat
        pos = lambda i: i % tiles_lat
    else:
        blk = lambda i: i
        modrow = lambda i: i // tiles_lat
        pos = lambda i: i % tiles_lat
    return blk, modrow, pos


def _layer_norm(z, g, b):
    mu = jnp.mean(z, axis=-1, keepdims=True)
    zc = z - mu
    var = jnp.mean(zc * zc, axis=-1, keepdims=True)
    return zc * lax.rsqrt(var + LN_EPS) * g + b


def _lnqkv_body(*refs, has_ln, has_qkv, alpha, n_rope, dup_from, n_out):
    refs = list(refs)
    x_ref = refs.pop(0)
    if has_ln:
        y_ref, g2_ref, lng_ref, lnb_ref = refs[:4]
        refs = refs[4:]
    if has_qkv:
        sc_ref, sh_ref, w_ref, cos_ref, sin_ref = refs[:5]
        refs = refs[5:]
    x = x_ref[...]
    if has_ln:
        xo_ref = refs.pop(0)
        z = alpha * x + g2_ref[0] * y_ref[...].astype(F32)
        x = _layer_norm(z, lng_ref[...], lnb_ref[...])
        xo_ref[...] = x
    if not has_qkv:
        return
    qkv_ref = refs.pop(0)
    h = (x * (1.0 + sc_ref[0]) + sh_ref[0]).astype(BF16)
    r = jnp.dot(h, w_ref[...], preferred_element_type=F32)
    cos = cos_ref[...]
    sin = sin_ref[...]
    lane = lax.broadcasted_iota(jnp.int32, (TM, LANES), 1)
    first16 = (lane % 32) < 16
    low_half = lane < HEAD_DIM
    q_cols = A_Q_HEADS * HEAD_DIM
    dst = 0
    for c in range(n_out // LANES):
        seg = r[:, c * LANES:(c + 1) * LANES]
        col = c * LANES
        if col < q_cols:
            seg = seg * ATTN_SCALE
        if col < n_rope:
            rot = jnp.where(first16, pltpu.roll(seg, LANES - 16, 1), pltpu.roll(seg, 16, 1))
            seg = seg * cos + rot * sin
        if col >= dup_from:
            swapped = pltpu.roll(seg, HEAD_DIM, 1)
            qkv_ref[:, dst:dst + LANES] = jnp.where(low_half, seg, swapped).astype(BF16)
            qkv_ref[:, dst + LANES:dst + 2 * LANES] = jnp.where(low_half, swapped, seg).astype(BF16)
            dst += 2 * LANES
        else:
            qkv_ref[:, dst:dst + LANES] = seg.astype(BF16)
            dst += LANES


def _lnqkv(x, y, mod_ln, ln_g, ln_b, mod_qkv, w, cos_t, sin_t, *, mode_in, n_batch, tiles_lat,
           alpha, n_rope=0, dup_from=0):
    d = x.shape[1]
    has_ln = y is not None
    has_qkv = w is not None
    n_tiles = (n_batch * (tiles_lat + 1)) if mode_in == "all" else n_batch * tiles_lat
    blk, modrow, pos = _tile_maps(mode_in, tiles_lat, n_batch)
    row = pl.BlockSpec((TM, d), lambda i: (blk(i), 0))
    vec = pl.BlockSpec((1, d), lambda i: (0, 0))

    def modspec(chunk):
        return pl.BlockSpec((1, 1, d), lambda i: (modrow(i), 0, chunk))

    args, in_specs, out_specs, out_shape = [x], [row], [], []
    n_out = 0
    if has_ln:
        args += [y, mod_ln, ln_g.reshape(1, d), ln_b.reshape(1, d)]
        in_specs += [row, modspec(5), vec, vec]
        out_specs.append(row)
        out_shape.append(jax.ShapeDtypeStruct(x.shape, F32))
    if has_qkv:
        n_out = w.shape[1]
        n_store = n_out + (n_out - dup_from)
        args += [mod_qkv, mod_qkv, w, cos_t, sin_t]
        in_specs += [modspec(1), modspec(0),
                     pl.BlockSpec((d, n_out), lambda i: (0, 0)),
                     pl.BlockSpec((TM, LANES), lambda i: (pos(i), 0)),
                     pl.BlockSpec((TM, LANES), lambda i: (pos(i), 0))]
        out_specs.append(pl.BlockSpec((TM, n_store), lambda i: (blk(i), 0)))
        out_shape.append(jax.ShapeDtypeStruct((x.shape[0], n_store), BF16))
    body = functools.partial(_lnqkv_body, has_ln=has_ln, has_qkv=has_qkv, alpha=alpha,
                             n_rope=n_rope, dup_from=dup_from, n_out=n_out)
    return pl.pallas_call(
        body, grid=(n_tiles,), in_specs=in_specs, out_specs=out_specs, out_shape=out_shape,
        compiler_params=_cparams("arbitrary"),
    )(*args)


def _softmax_parts(scores, sink):
    m = None
    for s in scores:
        sm = jnp.max(s, axis=-1, keepdims=True)
        m = sm if m is None else jnp.maximum(m, sm)
    if sink is not None:
        m = jnp.maximum(m, sink)
    probs = [jnp.exp(s - m) for s in scores]
    denom = None
    for p in probs:
        ps = jnp.sum(p, axis=-1, keepdims=True)
        denom = ps if denom is None else denom + ps
    if sink is not None:
        denom = denom + jnp.exp(sink - m)
    return probs, 1.0 / denom


def _attn_a_body(sink_ref, q_ref, k_ref, v_ref, o_ref, *, seq, tiles_lat):
    qi = pl.program_id(1)
    lane = lax.broadcasted_iota(jnp.int32, (TM, LANES), 1)
    low_half = lane < HEAD_DIM
    span = TM + 2 * WINDOW

    def run(parts):
        for grp in range(A_Q_HEADS // 2):
            q2 = q_ref[:, grp * LANES:(grp + 1) * LANES]
            h = (2 * grp) // A_GROUP
            outs = []
            for half in range(2):
                keep = low_half if half == 0 else jnp.logical_not(low_half)
                qh = jnp.where(keep, q2, jnp.zeros_like(q2))
                scores = []
                for (r0, nr, mask) in parts:
                    kk = k_ref[pl.ds(r0, nr), h * LANES:(h + 1) * LANES]
                    s = lax.dot_general(qh, kk, NT_DIMS, preferred_element_type=F32)
                    if mask is not None:
                        s = jnp.where(mask, s, NEG_INF)
                    scores.append(s)
                probs, inv = _softmax_parts(scores, sink_ref[2 * grp + half])
                acc = None
                for p, (r0, nr, _) in zip(probs, parts):
                    vv = v_ref[pl.ds(r0, nr), h * LANES:(h + 1) * LANES]
                    o = jnp.dot(p.astype(BF16), vv, preferred_element_type=F32)
                    acc = o if acc is None else acc + o
                outs.append(acc * inv)
            o_ref[:, grp * LANES:(grp + 1) * LANES] = jnp.where(low_half, outs[0], outs[1]).astype(BF16)

    @pl.when(qi < tiles_lat)
    def _():
        start = pl.multiple_of(jnp.clip(qi * TM - WINDOW, 0, seq - span), WINDOW)
        qpos = qi * TM + lax.broadcasted_iota(jnp.int32, (TM, span), 0)
        kpos = start + lax.broadcasted_iota(jnp.int32, (TM, span), 1)
        band = jnp.abs(kpos - qpos) <= WINDOW
        run([(start, span, band), (seq, k_ref.shape[0] - seq, None)])

    @pl.when(qi >= tiles_lat)
    def _():
        run([(seq, k_ref.shape[0] - seq, None)])


def _attn_a(qkv, sink, *, n_batch, seq, ctx_len, with_ctx):
    tiles_lat = seq // TM
    tpb = tiles_lat + ctx_len // TM
    rows_b = seq + ctx_len
    q_cols = A_Q_HEADS * HEAD_DIM
    kv_cols = 2 * A_KV_HEADS * HEAD_DIM
    nq = tpb if with_ctx else tiles_lat
    body = functools.partial(_attn_a_body, seq=seq, tiles_lat=tiles_lat)
    return pl.pallas_call(
        body,
        grid=(n_batch, nq),
        in_specs=[
            pl.BlockSpec(memory_space=pltpu.SMEM),
            pl.BlockSpec((TM, q_cols), lambda b, i: (b * tpb + i, 0)),
            pl.BlockSpec((rows_b, kv_cols), lambda b, i: (b, q_cols // kv_cols)),
            pl.BlockSpec((rows_b, kv_cols), lambda b, i: (b, q_cols // kv_cols + 1)),
        ],
        out_specs=pl.BlockSpec((TM, q_cols), lambda b, i: (b * tpb + i, 0)),
        out_shape=jax.ShapeDtypeStruct((qkv.shape[0], q_cols), BF16),
        compiler_params=_cparams("arbitrary", "arbitrary"),
    )(sink, qkv, qkv, qkv)


def _attn_b_body(lam_ref, g_ref, q_ref, k_ref, v_ref, o_ref, *, seq, tiles_lat, lam_init):
    qi = pl.program_id(2)
    lp = lam_ref[...]
    lam = (jnp.exp(jnp.sum(lp[0:1] * lp[1:2], axis=-1, keepdims=True))
           - jnp.exp(jnp.sum(lp[2:3] * lp[3:4], axis=-1, keepdims=True)) + lam_init)
    lane = lax.broadcasted_iota(jnp.int32, (TM, LANES), 1)
    low_half = lane < HEAD_DIM
    q = q_ref[...]
    q1 = jnp.where(low_half, q, jnp.zeros_like(q))
    q2 = jnp.where(low_half, jnp.zeros_like(q), q)

    def run(r0, nr):
        kk = k_ref[pl.ds(r0, nr), :]
        vv = v_ref[pl.ds(r0, nr), :]
        s1 = lax.dot_general(q1, kk, NT_DIMS, preferred_element_type=F32)
        s2 = lax.dot_general(q2, kk, NT_DIMS, preferred_element_type=F32)
        (p1,), inv1 = _softmax_parts([s1], None)
        (p2,), inv2 = _softmax_parts([s2], None)
        a = p1 * inv1 - p2 * (lam * inv2)
        o = jnp.dot(a.astype(BF16), vv, preferred_element_type=F32)
        ms = jnp.mean(o * o, axis=-1, keepdims=True)
        o = o * lax.rsqrt(ms + SUBLN_EPS) * g_ref[...] * (1.0 - lam_init)
        o_ref[...] = o.astype(BF16)

    @pl.when(qi < tiles_lat)
    def _():
        run(0, k_ref.shape[0])

    @pl.when(qi >= tiles_lat)
    def _():
        run(seq, k_ref.shape[0] - seq)


def _attn_b(qkv, lam_params, subln_g, *, n_batch, seq, ctx_len, with_ctx, lam_init):
    tiles_lat = seq // TM
    tpb = tiles_lat + ctx_len // TM
    rows_b = seq + ctx_len
    n_heads = qkv.shape[1] // (3 * LANES)
    nq = tpb if with_ctx else tiles_lat
    body = functools.partial(_attn_b_body, seq=seq, tiles_lat=tiles_lat, lam_init=lam_init)
    return pl.pallas_call(
        body,
        grid=(n_batch, n_heads, nq),
        in_specs=[
            pl.BlockSpec((4, HEAD_DIM), lambda b, h, i: (0, 0)),
            pl.BlockSpec((1, LANES), lambda b, h, i: (0, 0)),
            pl.BlockSpec((TM, LANES), lambda b, h, i: (b * tpb + i, h)),
            pl.BlockSpec((rows_b, LANES), lambda b, h, i: (b, n_heads + h)),
            pl.BlockSpec((rows_b, LANES), lambda b, h, i: (b, 2 * n_heads + h)),
        ],
        out_specs=pl.BlockSpec((TM, LANES), lambda b, h, i: (b * tpb + i, h)),
        out_shape=jax.ShapeDtypeStruct((qkv.shape[0], n_heads * LANES), BF16),
        compiler_params=_cparams("arbitrary", "arbitrary", "arbitrary"),
    )(lam_params, subln_g.reshape(1, LANES), qkv, qkv, qkv)


def _route_rows(logits, bias):
    s = jax.nn.sigmoid(logits)
    biased = s + bias
    b = [biased[e:e + 1, :] for e in range(N_EXPERTS)]
    u = [s[e:e + 1, :] for e in range(N_EXPERTS)]
    gscore = []
    for g in range(N_GROUPS):
        best = None
        for (i, j) in PAIRS:
            t = b[GROUP_SIZE * g + i] + b[GROUP_SIZE * g + j]
            best = t if best is None else jnp.maximum(best, t)
        gscore.append(best)
    gsel = jnp.zeros(gscore[0].shape, jnp.int32)
    gbest = gscore[0]
    for g in range(1, N_GROUPS):
        better = gscore[g] > gbest
        gsel = jnp.where(better, g, gsel)
        gbest = jnp.where(better, gscore[g], gbest)

    def pick(rows, k):
        out = rows[GROUP_SIZE * (N_GROUPS - 1) + k]
        for g in range(N_GROUPS - 2, -1, -1):
            out = jnp.where(gsel == g, rows[GROUP_SIZE * g + k], out)
        return out

    v = [pick(b, k) for k in range(GROUP_SIZE)]
    w = [pick(u, k) for k in range(GROUP_SIZE)]
    sel = []
    for k in range(GROUP_SIZE):
        cnt = jnp.zeros(gsel.shape, jnp.int32)
        for j in range(GROUP_SIZE):
            if j == k:
                continue
            beats = (v[j] >= v[k]) if j < k else (v[j] > v[k])
            cnt = cnt + jnp.where(beats, 1, 0)
        sel.append(cnt < 2)
    pidx = jnp.zeros(gsel.shape, jnp.int32)
    u_lo = jnp.zeros(gbest.shape, F32)
    u_hi = jnp.zeros(gbest.shape, F32)
    for idx, (i, j) in enumerate(PAIRS):
        both = jnp.logical_and(sel[i], sel[j])
        pidx = jnp.where(both, idx, pidx)
        u_lo = jnp.where(both, w[i], u_lo)
        u_hi = jnp.where(both, w[j], u_hi)
    tot = u_lo + u_hi
    bucket = (gsel * len(PAIRS) + pidx).astype(F32)
    return bucket, u_lo / tot, u_hi / tot


def _oproj_body(a_ref, wo_ref, x_ref, g1_ref, lng_ref, lnb_ref, sc_ref, sh_ref, wr2_ref, wrh_ref, rb_ref,
                xo_ref, h2_ref, r_ref, *, alpha):
    al = jnp.dot(a_ref[...], wo_ref[...], preferred_element_type=F32)
    z = alpha * x_ref[...] + g1_ref[0] * al
    xn = _layer_norm(z, lng_ref[...], lnb_ref[...])
    xo_ref[...] = xn
    h2 = xn * (1.0 + sc_ref[0]) + sh_ref[0]
    hi = h2.astype(BF16)
    lo = (h2 - hi.astype(F32)).astype(BF16)
    h2_ref[...] = hi
    l2 = lax.dot_general(wr2_ref[...], hi, NT_DIMS, preferred_element_type=F32)
    l1 = lax.dot_general(wrh_ref[...], lo, NT_DIMS, preferred_element_type=F32)
    logits = l2[:N_EXPERTS] + l2[N_EXPERTS:] + l1
    bucket, w_lo, w_hi = _route_rows(logits, rb_ref[...])
    r_ref[0:1, :] = bucket
    r_ref[1:2, :] = w_lo
    r_ref[2:3, :] = w_hi
    r_ref[3:8, :] = jnp.zeros((5, TM), F32)


def _oproj(a, wo, x, mod, ln_g, ln_b, wr2, wrh, rbias, *, mode_in, n_batch, tiles_lat, alpha):
    d = x.shape[1]
    compact = mode_in == "lat"
    n_tiles = n_batch * tiles_lat if compact else n_batch * (tiles_lat + 1)
    blk, modrow, _ = _tile_maps(mode_in, tiles_lat, n_batch)
    row_in = pl.BlockSpec((TM, d), lambda i: (blk(i), 0))
    row_out = pl.BlockSpec((TM, d), lambda i: (i, 0))
    vec = pl.BlockSpec((1, d), lambda i: (0, 0))

    def modspec(chunk):
        return pl.BlockSpec((1, 1, d), lambda i: (modrow(i), 0, chunk))

    body = functools.partial(_oproj_body, alpha=alpha)
    return pl.pallas_call(
        body,
        grid=(n_tiles,),
        in_specs=[row_in, pl.BlockSpec((d, d), lambda i: (0, 0)), row_in, modspec(2), vec, vec,
                  modspec(4), modspec(3),
                  pl.BlockSpec((2 * N_EXPERTS, d), lambda i: (0, 0)),
                  pl.BlockSpec((N_EXPERTS, d), lambda i: (0, 0)),
                  pl.BlockSpec((N_EXPERTS, 1), lambda i: (0, 0))],
        out_specs=[row_out, row_out, pl.BlockSpec((8, TM), lambda i: (0, i))],
        out_shape=[jax.ShapeDtypeStruct((n_tiles * TM, d), F32),
                   jax.ShapeDtypeStruct((n_tiles * TM, d), BF16),
                   jax.ShapeDtypeStruct((8, n_tiles * TM), F32)],
        compiler_params=_cparams("arbitrary"),
    )(a, wo, x, mod, ln_g.reshape(1, d), ln_b.reshape(1, d), mod, mod, wr2, wrh, rbias)


def _moe_body(ea_ref, eb_ref, act_ref, x_ref, gw_ref, wgu_a, wd_a, wgu_b, wd_b, y_ref, *, d_exp):
    i = pl.program_id(0)

    @pl.when(act_ref[i] > 0)
    def _():
        x = x_ref[...]
        gw = gw_ref[...]

        def expert(wgu, wd, col):
            gu = jnp.dot(x, wgu[0], preferred_element_type=F32)
            gate = gu[:, :d_exp]
            h = gate * jax.nn.sigmoid(gate) * gu[:, d_exp:] * gw[:, col:col + 1]
            return jnp.dot(h.astype(BF16), wd[0], preferred_element_type=F32)

        y_ref[...] = (expert(wgu_a, wd_a, 0) + expert(wgu_b, wd_b, 1)).astype(BF16)

    @pl.when(act_ref[i] == 0)
    def _():
        y_ref[...] = jnp.zeros_like(y_ref)


def _moe(xs, gws, wgu, wd, ea, eb, act):
    n_rows, d = xs.shape
    d_exp = wd.shape[1]
    n_tiles = n_rows // TM
    body = functools.partial(_moe_body, d_exp=d_exp)
    grid_spec = pltpu.PrefetchScalarGridSpec(
        num_scalar_prefetch=3,
        grid=(n_tiles,),
        in_specs=[
            pl.BlockSpec((TM, d), lambda i, ea, eb, act: (i, 0)),
            pl.BlockSpec((TM, 2), lambda i, ea, eb, act: (i, 0)),
            pl.BlockSpec((1, d, 2 * d_exp), lambda i, ea, eb, act: (ea[i], 0, 0)),
            pl.BlockSpec((1, d_exp, d), lambda i, ea, eb, act: (ea[i], 0, 0)),
            pl.BlockSpec((1, d, 2 * d_exp), lambda i, ea, eb, act: (eb[i], 0, 0)),
            pl.BlockSpec((1, d_exp, d), lambda i, ea, eb, act: (eb[i], 0, 0)),
        ],
        out_specs=pl.BlockSpec((TM, d), lambda i, ea, eb, act: (i, 0)),
    )
    return pl.pallas_call(
        body, grid_spec=grid_spec,
        out_shape=jax.ShapeDtypeStruct((n_rows, d), BF16),
        compiler_params=_cparams("arbitrary"),
    )(ea, eb, act, xs, gws, wgu, wd, wgu, wd)


def _dispatch(route, n_tiles):
    n = route.shape[1]
    bucket = route[0].astype(jnp.int32)
    counts = jnp.sum(jax.nn.one_hot(bucket, N_BUCKETS, dtype=jnp.int32), axis=0)
    padded = ((counts + TM - 1) // TM) * TM
    ends = jnp.cumsum(padded)
    off = ends - padded
    start = jnp.cumsum(counts) - counts
    order = jnp.argsort(bucket, stable=True).astype(jnp.int32)
    bs = bucket[order]
    pos_sorted = off[bs] + jnp.arange(n, dtype=jnp.int32) - start[bs]
    src = jnp.zeros((n_tiles * TM,), jnp.int32).at[pos_sorted].set(order)
    pos = jnp.zeros((n,), jnp.int32).at[order].set(pos_sorted)
    tile_start = jnp.arange(n_tiles, dtype=jnp.int32) * TM
    n_used = ends[-1] // TM
    active = tile_start < ends[-1]
    tb = jnp.searchsorted(ends, tile_start, side="right").astype(jnp.int32)
    last = jnp.searchsorted(ends, (n_used - 1) * TM, side="right").astype(jnp.int32)
    tb = jnp.where(active, tb, last)
    pair_lo = jnp.array([p[0] for p in PAIRS], jnp.int32)
    pair_hi = jnp.array([p[1] for p in PAIRS], jnp.int32)
    ea = (tb // len(PAIRS)) * GROUP_SIZE + pair_lo[tb % len(PAIRS)]
    eb = (tb // len(PAIRS)) * GROUP_SIZE + pair_hi[tb % len(PAIRS)]
    return src, pos, ea, eb, active.astype(jnp.int32)


def _rope_tables(seq, ctx_len):
    t = jnp.arange(seq)
    quarter = HEAD_DIM // 4
    inv_freq = ROPE_THETA ** (-jnp.arange(quarter, dtype=F32) / quarter)
    ang_r = (t // GRID_W).astype(F32)[:, None] * inv_freq
    ang_c = (t % GRID_W).astype(F32)[:, None] * inv_freq
    cos_h = jnp.concatenate([jnp.cos(ang_r)] * 2 + [jnp.cos(ang_c)] * 2, axis=-1)
    sin_h = jnp.concatenate([-jnp.sin(ang_r), jnp.sin(ang_r), -jnp.sin(ang_c), jnp.sin(ang_c)], axis=-1)
    cos_t = jnp.concatenate([cos_h, cos_h], axis=-1)
    sin_t = jnp.concatenate([sin_h, sin_h], axis=-1)
    cos_t = jnp.concatenate([cos_t, jnp.ones((ctx_len, LANES), F32)], axis=0)
    sin_t = jnp.concatenate([sin_t, jnp.zeros((ctx_len, LANES), F32)], axis=0)
    return cos_t, sin_t


def kernel(x, c, ctx, c_ctx, w_ada, b_ada, wqkv_a, wo_a, sink_a, wqkv_b, wo_b, lambda_b, subln_b, ln_attn_g, ln_attn_b, ln_ffn_g, ln_ffn_b, w_router, router_bias, w_gate, w_up, w_down):
    n_batch, seq, d = x.shape
    ctx_len = ctx.shape[1]
    depth = w_ada.shape[0]
    assert seq % TM == 0 and ctx_len == TM and seq >= TM + 2 * WINDOW
    assert n_batch + 1 <= ADA_ROWS and d == A_Q_HEADS * HEAD_DIM
    tiles_lat = seq // TM
    alpha = (2 * depth) ** 0.25
    q_cols = A_Q_HEADS * HEAD_DIM
    kv_cols = A_KV_HEADS * HEAD_DIM

    cc = jnp.concatenate([c, c_ctx[None, :], jnp.zeros((ADA_ROWS - n_batch - 1, d), F32)], axis=0)
    mods = _ada_modulation(cc, w_ada, b_ada).reshape(depth, ADA_ROWS, 1, 6 * d)
    cos_t, sin_t = _rope_tables(seq, ctx_len)

    xs = jnp.concatenate([x, ctx], axis=1).reshape(n_batch * (seq + ctx_len), d)
    wr_hi = w_router.astype(BF16)
    wr_lo = (w_router - wr_hi.astype(F32)).astype(BF16)
    wr2 = jnp.concatenate([wr_hi.T, wr_lo.T], axis=0)
    wrh = wr_hi.T
    rbias = router_bias.reshape(N_EXPERTS, 1)

    y = None
    mode = "all"
    for i in range(depth):
        last = i == depth - 1
        j = i // 2
        if i % 2 == 0:
            w = wqkv_a[j].astype(BF16)
            n_rope, dup_from = q_cols + kv_cols, q_cols
        else:
            w = wqkv_b[j].astype(BF16)
            n_rope, dup_from = 2 * q_cols, w.shape[1]
        outs = _lnqkv(xs, y, mods[i - 1] if i else None, ln_ffn_g[i - 1] if i else None,
                      ln_ffn_b[i - 1] if i else None, mods[i], w, cos_t, sin_t,
                      mode_in="all", n_batch=n_batch, tiles_lat=tiles_lat,
                      alpha=alpha, n_rope=n_rope, dup_from=dup_from)
        if i:
            xs, qkv = outs
        else:
            (qkv,) = outs
        if i % 2 == 0:
            att = _attn_a(qkv, sink_a[j], n_batch=n_batch, seq=seq, ctx_len=ctx_len, with_ctx=not last)
            wo = wo_a[j].astype(BF16)
        else:
            lam_init = 0.8 - 0.6 * math.exp(-0.3 * i)
            att = _attn_b(qkv, lambda_b[j], subln_b[j], n_batch=n_batch, seq=seq, ctx_len=ctx_len,
                          with_ctx=not last, lam_init=lam_init)
            wo = wo_b[j].astype(BF16)
        mode = "lat" if last else "all"
        xs, h2, route = _oproj(att, wo, xs, mods[i], ln_attn_g[i], ln_attn_b[i], wr2, wrh, rbias,
                               mode_in=mode, n_batch=n_batch, tiles_lat=tiles_lat, alpha=alpha)
        n_tok = h2.shape[0]
        n_tiles = n_tok // TM + N_BUCKETS
        src, pos, ea, eb, act = _dispatch(route, n_tiles)
        x_sorted = jnp.take(h2, src, axis=0)
        gw_sorted = jnp.take(route[1:3].T, src, axis=0)
        wgu = jnp.concatenate([w_gate[i], w_up[i]], axis=-1).astype(BF16)
        y_sorted = _moe(x_sorted, gw_sorted, wgu, w_down[i].astype(BF16), ea, eb, act)
        y = jnp.take(y_sorted, pos, axis=0)

    (out,) = _lnqkv(xs, y, mods[depth - 1], ln_ffn_g[depth - 1], ln_ffn_b[depth - 1], None, None, None, None,
                    mode_in="compact", n_batch=n_batch, tiles_lat=tiles_lat, alpha=alpha)
    return out.reshape(n_batch, seq, d)
```

```python
import functools
import math

import jax
import jax.numpy as jnp
from jax import lax
from jax.experimental import pallas as pl
from jax.experimental.pallas import tpu as pltpu

F32 = jnp.float32
BF16 = jnp.bfloat16

HEAD_DIM = 64
A_Q_HEADS = 16
A_KV_HEADS = 4
A_GROUP = A_Q_HEADS // A_KV_HEADS
GRID_W = 64
WINDOW = 128
N_EXPERTS = 16
N_GROUPS = 4
GROUP_SIZE = N_EXPERTS // N_GROUPS
ROPE_THETA = 10000.0
LN_EPS = 1e-6
SUBLN_EPS = 1e-5
NEG_INF = -1e30
ATTN_SCALE = HEAD_DIM ** -0.5

LANES = 128
TM = 256
ADA_ROWS = 24
VMEM_LIMIT = 56 * 1024 * 1024

PAIRS = ((0, 1), (0, 2), (0, 3), (1, 2), (1, 3), (2, 3))
N_BUCKETS = N_GROUPS * len(PAIRS)

BUCKET_ROWS = 32
ROUTE_ROWS = 8
HIGH16 = -65536

NT_DIMS = (((1,), (1,)), ((), ()))


def _cparams(*sem):
    return pltpu.CompilerParams(dimension_semantics=sem, vmem_limit_bytes=VMEM_LIMIT)


def _ada_body(c_ref, w_ref, b_ref, o_ref):
    c = c_ref[...]
    sc = c * jax.nn.sigmoid(c)
    o_ref[0] = jnp.dot(sc, w_ref[0], precision=lax.Precision.HIGHEST,
                       preferred_element_type=F32) + b_ref[0]


def _ada_modulation(cc, w_ada, b_ada):
    depth, d, d6 = w_ada.shape
    tn = 1536
    return pl.pallas_call(
        _ada_body,
        grid=(depth, d6 // tn),
        in_specs=[
            pl.BlockSpec((ADA_ROWS, d), lambda l, n: (0, 0)),
            pl.BlockSpec((1, d, tn), lambda l, n: (l, 0, n)),
            pl.BlockSpec((1, 1, tn), lambda l, n: (l, 0, n)),
        ],
        out_specs=pl.BlockSpec((1, ADA_ROWS, tn), lambda l, n: (l, 0, n)),
        out_shape=jax.ShapeDtypeStruct((depth, ADA_ROWS, d6), F32),
        compiler_params=_cparams("arbitrary", "arbitrary"),
    )(cc, w_ada, b_ada.reshape(depth, 1, d6))


def _tile_maps(mode, tiles_lat, n_batch):
    tpb = tiles_lat + 1
    if mode == "all":
        blk = lambda i: i
        modrow = lambda i: jnp.where(i % tpb == tiles_lat, n_batch, i // tpb)
        pos = lambda i: i % tpb
    elif mode == "lat":
        blk = lambda i: (i // tiles_lat) * tpb + i % tiles_lat
        modrow = lambda i: i // tiles_lat
        pos = lambda i: i % tiles_lat
    else:
        blk = lambda i: i
        modrow = lambda i: i // tiles_lat
        pos = lambda i: i % tiles_lat
    return blk, modrow, pos


def _layer_norm(z, g, b):
    mu = jnp.mean(z, axis=-1, keepdims=True)
    zc = z - mu
    var = jnp.mean(zc * zc, axis=-1, keepdims=True)
    return zc * lax.rsqrt(var + LN_EPS) * g + b


def _pack_bf16_pairs(v):
    half = v.shape[1] // 2
    bits = pltpu.bitcast(v.astype(BF16).astype(F32), jnp.int32)
    return lax.shift_right_logical(bits[:, :half], 16) | (bits[:, half:] & HIGH16)


def _unpack_bf16_pairs(w):
    return jnp.concatenate([pltpu.bitcast(lax.shift_left(w, 16), F32),
                            pltpu.bitcast(w & HIGH16, F32)], axis=1)


def _row_dma_wait(hbm_ref, buf_ref, sem):
    pltpu.make_async_copy(hbm_ref.at[pl.ds(0, TM)], buf_ref, sem).wait()


def _lnqkv_body(*refs, has_ln, has_qkv, alpha, n_rope, dup_from, n_out):
    refs = list(refs)
    x_ref = refs.pop(0)
    if has_ln:
        pos_ref, posn_ref, ys_ref, g2_ref, lng_ref, lnb_ref = refs[:6]
        refs = refs[6:]
        ybuf, ysem = refs[-2:]
        refs = refs[:-2]
    if has_qkv:
        sc_ref, sh_ref, w_ref, cos_ref, sin_ref = refs[:5]
        refs = refs[5:]
    x = x_ref[...]
    if has_ln:
        xo_ref = refs.pop(0)
        i = pl.program_id(0)
        slot = i % 2

        def gather(p_ref, s):
            def body(r, carry):
                pltpu.make_async_copy(ys_ref.at[pl.ds(p_ref[0, 0, r], 1)], ybuf.at[s, pl.ds(r, 1)],
                                      ysem.at[s]).start()
                return carry
            lax.fori_loop(0, TM, body, 0, unroll=8)

        @pl.when(i == 0)
        def _():
            gather(pos_ref, 0)

        @pl.when(i + 1 < pl.num_programs(0))
        def _():
            gather(posn_ref, 1 - slot)

        _row_dma_wait(ys_ref, ybuf.at[slot], ysem.at[slot])
        z = alpha * x + g2_ref[0] * _unpack_bf16_pairs(ybuf[slot])
        x = _layer_norm(z, lng_ref[...], lnb_ref[...])
        xo_ref[...] = x
    if not has_qkv:
        return
    qkv_ref = refs.pop(0)
    h = (x * (1.0 + sc_ref[0]) + sh_ref[0]).astype(BF16)
    r = jnp.dot(h, w_ref[...], preferred_element_type=F32)
    cos = cos_ref[...]
    sin = sin_ref[...]
    lane = lax.broadcasted_iota(jnp.int32, (TM, LANES), 1)
    first16 = (lane % 32) < 16
    low_half = lane < HEAD_DIM
    q_cols = A_Q_HEADS * HEAD_DIM
    dst = 0
    for c in range(n_out // LANES):
        seg = r[:, c * LANES:(c + 1) * LANES]
        col = c * LANES
        if col < q_cols:
            seg = seg * ATTN_SCALE
        if col < n_rope:
            rot = jnp.where(first16, pltpu.roll(seg, LANES - 16, 1), pltpu.roll(seg, 16, 1))
            seg = seg * cos + rot * sin
        if col >= dup_from:
            swapped = pltpu.roll(seg, HEAD_DIM, 1)
            qkv_ref[:, dst:dst + LANES] = jnp.where(low_half, seg, swapped).astype(BF16)
            qkv_ref[:, dst + LANES:dst + 2 * LANES] = jnp.where(low_half, swapped, seg).astype(BF16)
            dst += 2 * LANES
        else:
            qkv_ref[:, dst:dst + LANES] = seg.astype(BF16)
            dst += LANES


def _lnqkv(x, y, mod_ln, ln_g, ln_b, mod_qkv, w, cos_t, sin_t, *, mode_in, n_batch, tiles_lat,
           alpha, n_rope=0, dup_from=0):
    d = x.shape[1]
    has_ln = y is not None
    has_qkv = w is not None
    n_tiles = (n_batch * (tiles_lat + 1)) if mode_in == "all" else n_batch * tiles_lat
    blk, modrow, pos = _tile_maps(mode_in, tiles_lat, n_batch)
    row = pl.BlockSpec((TM, d), lambda i: (blk(i), 0))
    vec = pl.BlockSpec((1, d), lambda i: (0, 0))

    def modspec(chunk):
        return pl.BlockSpec((1, 1, d), lambda i: (modrow(i), 0, chunk))

    args, in_specs, out_specs, out_shape, scratch = [x], [row], [], [], []
    n_out = 0
    if has_ln:
        ys, pos3 = y
        args += [pos3, pos3, ys, mod_ln, ln_g.reshape(1, d), ln_b.reshape(1, d)]
        in_specs += [pl.BlockSpec((1, 1, TM), lambda i: (i, 0, 0), memory_space=pltpu.SMEM),
                     pl.BlockSpec((1, 1, TM), lambda i: (jnp.minimum(i + 1, n_tiles - 1), 0, 0),
                                  memory_space=pltpu.SMEM),
                     pl.BlockSpec(memory_space=pl.ANY), modspec(5), vec, vec]
        out_specs.append(row)
        out_shape.append(jax.ShapeDtypeStruct(x.shape, F32))
        scratch = [pltpu.VMEM((2, TM, ys.shape[1]), jnp.int32), pltpu.SemaphoreType.DMA((2,))]
    if has_qkv:
        n_out = w.shape[1]
        n_store = n_out + (n_out - dup_from)
        args += [mod_qkv, mod_qkv, w, cos_t, sin_t]
        in_specs += [modspec(1), modspec(0),
                     pl.BlockSpec((d, n_out), lambda i: (0, 0)),
                     pl.BlockSpec((TM, LANES), lambda i: (pos(i), 0)),
                     pl.BlockSpec((TM, LANES), lambda i: (pos(i), 0))]
        out_specs.append(pl.BlockSpec((TM, n_store), lambda i: (blk(i), 0)))
        out_shape.append(jax.ShapeDtypeStruct((x.shape[0], n_store), BF16))
    body = functools.partial(_lnqkv_body, has_ln=has_ln, has_qkv=has_qkv, alpha=alpha,
                             n_rope=n_rope, dup_from=dup_from, n_out=n_out)
    return pl.pallas_call(
        body, grid=(n_tiles,), in_specs=in_specs, out_specs=out_specs, out_shape=out_shape,
        scratch_shapes=scratch, compiler_params=_cparams("arbitrary"),
    )(*args)


def _softmax_parts(scores, sink):
    m = None
    for s in scores:
        sm = jnp.max(s, axis=-1, keepdims=True)
        m = sm if m is None else jnp.maximum(m, sm)
    if sink is not None:
        m = jnp.maximum(m, sink)
    probs = [jnp.exp(s - m) for s in scores]
    denom = None
    for p in probs:
        ps = jnp.sum(p, axis=-1, keepdims=True)
        denom = ps if denom is None else denom + ps
    if sink is not None:
        denom = denom + jnp.exp(sink - m)
    return probs, 1.0 / denom


def _attn_a_body(sink_ref, q_ref, k_ref, v_ref, o_ref, *, seq, tiles_lat):
    qi = pl.program_id(1)
    lane = lax.broadcasted_iota(jnp.int32, (TM, LANES), 1)
    low_half = lane < HEAD_DIM
    span = TM + 2 * WINDOW

    def run(parts):
        for grp in range(A_Q_HEADS // 2):
            q2 = q_ref[:, grp * LANES:(grp + 1) * LANES]
            h = (2 * grp) // A_GROUP
            outs = []
            for half in range(2):
                keep = low_half if half == 0 else jnp.logical_not(low_half)
                qh = jnp.where(keep, q2, jnp.zeros_like(q2))
                scores = []
                for (r0, nr, mask) in parts:
                    kk = k_ref[pl.ds(r0, nr), h * LANES:(h + 1) * LANES]
                    s = lax.dot_general(qh, kk, NT_DIMS, preferred_element_type=F32)
                    if mask is not None:
                        s = jnp.where(mask, s, NEG_INF)
                    scores.append(s)
                probs, inv = _softmax_parts(scores, sink_ref[2 * grp + half])
                acc = None
                for p, (r0, nr, _) in zip(probs, parts):
                    vv = v_ref[pl.ds(r0, nr), h * LANES:(h + 1) * LANES]
                    o = jnp.dot(p.astype(BF16), vv, preferred_element_type=F32)
                    acc = o if acc is None else acc + o
                outs.append(acc * inv)
            o_ref[:, grp * LANES:(grp + 1) * LANES] = jnp.where(low_half, outs[0], outs[1]).astype(BF16)

    @pl.when(qi < tiles_lat)
    def _():
        start = pl.multiple_of(jnp.clip(qi * TM - WINDOW, 0, seq - span), WINDOW)
        qpos = qi * TM + lax.broadcasted_iota(jnp.int32, (TM, span), 0)
        kpos = start + lax.broadcasted_iota(jnp.int32, (TM, span), 1)
        band = jnp.abs(kpos - qpos) <= WINDOW
        run([(start, span, band), (seq, k_ref.shape[0] - seq, None)])

    @pl.when(qi >= tiles_lat)
    def _():
        run([(seq, k_ref.shape[0] - seq, None)])


def _attn_a(qkv, sink, *, n_batch, seq, ctx_len, with_ctx):
    tiles_lat = seq // TM
    tpb = tiles_lat + ctx_len // TM
    rows_b = seq + ctx_len
    q_cols = A_Q_HEADS * HEAD_DIM
    kv_cols = 2 * A_KV_HEADS * HEAD_DIM
    nq = tpb if with_ctx else tiles_lat
    body = functools.partial(_attn_a_body, seq=seq, tiles_lat=tiles_lat)
    return pl.pallas_call(
        body,
        grid=(n_batch, nq),
        in_specs=[
            pl.BlockSpec(memory_space=pltpu.SMEM),
            pl.BlockSpec((TM, q_cols), lambda b, i: (b * tpb + i, 0)),
            pl.BlockSpec((rows_b, kv_cols), lambda b, i: (b, q_cols // kv_cols)),
            pl.BlockSpec((rows_b, kv_cols), lambda b, i: (b, q_cols // kv_cols + 1)),
        ],
        out_specs=pl.BlockSpec((TM, q_cols), lambda b, i: (b * nq + i, 0)),
        out_shape=jax.ShapeDtypeStruct((n_batch * nq * TM, q_cols), BF16),
        compiler_params=_cparams("arbitrary", "arbitrary"),
    )(sink, qkv, qkv, qkv)


def _attn_b_body(lam_ref, g_ref, q_ref, k_ref, v_ref, o_ref, *, seq, tiles_lat, lam_init):
    qi = pl.program_id(2)
    lp = lam_ref[...]
    lam = (jnp.exp(jnp.sum(lp[0:1] * lp[1:2], axis=-1, keepdims=True))
           - jnp.exp(jnp.sum(lp[2:3] * lp[3:4], axis=-1, keepdims=True)) + lam_init)
    lane = lax.broadcasted_iota(jnp.int32, (TM, LANES), 1)
    low_half = lane < HEAD_DIM
    q = q_ref[...]
    q1 = jnp.where(low_half, q, jnp.zeros_like(q))
    q2 = jnp.where(low_half, jnp.zeros_like(q), q)

    def run(r0, nr):
        kk = k_ref[pl.ds(r0, nr), :]
        vv = v_ref[pl.ds(r0, nr), :]
        s1 = lax.dot_general(q1, kk, NT_DIMS, preferred_element_type=F32)
        s2 = lax.dot_general(q2, kk, NT_DIMS, preferred_element_type=F32)
        (p1,), inv1 = _softmax_parts([s1], None)
        (p2,), inv2 = _softmax_parts([s2], None)
        a = p1 * inv1 - p2 * (lam * inv2)
        o = jnp.dot(a.astype(BF16), vv, preferred_element_type=F32)
        ms = jnp.mean(o * o, axis=-1, keepdims=True)
        o = o * lax.rsqrt(ms + SUBLN_EPS) * g_ref[...] * (1.0 - lam_init)
        o_ref[...] = o.astype(BF16)

    @pl.when(qi < tiles_lat)
    def _():
        run(0, k_ref.shape[0])

    @pl.when(qi >= tiles_lat)
    def _():
        run(seq, k_ref.shape[0] - seq)


def _attn_b(qkv, lam_params, subln_g, *, n_batch, seq, ctx_len, with_ctx, lam_init):
    tiles_lat = seq // TM
    tpb = tiles_lat + ctx_len // TM
    rows_b = seq + ctx_len
    n_heads = qkv.shape[1] // (3 * LANES)
    nq = tpb if with_ctx else tiles_lat
    body = functools.partial(_attn_b_body, seq=seq, tiles_lat=tiles_lat, lam_init=lam_init)
    return pl.pallas_call(
        body,
        grid=(n_batch, n_heads, nq),
        in_specs=[
            pl.BlockSpec((4, HEAD_DIM), lambda b, h, i: (0, 0)),
            pl.BlockSpec((1, LANES), lambda b, h, i: (0, 0)),
            pl.BlockSpec((TM, LANES), lambda b, h, i: (b * tpb + i, h)),
            pl.BlockSpec((rows_b, LANES), lambda b, h, i: (b, n_heads + h)),
            pl.BlockSpec((rows_b, LANES), lambda b, h, i: (b, 2 * n_heads + h)),
        ],
        out_specs=pl.BlockSpec((TM, LANES), lambda b, h, i: (b * nq + i, h)),
        out_shape=jax.ShapeDtypeStruct((n_batch * nq * TM, n_heads * LANES), BF16),
        compiler_params=_cparams("arbitrary", "arbitrary", "arbitrary"),
    )(lam_params, subln_g.reshape(1, LANES), qkv, qkv, qkv)


def _route_rows(logits, bias):
    s = jax.nn.sigmoid(logits)
    biased = s + bias
    b = [biased[e:e + 1, :] for e in range(N_EXPERTS)]
    u = [s[e:e + 1, :] for e in range(N_EXPERTS)]
    gscore = []
    for g in range(N_GROUPS):
        best = None
        for (i, j) in PAIRS:
            t = b[GROUP_SIZE * g + i] + b[GROUP_SIZE * g + j]
            best = t if best is None else jnp.maximum(best, t)
        gscore.append(best)
    gsel = jnp.zeros(gscore[0].shape, jnp.int32)
    gbest = gscore[0]
    for g in range(1, N_GROUPS):
        better = gscore[g] > gbest
        gsel = jnp.where(better, g, gsel)
        gbest = jnp.where(better, gscore[g], gbest)

    def pick(rows, k):
        out = rows[GROUP_SIZE * (N_GROUPS - 1) + k]
        for g in range(N_GROUPS - 2, -1, -1):
            out = jnp.where(gsel == g, rows[GROUP_SIZE * g + k], out)
        return out

    v = [pick(b, k) for k in range(GROUP_SIZE)]
    w = [pick(u, k) for k in range(GROUP_SIZE)]
    sel = []
    for k in range(GROUP_SIZE):
        cnt = jnp.zeros(gsel.shape, jnp.int32)
        for j in range(GROUP_SIZE):
            if j == k:
                continue
            beats = (v[j] >= v[k]) if j < k else (v[j] > v[k])
            cnt = cnt + jnp.where(beats, 1, 0)
        sel.append(cnt < 2)
    pidx = jnp.zeros(gsel.shape, jnp.int32)
    u_lo = jnp.zeros(gbest.shape, F32)
    u_hi = jnp.zeros(gbest.shape, F32)
    for idx, (i, j) in enumerate(PAIRS):
        both = jnp.logical_and(sel[i], sel[j])
        pidx = jnp.where(both, idx, pidx)
        u_lo = jnp.where(both, w[i], u_lo)
        u_hi = jnp.where(both, w[j], u_hi)
    tot = u_lo + u_hi
    bucket = (gsel * len(PAIRS) + pidx).astype(F32)
    return bucket, u_lo / tot, u_hi / tot


def _oproj_body(a_ref, wo_ref, x_ref, g1_ref, lng_ref, lnb_ref, sc_ref, sh_ref, wr2_ref, wrh_ref, rb_ref,
                xo_ref, h2p_ref, r_ref, cnt_ref, *, alpha):
    i = pl.program_id(0)
    d = x_ref.shape[1]
    al = jnp.dot(a_ref[...], wo_ref[...], preferred_element_type=F32)
    z = alpha * x_ref[...] + g1_ref[0] * al
    xn = _layer_norm(z, lng_ref[...], lnb_ref[...])
    xo_ref[...] = xn
    h2 = xn * (1.0 + sc_ref[0]) + sh_ref[0]
    hi = h2.astype(BF16)
    lo = (h2 - hi.astype(F32)).astype(BF16)
    l2 = lax.dot_general(wr2_ref[...], hi, NT_DIMS, preferred_element_type=F32)
    l1 = lax.dot_general(wrh_ref[...], lo, NT_DIMS, preferred_element_type=F32)
    logits = l2[:N_EXPERTS] + l2[N_EXPERTS:] + l1
    bucket, w_lo, w_hi = _route_rows(logits, rb_ref[...])

    @pl.when(i == 0)
    def _():
        cnt_ref[...] = jnp.zeros_like(cnt_ref)

    onehot = jnp.where(lax.broadcasted_iota(jnp.int32, (BUCKET_ROWS, TM), 0).astype(F32) == bucket, 1.0, 0.0)
    earlier = (lax.broadcasted_iota(jnp.int32, (TM, TM), 0) < lax.broadcasted_iota(jnp.int32, (TM, TM), 1))
    before = jnp.dot(onehot.astype(BF16), jnp.where(earlier, 1.0, 0.0).astype(BF16),
                     preferred_element_type=F32)
    cnt = cnt_ref[...]
    rank = jnp.sum(onehot * (before + cnt[:, 0:1]), axis=0, keepdims=True)
    cnt_ref[...] = cnt + jnp.sum(onehot, axis=1, keepdims=True)

    r_ref[0:1, :] = bucket
    r_ref[1:2, :] = w_lo
    r_ref[2:3, :] = w_hi
    r_ref[3:4, :] = rank
    r_ref[4:ROUTE_ROWS, :] = jnp.zeros((ROUTE_ROWS - 4, TM), F32)
    rec = jnp.concatenate([r_ref[...], jnp.zeros((LANES - ROUTE_ROWS, TM), F32)], axis=0).T
    h2p_ref[:, :d // 2] = _pack_bf16_pairs(h2)
    h2p_ref[:, d // 2:] = pltpu.bitcast(rec, jnp.int32)


def _oproj(a, wo, x, mod, ln_g, ln_b, wr2, wrh, rbias, *, mode_in, n_batch, tiles_lat, alpha):
    d = x.shape[1]
    compact = mode_in == "lat"
    n_tiles = n_batch * tiles_lat if compact else n_batch * (tiles_lat + 1)
    blk, modrow, _ = _tile_maps(mode_in, tiles_lat, n_batch)
    row_in = pl.BlockSpec((TM, d), lambda i: (blk(i), 0))
    row_out = pl.BlockSpec((TM, d), lambda i: (i, 0))
    vec = pl.BlockSpec((1, d), lambda i: (0, 0))

    def modspec(chunk):
        return pl.BlockSpec((1, 1, d), lambda i: (modrow(i), 0, chunk))

    body = functools.partial(_oproj_body, alpha=alpha)
    return pl.pallas_call(
        body,
        grid=(n_tiles,),
        in_specs=[row_out, pl.BlockSpec((d, d), lambda i: (0, 0)), row_in, modspec(2), vec, vec,
                  modspec(4), modspec(3),
                  pl.BlockSpec((2 * N_EXPERTS, d), lambda i: (0, 0)),
                  pl.BlockSpec((N_EXPERTS, d), lambda i: (0, 0)),
                  pl.BlockSpec((N_EXPERTS, 1), lambda i: (0, 0))],
        out_specs=[row_out, pl.BlockSpec((TM, d // 2 + LANES), lambda i: (i, 0)),
                   pl.BlockSpec((ROUTE_ROWS, TM), lambda i: (0, i)),
                   pl.BlockSpec((BUCKET_ROWS, LANES), lambda i: (0, 0))],
        out_shape=[jax.ShapeDtypeStruct((n_tiles * TM, d), F32),
                   jax.ShapeDtypeStruct((n_tiles * TM, d // 2 + LANES), jnp.int32),
                   jax.ShapeDtypeStruct((ROUTE_ROWS, n_tiles * TM), F32),
                   jax.ShapeDtypeStruct((BUCKET_ROWS, LANES), F32)],
        compiler_params=_cparams("arbitrary"),
    )(a, wo, x, mod, ln_g.reshape(1, d), ln_b.reshape(1, d), mod, mod, wr2, wrh, rbias)


def _scatter_body(pos_ref, h_ref, init_ref, xs_ref, sem):
    del init_ref
    i = pl.program_id(0)
    slot = i % 2

    def body(r, carry):
        pltpu.make_async_copy(h_ref.at[pl.ds(i * TM + r, 1)], xs_ref.at[pl.ds(pos_ref[0, 0, r], 1)],
                              sem.at[slot]).start()
        return carry
    lax.fori_loop(0, TM, body, 0, unroll=8)

    @pl.when(i > 0)
    def _():
        _row_dma_wait(h_ref, xs_ref.at[pl.ds(0, TM)], sem.at[1 - slot])

    @pl.when(i == pl.num_programs(0) - 1)
    def _():
        _row_dma_wait(h_ref, xs_ref.at[pl.ds(0, TM)], sem.at[slot])


def _scatter_rows(h2p, pos3, n_rows_sorted):
    n_tiles = pos3.shape[0]
    width = h2p.shape[1]
    return pl.pallas_call(
        _scatter_body,
        grid=(n_tiles,),
        in_specs=[pl.BlockSpec((1, 1, TM), lambda i: (i, 0, 0), memory_space=pltpu.SMEM),
                  pl.BlockSpec(memory_space=pl.ANY), pl.BlockSpec(memory_space=pl.ANY)],
        out_specs=pl.BlockSpec(memory_space=pl.ANY),
        out_shape=jax.ShapeDtypeStruct((n_rows_sorted, width), jnp.int32),
        scratch_shapes=[pltpu.SemaphoreType.DMA((2,))],
        input_output_aliases={2: 0},
        compiler_params=_cparams("arbitrary"),
    )(pos3, h2p, jnp.zeros((n_rows_sorted, width), jnp.int32))


def _moe_body(ea_ref, eb_ref, act_ref, x_ref, wgu_a, wd_a, wgu_b, wd_b, y_ref, *, d_exp):
    i = pl.program_id(0)
    half = wd_a.shape[2] // 2

    @pl.when(act_ref[i] > 0)
    def _():
        xw = x_ref[...]
        x = _unpack_bf16_pairs(xw[:, :half]).astype(BF16)
        rec = pltpu.bitcast(xw[:, half:], F32)

        def expert(wgu, wd, col):
            gu = jnp.dot(x, wgu[0], preferred_element_type=F32)
            gate = gu[:, :d_exp]
            h = gate * jax.nn.sigmoid(gate) * gu[:, d_exp:] * rec[:, col:col + 1]
            return jnp.dot(h.astype(BF16), wd[0], preferred_element_type=F32)

        y_ref[...] = _pack_bf16_pairs(expert(wgu_a, wd_a, 1) + expert(wgu_b, wd_b, 2))

    @pl.when(act_ref[i] == 0)
    def _():
        y_ref[...] = jnp.zeros_like(y_ref)


def _moe(xs, wgu, wd, ea, eb, act):
    n_rows, width = xs.shape
    d_exp, d = wd.shape[1:]
    n_tiles = n_rows // TM
    body = functools.partial(_moe_body, d_exp=d_exp)
    grid_spec = pltpu.PrefetchScalarGridSpec(
        num_scalar_prefetch=3,
        grid=(n_tiles,),
        in_specs=[
            pl.BlockSpec((TM, width), lambda i, ea, eb, act: (i, 0)),
            pl.BlockSpec((1, d, 2 * d_exp), lambda i, ea, eb, act: (ea[i], 0, 0)),
            pl.BlockSpec((1, d_exp, d), lambda i, ea, eb, act: (ea[i], 0, 0)),
            pl.BlockSpec((1, d, 2 * d_exp), lambda i, ea, eb, act: (eb[i], 0, 0)),
            pl.BlockSpec((1, d_exp, d), lambda i, ea, eb, act: (eb[i], 0, 0)),
        ],
        out_specs=pl.BlockSpec((TM, d // 2), lambda i, ea, eb, act: (i, 0)),
    )
    return pl.pallas_call(
        body, grid_spec=grid_spec,
        out_shape=jax.ShapeDtypeStruct((n_rows, d // 2), jnp.int32),
        compiler_params=_cparams("arbitrary"),
    )(ea, eb, act, xs, wgu, wd, wgu, wd)


def _dispatch(route, counts, n_tiles):
    n = route.shape[1]
    bucket = route[0].astype(jnp.int32)
    rank = route[3].astype(jnp.int32)
    counts = counts[:N_BUCKETS, 0].astype(jnp.int32)
    padded = ((counts + TM - 1) // TM) * TM
    ends = jnp.cumsum(padded)
    off = ends - padded

    def lookup(idx, table):
        return jnp.sum(jnp.where(idx[:, None] == jnp.arange(table.shape[0])[None, :], table[None, :], 0), axis=1)

    pos = lookup(bucket, off) + rank
    tile_start = jnp.arange(n_tiles, dtype=jnp.int32) * TM
    active = tile_start < ends[-1]
    last_start = jnp.maximum(ends[-1] - TM, 0)
    tb = jnp.sum((jnp.where(active, tile_start, last_start)[:, None] >= ends[None, :]).astype(jnp.int32), axis=1)
    tb = jnp.minimum(tb, N_BUCKETS - 1)
    ea = (tb // len(PAIRS)) * GROUP_SIZE + lookup(tb % len(PAIRS), jnp.array([p[0] for p in PAIRS], jnp.int32))
    eb = (tb // len(PAIRS)) * GROUP_SIZE + lookup(tb % len(PAIRS), jnp.array([p[1] for p in PAIRS], jnp.int32))
    return pos.reshape(n // TM, 1, TM), ea, eb, active.astype(jnp.int32)


def _rope_tables(seq, ctx_len):
    t = jnp.arange(seq)
    quarter = HEAD_DIM // 4
    inv_freq = ROPE_THETA ** (-jnp.arange(quarter, dtype=F32) / quarter)
    ang_r = (t // GRID_W).astype(F32)[:, None] * inv_freq
    ang_c = (t % GRID_W).astype(F32)[:, None] * inv_freq
    cos_h = jnp.concatenate([jnp.cos(ang_r)] * 2 + [jnp.cos(ang_c)] * 2, axis=-1)
    sin_h = jnp.concatenate([-jnp.sin(ang_r), jnp.sin(ang_r), -jnp.sin(ang_c), jnp.sin(ang_c)], axis=-1)
    cos_t = jnp.concatenate([cos_h, cos_h], axis=-1)
    sin_t = jnp.concatenate([sin_h, sin_h], axis=-1)
    cos_t = jnp.concatenate([cos_t, jnp.ones((ctx_len, LANES), F32)], axis=0)
    sin_t = jnp.concatenate([sin_t, jnp.zeros((ctx_len, LANES), F32)], axis=0)
    return cos_t, sin_t


def kernel(x, c, ctx, c_ctx, w_ada, b_ada, wqkv_a, wo_a, sink_a, wqkv_b, wo_b, lambda_b, subln_b, ln_attn_g, ln_attn_b, ln_ffn_g, ln_ffn_b, w_router, router_bias, w_gate, w_up, w_down):
    n_batch, seq, d = x.shape
    ctx_len = ctx.shape[1]
    depth = w_ada.shape[0]
    assert seq % TM == 0 and ctx_len == TM and seq >= TM + 2 * WINDOW
    assert n_batch + 1 <= ADA_ROWS and d == A_Q_HEADS * HEAD_DIM
    tiles_lat = seq // TM
    alpha = (2 * depth) ** 0.25
    q_cols = A_Q_HEADS * HEAD_DIM
    kv_cols = A_KV_HEADS * HEAD_DIM

    cc = jnp.concatenate([c, c_ctx[None, :], jnp.zeros((ADA_ROWS - n_batch - 1, d), F32)], axis=0)
    mods = _ada_modulation(cc, w_ada, b_ada).reshape(depth, ADA_ROWS, 1, 6 * d)
    cos_t, sin_t = _rope_tables(seq, ctx_len)

    xs = jnp.concatenate([x, ctx], axis=1).reshape(n_batch * (seq + ctx_len), d)
    wr_hi = w_router.astype(BF16)
    wr_lo = (w_router - wr_hi.astype(F32)).astype(BF16)
    wr2 = jnp.concatenate([wr_hi.T, wr_lo.T], axis=0)
    wrh = wr_hi.T
    rbias = router_bias.reshape(N_EXPERTS, 1)

    y = None
    mode = "all"
    for i in range(depth):
        last = i == depth - 1
        j = i // 2
        if i % 2 == 0:
            w = wqkv_a[j].astype(BF16)
            n_rope, dup_from = q_cols + kv_cols, q_cols
        else:
            w = wqkv_b[j].astype(BF16)
            n_rope, dup_from = 2 * q_cols, w.shape[1]
        outs = _lnqkv(xs, y, mods[i - 1] if i else None, ln_ffn_g[i - 1] if i else None,
                      ln_ffn_b[i - 1] if i else None, mods[i], w, cos_t, sin_t,
                      mode_in="all", n_batch=n_batch, tiles_lat=tiles_lat,
                      alpha=alpha, n_rope=n_rope, dup_from=dup_from)
        if i:
            xs, qkv = outs
        else:
            (qkv,) = outs
        if i % 2 == 0:
            att = _attn_a(qkv, sink_a[j], n_batch=n_batch, seq=seq, ctx_len=ctx_len, with_ctx=not last)
            wo = wo_a[j].astype(BF16)
        else:
            lam_init = 0.8 - 0.6 * math.exp(-0.3 * i)
            att = _attn_b(qkv, lambda_b[j], subln_b[j], n_batch=n_batch, seq=seq, ctx_len=ctx_len,
                          with_ctx=not last, lam_init=lam_init)
            wo = wo_b[j].astype(BF16)
        mode = "lat" if last else "all"
        xs, h2p, route, counts = _oproj(att, wo, xs, mods[i], ln_attn_g[i], ln_attn_b[i], wr2, wrh, rbias,
                                        mode_in=mode, n_batch=n_batch, tiles_lat=tiles_lat, alpha=alpha)
        n_tiles = h2p.shape[0] // TM + N_BUCKETS
        pos3, ea, eb, act = _dispatch(route, counts, n_tiles)
        x_sorted = _scatter_rows(h2p, pos3, n_tiles * TM)
        wgu = jnp.concatenate([w_gate[i], w_up[i]], axis=-1).astype(BF16)
        y = (_moe(x_sorted, wgu, w_down[i].astype(BF16), ea, eb, act), pos3)

    (out,) = _lnqkv(xs, y, mods[depth - 1], ln_ffn_g[depth - 1], ln_ffn_b[depth - 1], None, None, None, None,
                    mode_in="compact", n_batch=n_batch, tiles_lat=tiles_lat, alpha=alpha)
    return out.reshape(n_batch, seq, d)
```

```python
import functools
import math

import jax
import jax.numpy as jnp
from jax import lax
from jax.experimental import pallas as pl
from jax.experimental.pallas import tpu as pltpu

F32 = jnp.float32
BF16 = jnp.bfloat16

HEAD_DIM = 64
A_Q_HEADS = 16
A_KV_HEADS = 4
A_GROUP = A_Q_HEADS // A_KV_HEADS
GRID_W = 64
WINDOW = 128
N_EXPERTS = 16
N_GROUPS = 4
GROUP_SIZE = N_EXPERTS // N_GROUPS
ROPE_THETA = 10000.0
LN_EPS = 1e-6
SUBLN_EPS = 1e-5
NEG_INF = -1e30
ATTN_SCALE = HEAD_DIM ** -0.5

LANES = 128
TM = 256
STRIP = 256
DIFF_HEADS_PER_STEP = 2
ADA_ROWS = 24
VMEM_LIMIT = 56 * 1024 * 1024

PAIRS = ((0, 1), (0, 2), (0, 3), (1, 2), (1, 3), (2, 3))
N_BUCKETS = N_GROUPS * len(PAIRS)

BUCKET_ROWS = 32
ROUTE_ROWS = 8
HIGH16 = -65536

NT_DIMS = (((1,), (1,)), ((), ()))


def _cparams(*sem):
    return pltpu.CompilerParams(dimension_semantics=sem, vmem_limit_bytes=VMEM_LIMIT)


def _ada_body(c_ref, w_ref, b_ref, o_ref):
    c = c_ref[...]
    sc = c * jax.nn.sigmoid(c)
    o_ref[0] = jnp.dot(sc, w_ref[0], precision=lax.Precision.HIGHEST,
                       preferred_element_type=F32) + b_ref[0]


def _ada_modulation(cc, w_ada, b_ada):
    depth, d, d6 = w_ada.shape
    tn = 1536
    return pl.pallas_call(
        _ada_body,
        grid=(depth, d6 // tn),
        in_specs=[
            pl.BlockSpec((ADA_ROWS, d), lambda l, n: (0, 0)),
            pl.BlockSpec((1, d, tn), lambda l, n: (l, 0, n)),
            pl.BlockSpec((1, 1, tn), lambda l, n: (l, 0, n)),
        ],
        out_specs=pl.BlockSpec((1, ADA_ROWS, tn), lambda l, n: (l, 0, n)),
        out_shape=jax.ShapeDtypeStruct((depth, ADA_ROWS, d6), F32),
        compiler_params=_cparams("arbitrary", "arbitrary"),
        name="ada_mod",
    )(cc, w_ada, b_ada.reshape(depth, 1, d6))


def _tile_maps(mode, tiles_lat, n_batch):
    tpb = tiles_lat + 1
    if mode == "all":
        blk = lambda i: i
        modrow = lambda i: jnp.where(i % tpb == tiles_lat, n_batch, i // tpb)
        pos = lambda i: i % tpb
    elif mode == "lat":
        blk = lambda i: (i // tiles_lat) * tpb + i % tiles_lat
        modrow = lambda i: i // tiles_lat
        pos = lambda i: i % tiles_lat
    else:
        blk = lambda i: i
        modrow = lambda i: i // tiles_lat
        pos = lambda i: i % tiles_lat
    return blk, modrow, pos


def _layer_norm(z, g, b):
    mu = jnp.mean(z, axis=-1, keepdims=True)
    zc = z - mu
    var = jnp.mean(zc * zc, axis=-1, keepdims=True)
    return zc * lax.rsqrt(var + LN_EPS) * g + b


def _pack_bf16_pairs(v):
    half = v.shape[1] // 2
    bits = pltpu.bitcast(v.astype(BF16).astype(F32), jnp.int32)
    return lax.shift_right_logical(bits[:, :half], 16) | (bits[:, half:] & HIGH16)


def _unpack_bf16_pairs(w):
    return jnp.concatenate([pltpu.bitcast(lax.shift_left(w, 16), F32),
                            pltpu.bitcast(w & HIGH16, F32)], axis=1)


def _row_dma_wait(hbm_ref, buf_ref, sem):
    pltpu.make_async_copy(hbm_ref.at[pl.ds(0, TM)], buf_ref, sem).wait()


def _lnqkv_body(*refs, has_ln, has_qkv, alpha, n_rope, dup_from, n_out):
    refs = list(refs)
    x_ref = refs.pop(0)
    if has_ln:
        pos_ref, posn_ref, ys_ref, g2_ref, lng_ref, lnb_ref = refs[:6]
        refs = refs[6:]
        ybuf, ysem = refs[-2:]
        refs = refs[:-2]
    if has_qkv:
        sc_ref, sh_ref, w_ref, cos_ref, sin_ref = refs[:5]
        refs = refs[5:]
    x = x_ref[...]
    if has_ln:
        xo_ref = refs.pop(0)
        i = pl.program_id(0)
        slot = i % 2

        def gather(p_ref, s):
            def body(r, carry):
                pltpu.make_async_copy(ys_ref.at[pl.ds(p_ref[0, 0, r], 1)], ybuf.at[s, pl.ds(r, 1)],
                                      ysem.at[s]).start()
                return carry
            lax.fori_loop(0, TM, body, 0, unroll=8)

        @pl.when(i == 0)
        def _():
            gather(pos_ref, 0)

        @pl.when(i + 1 < pl.num_programs(0))
        def _():
            gather(posn_ref, 1 - slot)

        _row_dma_wait(ys_ref, ybuf.at[slot], ysem.at[slot])
        z = alpha * x + g2_ref[0] * _unpack_bf16_pairs(ybuf[slot])
        x = _layer_norm(z, lng_ref[...], lnb_ref[...])
        xo_ref[...] = x
    if not has_qkv:
        return
    qkv_ref = refs.pop(0)
    h = (x * (1.0 + sc_ref[0]) + sh_ref[0]).astype(BF16)
    r = jnp.dot(h, w_ref[...], preferred_element_type=F32)
    cos = cos_ref[...]
    sin = sin_ref[...]
    lane = lax.broadcasted_iota(jnp.int32, (TM, LANES), 1)
    first16 = (lane % 32) < 16
    low_half = lane < HEAD_DIM
    q_cols = A_Q_HEADS * HEAD_DIM
    dst = 0
    for c in range(n_out // LANES):
        seg = r[:, c * LANES:(c + 1) * LANES]
        col = c * LANES
        if col < q_cols:
            seg = seg * ATTN_SCALE
        if col < n_rope:
            rot = jnp.where(first16, pltpu.roll(seg, LANES - 16, 1), pltpu.roll(seg, 16, 1))
            seg = seg * cos + rot * sin
        if col >= dup_from:
            swapped = pltpu.roll(seg, HEAD_DIM, 1)
            qkv_ref[:, dst:dst + LANES] = jnp.where(low_half, seg, swapped).astype(BF16)
            qkv_ref[:, dst + LANES:dst + 2 * LANES] = jnp.where(low_half, swapped, seg).astype(BF16)
            dst += 2 * LANES
        else:
            qkv_ref[:, dst:dst + LANES] = seg.astype(BF16)
            dst += LANES


def _lnqkv(x, y, mod_ln, ln_g, ln_b, mod_qkv, w, cos_t, sin_t, *, mode_in, n_batch, tiles_lat,
           alpha, n_rope=0, dup_from=0):
    d = x.shape[1]
    has_ln = y is not None
    has_qkv = w is not None
    n_tiles = (n_batch * (tiles_lat + 1)) if mode_in == "all" else n_batch * tiles_lat
    blk, modrow, pos = _tile_maps(mode_in, tiles_lat, n_batch)
    row = pl.BlockSpec((TM, d), lambda i: (blk(i), 0))
    vec = pl.BlockSpec((1, d), lambda i: (0, 0))

    def modspec(chunk):
        return pl.BlockSpec((1, 1, d), lambda i: (modrow(i), 0, chunk))

    args, in_specs, out_specs, out_shape, scratch = [x], [row], [], [], []
    n_out = 0
    if has_ln:
        ys, pos3 = y
        args += [pos3, pos3, ys, mod_ln, ln_g.reshape(1, d), ln_b.reshape(1, d)]
        in_specs += [pl.BlockSpec((1, 1, TM), lambda i: (i, 0, 0), memory_space=pltpu.SMEM),
                     pl.BlockSpec((1, 1, TM), lambda i: (jnp.minimum(i + 1, n_tiles - 1), 0, 0),
                                  memory_space=pltpu.SMEM),
                     pl.BlockSpec(memory_space=pl.ANY), modspec(5), vec, vec]
        out_specs.append(row)
        out_shape.append(jax.ShapeDtypeStruct(x.shape, F32))
        scratch = [pltpu.VMEM((2, TM, ys.shape[1]), jnp.int32), pltpu.SemaphoreType.DMA((2,))]
    if has_qkv:
        n_out = w.shape[1]
        n_store = n_out + (n_out - dup_from)
        args += [mod_qkv, mod_qkv, w, cos_t, sin_t]
        in_specs += [modspec(1), modspec(0),
                     pl.BlockSpec((d, n_out), lambda i: (0, 0)),
                     pl.BlockSpec((TM, LANES), lambda i: (pos(i), 0)),
                     pl.BlockSpec((TM, LANES), lambda i: (pos(i), 0))]
        out_specs.append(pl.BlockSpec((TM, n_store), lambda i: (blk(i), 0)))
        out_shape.append(jax.ShapeDtypeStruct((x.shape[0], n_store), BF16))
    body = functools.partial(_lnqkv_body, has_ln=has_ln, has_qkv=has_qkv, alpha=alpha,
                             n_rope=n_rope, dup_from=dup_from, n_out=n_out)
    return pl.pallas_call(
        body, grid=(n_tiles,), in_specs=in_specs, out_specs=out_specs, out_shape=out_shape,
        scratch_shapes=scratch, compiler_params=_cparams("arbitrary"),
        name=("ln_" if has_ln else "") + ("qkv" if has_qkv else "out"),
    )(*args)


def _softmax_parts(scores, biases, sink, cast=True):
    n_rows = scores[0].shape[0]
    prob_strips = [[] for _ in scores]
    inv_strips = []
    for r in range(0, n_rows, STRIP):
        strips = []
        for s, bias in zip(scores, biases):
            t = s[r:r + STRIP]
            if bias is not None:
                t = t + bias[r:r + STRIP, :]
            strips.append(t)
        m = None
        for t in strips:
            tm = jnp.max(t, axis=-1, keepdims=True)
            m = tm if m is None else jnp.maximum(m, tm)
        if sink is not None:
            m = jnp.maximum(m, sink)
        denom = None
        for idx, t in enumerate(strips):
            e = jnp.exp(t - m)
            es = jnp.sum(e, axis=-1, keepdims=True)
            denom = es if denom is None else denom + es
            prob_strips[idx].append(e.astype(BF16) if cast else e)
        if sink is not None:
            denom = denom + jnp.exp(sink - m)
        inv_strips.append(1.0 / denom)
    probs = [jnp.concatenate(ps, axis=0) for ps in prob_strips]
    return probs, jnp.concatenate(inv_strips, axis=0)


def _attn_a_body(sink_ref, q_ref, k_ref, v_ref, o_ref, bias_ref, *, seq, tiles_lat):
    qi = pl.program_id(1)
    lane = lax.broadcasted_iota(jnp.int32, (TM, LANES), 1)
    low_half = lane < HEAD_DIM
    span = TM + 2 * WINDOW

    def run(parts):
        for grp in range(A_Q_HEADS // 2):
            q2 = q_ref[:, grp * LANES:(grp + 1) * LANES]
            h = (2 * grp) // A_GROUP
            outs = []
            for half in range(2):
                keep = low_half if half == 0 else jnp.logical_not(low_half)
                qh = jnp.where(keep, q2, jnp.zeros_like(q2))
                scores = []
                for (r0, nr, _) in parts:
                    kk = k_ref[pl.ds(r0, nr), h * LANES:(h + 1) * LANES]
                    scores.append(lax.dot_general(qh, kk, NT_DIMS, preferred_element_type=F32))
                probs, inv = _softmax_parts(scores, [p[2] for p in parts], sink_ref[2 * grp + half])
                acc = None
                for p, (r0, nr, _) in zip(probs, parts):
                    vv = v_ref[pl.ds(r0, nr), h * LANES:(h + 1) * LANES]
                    o = jnp.dot(p, vv, preferred_element_type=F32)
                    acc = o if acc is None else acc + o
                outs.append(acc * inv)
            o_ref[:, grp * LANES:(grp + 1) * LANES] = jnp.where(low_half, outs[0], outs[1]).astype(BF16)

    @pl.when(qi < tiles_lat)
    def _():
        start = pl.multiple_of(jnp.clip(qi * TM - WINDOW, 0, seq - span), WINDOW)
        qpos = qi * TM + lax.broadcasted_iota(jnp.int32, (TM, span), 0)
        kpos = start + lax.broadcasted_iota(jnp.int32, (TM, span), 1)
        bias_ref[...] = jnp.where(jnp.abs(kpos - qpos) <= WINDOW, 0.0, NEG_INF)
        run([(start, span, bias_ref), (seq, k_ref.shape[0] - seq, None)])

    @pl.when(qi >= tiles_lat)
    def _():
        run([(seq, k_ref.shape[0] - seq, None)])


def _attn_a(qkv, sink, *, n_batch, seq, ctx_len, with_ctx):
    tiles_lat = seq // TM
    tpb = tiles_lat + ctx_len // TM
    rows_b = seq + ctx_len
    q_cols = A_Q_HEADS * HEAD_DIM
    kv_cols = 2 * A_KV_HEADS * HEAD_DIM
    nq = tpb if with_ctx else tiles_lat
    body = functools.partial(_attn_a_body, seq=seq, tiles_lat=tiles_lat)
    return pl.pallas_call(
        body,
        grid=(n_batch, nq),
        in_specs=[
            pl.BlockSpec(memory_space=pltpu.SMEM),
            pl.BlockSpec((TM, q_cols), lambda b, i: (b * tpb + i, 0)),
            pl.BlockSpec((rows_b, kv_cols), lambda b, i: (b, q_cols // kv_cols)),
            pl.BlockSpec((rows_b, kv_cols), lambda b, i: (b, q_cols // kv_cols + 1)),
        ],
        out_specs=pl.BlockSpec((TM, q_cols), lambda b, i: (b * nq + i, 0)),
        out_shape=jax.ShapeDtypeStruct((n_batch * nq * TM, q_cols), BF16),
        scratch_shapes=[pltpu.VMEM((TM, TM + 2 * WINDOW), F32)],
        compiler_params=_cparams("arbitrary", "arbitrary"),
        name="attn_window",
    )(sink, qkv, qkv, qkv)


def _attn_b_body(lam_ref, g_ref, q_ref, k_ref, v_ref, o_ref, *, seq, tiles_lat, lam_init, heads):
    qi = pl.program_id(2)
    lp = lam_ref[...]
    lam = (jnp.exp(jnp.sum(lp[0:1] * lp[1:2], axis=-1, keepdims=True))
           - jnp.exp(jnp.sum(lp[2:3] * lp[3:4], axis=-1, keepdims=True)) + lam_init)
    lane = lax.broadcasted_iota(jnp.int32, (TM, LANES), 1)
    low_half = lane < HEAD_DIM

    def run(r0, nr):
        for hh in range(heads):
            cols = slice(hh * LANES, (hh + 1) * LANES)
            q = q_ref[:, cols]
            kk = k_ref[pl.ds(r0, nr), cols]
            vv = v_ref[pl.ds(r0, nr), cols]
            outs = []
            for qm in (jnp.where(low_half, q, jnp.zeros_like(q)), jnp.where(low_half, jnp.zeros_like(q), q)):
                s = lax.dot_general(qm, kk, NT_DIMS, preferred_element_type=F32)
                outs.append(_softmax_parts([s], [None], None, cast=False))
            a = outs[0][0][0] * outs[0][1] - outs[1][0][0] * (lam * outs[1][1])
            o = jnp.dot(a.astype(BF16), vv, preferred_element_type=F32)
            ms = jnp.mean(o * o, axis=-1, keepdims=True)
            o = o * lax.rsqrt(ms + SUBLN_EPS) * g_ref[...] * (1.0 - lam_init)
            o_ref[:, cols] = o.astype(BF16)

    @pl.when(qi < tiles_lat)
    def _():
        run(0, k_ref.shape[0])

    @pl.when(qi >= tiles_lat)
    def _():
        run(seq, k_ref.shape[0] - seq)


def _attn_b(qkv, lam_params, subln_g, *, n_batch, seq, ctx_len, with_ctx, lam_init):
    tiles_lat = seq // TM
    tpb = tiles_lat + ctx_len // TM
    rows_b = seq + ctx_len
    n_heads = qkv.shape[1] // (3 * LANES)
    n_hgrp = n_heads // DIFF_HEADS_PER_STEP
    width = DIFF_HEADS_PER_STEP * LANES
    nq = tpb if with_ctx else tiles_lat
    body = functools.partial(_attn_b_body, seq=seq, tiles_lat=tiles_lat, lam_init=lam_init,
                             heads=DIFF_HEADS_PER_STEP)
    return pl.pallas_call(
        body,
        grid=(n_batch, n_hgrp, nq),
        in_specs=[
            pl.BlockSpec((4, HEAD_DIM), lambda b, h, i: (0, 0)),
            pl.BlockSpec((1, LANES), lambda b, h, i: (0, 0)),
            pl.BlockSpec((TM, width), lambda b, h, i: (b * tpb + i, h)),
            pl.BlockSpec((rows_b, width), lambda b, h, i: (b, n_hgrp + h)),
            pl.BlockSpec((rows_b, width), lambda b, h, i: (b, 2 * n_hgrp + h)),
        ],
        out_specs=pl.BlockSpec((TM, width), lambda b, h, i: (b * nq + i, h)),
        out_shape=jax.ShapeDtypeStruct((n_batch * nq * TM, n_heads * LANES), BF16),
        compiler_params=_cparams("arbitrary", "arbitrary", "arbitrary"),
        name="attn_diff",
    )(lam_params, subln_g.reshape(1, LANES), qkv, qkv, qkv)


def _route_rows(logits, bias):
    s = jax.nn.sigmoid(logits)
    biased = s + bias
    b = [biased[e:e + 1, :] for e in range(N_EXPERTS)]
    u = [s[e:e + 1, :] for e in range(N_EXPERTS)]
    gscore = []
    for g in range(N_GROUPS):
        best = None
        for (i, j) in PAIRS:
            t = b[GROUP_SIZE * g + i] + b[GROUP_SIZE * g + j]
            best = t if best is None else jnp.maximum(best, t)
        gscore.append(best)
    gsel = jnp.zeros(gscore[0].shape, jnp.int32)
    gbest = gscore[0]
    for g in range(1, N_GROUPS):
        better = gscore[g] > gbest
        gsel = jnp.where(better, g, gsel)
        gbest = jnp.where(better, gscore[g], gbest)

    def pick(rows, k):
        out = rows[GROUP_SIZE * (N_GROUPS - 1) + k]
        for g in range(N_GROUPS - 2, -1, -1):
            out = jnp.where(gsel == g, rows[GROUP_SIZE * g + k], out)
        return out

    v = [pick(b, k) for k in range(GROUP_SIZE)]
    w = [pick(u, k) for k in range(GROUP_SIZE)]
    sel = []
    for k in range(GROUP_SIZE):
        cnt = jnp.zeros(gsel.shape, jnp.int32)
        for j in range(GROUP_SIZE):
            if j == k:
                continue
            beats = (v[j] >= v[k]) if j < k else (v[j] > v[k])
            cnt = cnt + jnp.where(beats, 1, 0)
        sel.append(cnt < 2)
    pidx = jnp.zeros(gsel.shape, jnp.int32)
    u_lo = jnp.zeros(gbest.shape, F32)
    u_hi = jnp.zeros(gbest.shape, F32)
    for idx, (i, j) in enumerate(PAIRS):
        both = jnp.logical_and(sel[i], sel[j])
        pidx = jnp.where(both, idx, pidx)
        u_lo = jnp.where(both, w[i], u_lo)
        u_hi = jnp.where(both, w[j], u_hi)
    tot = u_lo + u_hi
    bucket = (gsel * len(PAIRS) + pidx).astype(F32)
    return bucket, u_lo / tot, u_hi / tot


def _oproj_body(a_ref, wo_ref, x_ref, g1_ref, lng_ref, lnb_ref, sc_ref, sh_ref, wr2_ref, wrh_ref, rb_ref,
                xo_ref, h2p_ref, r_ref, cnt_ref, *, alpha):
    i = pl.program_id(0)
    d = x_ref.shape[1]
    al = jnp.dot(a_ref[...], wo_ref[...], preferred_element_type=F32)
    z = alpha * x_ref[...] + g1_ref[0] * al
    xn = _layer_norm(z, lng_ref[...], lnb_ref[...])
    xo_ref[...] = xn
    h2 = xn * (1.0 + sc_ref[0]) + sh_ref[0]
    hi = h2.astype(BF16)
    lo = (h2 - hi.astype(F32)).astype(BF16)
    l2 = lax.dot_general(wr2_ref[...], hi, NT_DIMS, preferred_element_type=F32)
    l1 = lax.dot_general(wrh_ref[...], lo, NT_DIMS, preferred_element_type=F32)
    logits = l2[:N_EXPERTS] + l2[N_EXPERTS:] + l1
    bucket, w_lo, w_hi = _route_rows(logits, rb_ref[...])

    @pl.when(i == 0)
    def _():
        cnt_ref[...] = jnp.zeros_like(cnt_ref)

    onehot = jnp.where(lax.broadcasted_iota(jnp.int32, (BUCKET_ROWS, TM), 0).astype(F32) == bucket, 1.0, 0.0)
    earlier = (lax.broadcasted_iota(jnp.int32, (TM, TM), 0) < lax.broadcasted_iota(jnp.int32, (TM, TM), 1))
    before = jnp.dot(onehot.astype(BF16), jnp.where(earlier, 1.0, 0.0).astype(BF16),
                     preferred_element_type=F32)
    cnt = cnt_ref[...]
    rank = jnp.sum(onehot * (before + cnt[:, 0:1]), axis=0, keepdims=True)
    cnt_ref[...] = cnt + jnp.sum(onehot, axis=1, keepdims=True)

    r_ref[0:1, :] = bucket
    r_ref[1:2, :] = w_lo
    r_ref[2:3, :] = w_hi
    r_ref[3:4, :] = rank
    r_ref[4:ROUTE_ROWS, :] = jnp.zeros((ROUTE_ROWS - 4, TM), F32)
    rec = jnp.concatenate([r_ref[...], jnp.zeros((LANES - ROUTE_ROWS, TM), F32)], axis=0).T
    h2p_ref[:, :d // 2] = _pack_bf16_pairs(h2)
    h2p_ref[:, d // 2:] = pltpu.bitcast(rec, jnp.int32)


def _oproj(a, wo, x, mod, ln_g, ln_b, wr2, wrh, rbias, *, mode_in, n_batch, tiles_lat, alpha):
    d = x.shape[1]
    compact = mode_in == "lat"
    n_tiles = n_batch * tiles_lat if compact else n_batch * (tiles_lat + 1)
    blk, modrow, _ = _tile_maps(mode_in, tiles_lat, n_batch)
    row_in = pl.BlockSpec((TM, d), lambda i: (blk(i), 0))
    row_out = pl.BlockSpec((TM, d), lambda i: (i, 0))
    vec = pl.BlockSpec((1, d), lambda i: (0, 0))

    def modspec(chunk):
        return pl.BlockSpec((1, 1, d), lambda i: (modrow(i), 0, chunk))

    body = functools.partial(_oproj_body, alpha=alpha)
    return pl.pallas_call(
        body,
        grid=(n_tiles,),
        in_specs=[row_out, pl.BlockSpec((d, d), lambda i: (0, 0)), row_in, modspec(2), vec, vec,
                  modspec(4), modspec(3),
                  pl.BlockSpec((2 * N_EXPERTS, d), lambda i: (0, 0)),
                  pl.BlockSpec((N_EXPERTS, d), lambda i: (0, 0)),
                  pl.BlockSpec((N_EXPERTS, 1), lambda i: (0, 0))],
        out_specs=[row_out, pl.BlockSpec((TM, d // 2 + LANES), lambda i: (i, 0)),
                   pl.BlockSpec((ROUTE_ROWS, TM), lambda i: (0, i)),
                   pl.BlockSpec((BUCKET_ROWS, LANES), lambda i: (0, 0))],
        out_shape=[jax.ShapeDtypeStruct((n_tiles * TM, d), F32),
                   jax.ShapeDtypeStruct((n_tiles * TM, d // 2 + LANES), jnp.int32),
                   jax.ShapeDtypeStruct((ROUTE_ROWS, n_tiles * TM), F32),
                   jax.ShapeDtypeStruct((BUCKET_ROWS, LANES), F32)],
        compiler_params=_cparams("arbitrary"),
        name="oproj_ln_route",
    )(a, wo, x, mod, ln_g.reshape(1, d), ln_b.reshape(1, d), mod, mod, wr2, wrh, rbias)


def _scatter_body(pos_ref, h_ref, init_ref, xs_ref, stage, sem):
    del init_ref
    i = pl.program_id(0)
    slot = i % 2

    @pl.when(i >= 2)
    def _():
        _row_dma_wait(xs_ref, stage.at[slot], sem.at[slot])

    stage[slot] = h_ref[...]

    def body(r, carry):
        pltpu.make_async_copy(stage.at[slot, pl.ds(r, 1)], xs_ref.at[pl.ds(pos_ref[0, 0, r], 1)],
                              sem.at[slot]).start()
        return carry
    lax.fori_loop(0, TM, body, 0, unroll=8)

    @pl.when(i == pl.num_programs(0) - 1)
    def _():
        @pl.when(i >= 1)
        def _():
            _row_dma_wait(xs_ref, stage.at[1 - slot], sem.at[1 - slot])
        _row_dma_wait(xs_ref, stage.at[slot], sem.at[slot])


def _scatter_rows(h2p, pos3, n_rows_sorted):
    n_tiles = pos3.shape[0]
    width = h2p.shape[1]
    return pl.pallas_call(
        _scatter_body,
        grid=(n_tiles,),
        in_specs=[pl.BlockSpec((1, 1, TM), lambda i: (i, 0, 0), memory_space=pltpu.SMEM),
                  pl.BlockSpec((TM, width), lambda i: (i, 0)), pl.BlockSpec(memory_space=pl.ANY)],
        out_specs=pl.BlockSpec(memory_space=pl.ANY),
        out_shape=jax.ShapeDtypeStruct((n_rows_sorted, width), jnp.int32),
        scratch_shapes=[pltpu.VMEM((2, TM, width), jnp.int32), pltpu.SemaphoreType.DMA((2,))],
        input_output_aliases={2: 0},
        compiler_params=_cparams("arbitrary"),
        name="moe_scatter",
    )(pos3, h2p, jnp.zeros((n_rows_sorted, width), jnp.int32))


def _moe_body(ea_ref, eb_ref, act_ref, new_ref, x_ref, wg_a, wu_a, wd_a, wg_b, wu_b, wd_b, y_ref,
              wgu_s, wd_s, *, d_exp):
    i = pl.program_id(0)
    half = wd_a.shape[3] // 2

    @pl.when(new_ref[i] > 0)
    def _():
        for e, (wg, wu, wd) in enumerate(((wg_a, wu_a, wd_a), (wg_b, wu_b, wd_b))):
            wgu_s[e, :, :d_exp] = wg[0, 0].astype(BF16)
            wgu_s[e, :, d_exp:] = wu[0, 0].astype(BF16)
            wd_s[e] = wd[0, 0].astype(BF16)

    @pl.when(act_ref[i] > 0)
    def _():
        xw = x_ref[...]
        x = _unpack_bf16_pairs(xw[:, :half]).astype(BF16)
        rec = pltpu.bitcast(xw[:, half:], F32)

        def expert(e):
            gu = jnp.dot(x, wgu_s[e], preferred_element_type=F32)
            gate = gu[:, :d_exp]
            h = gate * jax.nn.sigmoid(gate) * gu[:, d_exp:] * rec[:, 1 + e:2 + e]
            return jnp.dot(h.astype(BF16), wd_s[e], preferred_element_type=F32)

        y_ref[...] = _pack_bf16_pairs(expert(0) + expert(1))

    @pl.when(act_ref[i] == 0)
    def _():
        y_ref[...] = jnp.zeros_like(y_ref)


def _moe(xs, w_gate, w_up, w_down, layer, ea, eb, act, new):
    n_rows, width = xs.shape
    d_exp, d = w_down.shape[2:]
    n_tiles = n_rows // TM
    body = functools.partial(_moe_body, d_exp=d_exp)
    in_w = lambda sel: [
        pl.BlockSpec((1, 1, d, d_exp), lambda i, ea, eb, act, new: (layer, sel(ea, eb)[i], 0, 0)),
        pl.BlockSpec((1, 1, d, d_exp), lambda i, ea, eb, act, new: (layer, sel(ea, eb)[i], 0, 0)),
        pl.BlockSpec((1, 1, d_exp, d), lambda i, ea, eb, act, new: (layer, sel(ea, eb)[i], 0, 0))]
    grid_spec = pltpu.PrefetchScalarGridSpec(
        num_scalar_prefetch=4,
        grid=(n_tiles,),
        in_specs=[pl.BlockSpec((TM, width), lambda i, ea, eb, act, new: (i, 0))]
        + in_w(lambda ea, eb: ea) + in_w(lambda ea, eb: eb),
        out_specs=pl.BlockSpec((TM, d // 2), lambda i, ea, eb, act, new: (i, 0)),
        scratch_shapes=[pltpu.VMEM((2, d, 2 * d_exp), BF16), pltpu.VMEM((2, d_exp, d), BF16)],
    )
    return pl.pallas_call(
        body, grid_spec=grid_spec,
        out_shape=jax.ShapeDtypeStruct((n_rows, d // 2), jnp.int32),
        compiler_params=_cparams("arbitrary"),
        name="moe_pairs",
    )(ea, eb, act, new, xs, w_gate, w_up, w_down, w_gate, w_up, w_down)


def _dispatch(route, counts, n_tiles):
    n = route.shape[1]
    bucket = route[0].astype(jnp.int32)
    rank = route[3].astype(jnp.int32)
    counts = counts[:N_BUCKETS, 0].astype(jnp.int32)
    padded = ((counts + TM - 1) // TM) * TM
    ends = jnp.cumsum(padded)
    off = ends - padded

    def lookup(idx, table):
        return jnp.sum(jnp.where(idx[:, None] == jnp.arange(table.shape[0])[None, :], table[None, :], 0), axis=1)

    pos = lookup(bucket, off) + rank
    tile_start = jnp.arange(n_tiles, dtype=jnp.int32) * TM
    active = tile_start < ends[-1]
    last_start = jnp.maximum(ends[-1] - TM, 0)
    tb = jnp.sum((jnp.where(active, tile_start, last_start)[:, None] >= ends[None, :]).astype(jnp.int32), axis=1)
    tb = jnp.minimum(tb, N_BUCKETS - 1)
    ea = (tb // len(PAIRS)) * GROUP_SIZE + lookup(tb % len(PAIRS), jnp.array([p[0] for p in PAIRS], jnp.int32))
    eb = (tb // len(PAIRS)) * GROUP_SIZE + lookup(tb % len(PAIRS), jnp.array([p[1] for p in PAIRS], jnp.int32))
    new = jnp.concatenate([jnp.ones((1,), jnp.int32), (tb[1:] != tb[:-1]).astype(jnp.int32)])
    return pos.reshape(n // TM, 1, TM), ea, eb, active.astype(jnp.int32), new


def _rope_tables(seq, ctx_len):
    t = jnp.arange(seq)
    quarter = HEAD_DIM // 4
    inv_freq = ROPE_THETA ** (-jnp.arange(quarter, dtype=F32) / quarter)
    ang_r = (t // GRID_W).astype(F32)[:, None] * inv_freq
    ang_c = (t % GRID_W).astype(F32)[:, None] * inv_freq
    cos_h = jnp.concatenate([jnp.cos(ang_r)] * 2 + [jnp.cos(ang_c)] * 2, axis=-1)
    sin_h = jnp.concatenate([-jnp.sin(ang_r), jnp.sin(ang_r), -jnp.sin(ang_c), jnp.sin(ang_c)], axis=-1)
    cos_t = jnp.concatenate([cos_h, cos_h], axis=-1)
    sin_t = jnp.concatenate([sin_h, sin_h], axis=-1)
    cos_t = jnp.concatenate([cos_t, jnp.ones((ctx_len, LANES), F32)], axis=0)
    sin_t = jnp.concatenate([sin_t, jnp.zeros((ctx_len, LANES), F32)], axis=0)
    return cos_t, sin_t


def kernel(x, c, ctx, c_ctx, w_ada, b_ada, wqkv_a, wo_a, sink_a, wqkv_b, wo_b, lambda_b, subln_b, ln_attn_g, ln_attn_b, ln_ffn_g, ln_ffn_b, w_router, router_bias, w_gate, w_up, w_down):
    n_batch, seq, d = x.shape
    ctx_len = ctx.shape[1]
    depth = w_ada.shape[0]
    assert seq % TM == 0 and ctx_len == TM and seq >= TM + 2 * WINDOW
    assert n_batch + 1 <= ADA_ROWS and d == A_Q_HEADS * HEAD_DIM
    tiles_lat = seq // TM
    alpha = (2 * depth) ** 0.25
    q_cols = A_Q_HEADS * HEAD_DIM
    kv_cols = A_KV_HEADS * HEAD_DIM

    cc = jnp.concatenate([c, c_ctx[None, :], jnp.zeros((ADA_ROWS - n_batch - 1, d), F32)], axis=0)
    mods = _ada_modulation(cc, w_ada, b_ada).reshape(depth, ADA_ROWS, 1, 6 * d)
    cos_t, sin_t = _rope_tables(seq, ctx_len)

    xs = jnp.concatenate([x, ctx], axis=1).reshape(n_batch * (seq + ctx_len), d)
    wr_hi = w_router.astype(BF16)
    wr_lo = (w_router - wr_hi.astype(F32)).astype(BF16)
    wr2 = jnp.concatenate([wr_hi.T, wr_lo.T], axis=0)
    wrh = wr_hi.T
    rbias = router_bias.reshape(N_EXPERTS, 1)

    y = None
    mode = "all"
    for i in range(depth):
        last = i == depth - 1
        j = i // 2
        if i % 2 == 0:
            w = wqkv_a[j].astype(BF16)
            n_rope, dup_from = q_cols + kv_cols, q_cols
        else:
            w = wqkv_b[j].astype(BF16)
            n_rope, dup_from = 2 * q_cols, w.shape[1]
        outs = _lnqkv(xs, y, mods[i - 1] if i else None, ln_ffn_g[i - 1] if i else None,
                      ln_ffn_b[i - 1] if i else None, mods[i], w, cos_t, sin_t,
                      mode_in="all", n_batch=n_batch, tiles_lat=tiles_lat,
                      alpha=alpha, n_rope=n_rope, dup_from=dup_from)
        if i:
            xs, qkv = outs
        else:
            (qkv,) = outs
        if i % 2 == 0:
            att = _attn_a(qkv, sink_a[j], n_batch=n_batch, seq=seq, ctx_len=ctx_len, with_ctx=not last)
            wo = wo_a[j].astype(BF16)
        else:
            lam_init = 0.8 - 0.6 * math.exp(-0.3 * i)
            att = _attn_b(qkv, lambda_b[j], subln_b[j], n_batch=n_batch, seq=seq, ctx_len=ctx_len,
                          with_ctx=not last, lam_init=lam_init)
            wo = wo_b[j].astype(BF16)
        mode = "lat" if last else "all"
        xs, h2p, route, counts = _oproj(att, wo, xs, mods[i], ln_attn_g[i], ln_attn_b[i], wr2, wrh, rbias,
                                        mode_in=mode, n_batch=n_batch, tiles_lat=tiles_lat, alpha=alpha)
        n_tiles = h2p.shape[0] // TM + N_BUCKETS
        pos3, ea, eb, act, new = _dispatch(route, counts, n_tiles)
        x_sorted = _scatter_rows(h2p, pos3, n_tiles * TM)
        y = (_moe(x_sorted, w_gate, w_up, w_down, i, ea, eb, act, new), pos3)

    (out,) = _lnqkv(xs, y, mods[depth - 1], ln_ffn_g[depth - 1], ln_ffn_b[depth - 1], None, None, None, None,
                    mode_in="compact", n_batch=n_batch, tiles_lat=tiles_lat, alpha=alpha)
    return out.reshape(n_batch, seq, d)
```

```python
import functools
import math

import jax
import jax.numpy as jnp
from jax import lax
from jax.experimental import pallas as pl
from jax.experimental.pallas import tpu as pltpu

F32 = jnp.float32
BF16 = jnp.bfloat16

HEAD_DIM = 64
A_Q_HEADS = 16
A_KV_HEADS = 4
A_GROUP = A_Q_HEADS // A_KV_HEADS
GRID_W = 64
WINDOW = 128
N_EXPERTS = 16
N_GROUPS = 4
GROUP_SIZE = N_EXPERTS // N_GROUPS
ROPE_THETA = 10000.0
LN_EPS = 1e-6
SUBLN_EPS = 1e-5
NEG_INF = -1e30
ATTN_SCALE = HEAD_DIM ** -0.5
LOG2E = math.log2(math.e)
VT_ROWS = 144

LANES = 128
TM = 256
DIFF_HEADS_PER_STEP = 2
ADA_ROWS = 24
VMEM_LIMIT = 56 * 1024 * 1024

PAIRS = ((0, 1), (0, 2), (0, 3), (1, 2), (1, 3), (2, 3))
N_BUCKETS = N_GROUPS * len(PAIRS)

BUCKET_ROWS = 32
ROUTE_ROWS = 8
HIGH16 = -65536

NT_DIMS = (((1,), (1,)), ((), ()))


def _cparams(*sem):
    return pltpu.CompilerParams(dimension_semantics=sem, vmem_limit_bytes=VMEM_LIMIT)


def _ada_body(c_ref, w_ref, b_ref, o_ref):
    c = c_ref[...]
    sc = c * jax.nn.sigmoid(c)
    o_ref[0] = jnp.dot(sc, w_ref[0], precision=lax.Precision.HIGHEST,
                       preferred_element_type=F32) + b_ref[0]


def _ada_modulation(cc, w_ada, b_ada):
    depth, d, d6 = w_ada.shape
    tn = 1536
    return pl.pallas_call(
        _ada_body,
        grid=(depth, d6 // tn),
        in_specs=[
            pl.BlockSpec((ADA_ROWS, d), lambda l, n: (0, 0)),
            pl.BlockSpec((1, d, tn), lambda l, n: (l, 0, n)),
            pl.BlockSpec((1, 1, tn), lambda l, n: (l, 0, n)),
        ],
        out_specs=pl.BlockSpec((1, ADA_ROWS, tn), lambda l, n: (l, 0, n)),
        out_shape=jax.ShapeDtypeStruct((depth, ADA_ROWS, d6), F32),
        compiler_params=_cparams("arbitrary", "arbitrary"),
        name="ada_mod",
    )(cc, w_ada, b_ada.reshape(depth, 1, d6))


def _tile_maps(mode, tiles_lat, n_batch):
    tpb = tiles_lat + 1
    if mode == "all":
        blk = lambda i: i
        modrow = lambda i: jnp.where(i % tpb == tiles_lat, n_batch, i // tpb)
        pos = lambda i: i % tpb
    elif mode == "lat":
        blk = lambda i: (i // tiles_lat) * tpb + i % tiles_lat
        modrow = lambda i: i // tiles_lat
        pos = lambda i: i % tiles_lat
    else:
        blk = lambda i: i
        modrow = lambda i: i // tiles_lat
        pos = lambda i: i % tiles_lat
    return blk, modrow, pos


def _layer_norm(z, g, b):
    mu = jnp.mean(z, axis=-1, keepdims=True)
    zc = z - mu
    var = jnp.mean(zc * zc, axis=-1, keepdims=True)
    return zc * lax.rsqrt(var + LN_EPS) * g + b


def _pack_bf16_pairs(v):
    half = v.shape[1] // 2
    bits = pltpu.bitcast(v.astype(BF16).astype(F32), jnp.int32)
    return lax.shift_right_logical(bits[:, :half], 16) | (bits[:, half:] & HIGH16)


def _unpack_bf16_pairs(w):
    return jnp.concatenate([pltpu.bitcast(lax.shift_left(w, 16), F32),
                            pltpu.bitcast(w & HIGH16, F32)], axis=1)


def _row_dma_wait(hbm_ref, buf_ref, sem):
    pltpu.make_async_copy(hbm_ref.at[pl.ds(0, TM)], buf_ref, sem).wait()


def _lnqkv_body(*refs, has_ln, has_qkv, alpha, n_rope, dup_from, n_out):
    refs = list(refs)
    x_ref = refs.pop(0)
    if has_ln:
        pos_ref, posn_ref, ys_ref, g2_ref, lng_ref, lnb_ref = refs[:6]
        refs = refs[6:]
        ybuf, ysem = refs[-2:]
        refs = refs[:-2]
    if has_qkv:
        sc_ref, sh_ref, w_ref, cos_ref, sin_ref = refs[:5]
        refs = refs[5:]
    x = x_ref[...]
    if has_ln:
        xo_ref = refs.pop(0)
        i = pl.program_id(0)
        slot = i % 2

        def gather(p_ref, s):
            def body(r, carry):
                pltpu.make_async_copy(ys_ref.at[pl.ds(p_ref[0, 0, r], 1)], ybuf.at[s, pl.ds(r, 1)],
                                      ysem.at[s]).start()
                return carry
            lax.fori_loop(0, TM, body, 0, unroll=8)

        @pl.when(i == 0)
        def _():
            gather(pos_ref, 0)

        @pl.when(i + 1 < pl.num_programs(0))
        def _():
            gather(posn_ref, 1 - slot)

        _row_dma_wait(ys_ref, ybuf.at[slot], ysem.at[slot])
        z = alpha * x + g2_ref[0] * _unpack_bf16_pairs(ybuf[slot])
        x = _layer_norm(z, lng_ref[...], lnb_ref[...])
        xo_ref[...] = x
    if not has_qkv:
        return
    qkv_ref = refs.pop(0)
    h = (x * (1.0 + sc_ref[0]) + sh_ref[0]).astype(BF16)
    r = jnp.dot(h, w_ref[...], preferred_element_type=F32)
    cos = cos_ref[...]
    sin = sin_ref[...]
    lane = lax.broadcasted_iota(jnp.int32, (TM, LANES), 1)
    first16 = (lane % 32) < 16
    low_half = lane < HEAD_DIM
    q_cols = A_Q_HEADS * HEAD_DIM
    dst = 0
    for c in range(n_out // LANES):
        seg = r[:, c * LANES:(c + 1) * LANES]
        col = c * LANES
        if col < q_cols:
            seg = seg * (ATTN_SCALE * LOG2E)
        if col < n_rope:
            rot = jnp.where(first16, pltpu.roll(seg, LANES - 16, 1), pltpu.roll(seg, 16, 1))
            seg = seg * cos + rot * sin
        if col >= dup_from:
            swapped = pltpu.roll(seg, HEAD_DIM, 1)
            qkv_ref[:, dst:dst + LANES] = jnp.where(low_half, seg, swapped).astype(BF16)
            qkv_ref[:, dst + LANES:dst + 2 * LANES] = jnp.where(low_half, swapped, seg).astype(BF16)
            dst += 2 * LANES
        else:
            qkv_ref[:, dst:dst + LANES] = seg.astype(BF16)
            dst += LANES


def _lnqkv(x, y, mod_ln, ln_g, ln_b, mod_qkv, w, cos_t, sin_t, *, mode_in, n_batch, tiles_lat,
           alpha, n_rope=0, dup_from=0):
    d = x.shape[1]
    has_ln = y is not None
    has_qkv = w is not None
    n_tiles = (n_batch * (tiles_lat + 1)) if mode_in == "all" else n_batch * tiles_lat
    blk, modrow, pos = _tile_maps(mode_in, tiles_lat, n_batch)
    row = pl.BlockSpec((TM, d), lambda i: (blk(i), 0))
    vec = pl.BlockSpec((1, d), lambda i: (0, 0))

    def modspec(chunk):
        return pl.BlockSpec((1, 1, d), lambda i: (modrow(i), 0, chunk))

    args, in_specs, out_specs, out_shape, scratch = [x], [row], [], [], []
    n_out = 0
    if has_ln:
        ys, pos3 = y
        args += [pos3, pos3, ys, mod_ln, ln_g.reshape(1, d), ln_b.reshape(1, d)]
        in_specs += [pl.BlockSpec((1, 1, TM), lambda i: (i, 0, 0), memory_space=pltpu.SMEM),
                     pl.BlockSpec((1, 1, TM), lambda i: (jnp.minimum(i + 1, n_tiles - 1), 0, 0),
                                  memory_space=pltpu.SMEM),
                     pl.BlockSpec(memory_space=pl.ANY), modspec(5), vec, vec]
        out_specs.append(row)
        out_shape.append(jax.ShapeDtypeStruct(x.shape, F32))
        scratch = [pltpu.VMEM((2, TM, ys.shape[1]), jnp.int32), pltpu.SemaphoreType.DMA((2,))]
    if has_qkv:
        n_out = w.shape[1]
        n_store = n_out + (n_out - dup_from)
        args += [mod_qkv, mod_qkv, w, cos_t, sin_t]
        in_specs += [modspec(1), modspec(0),
                     pl.BlockSpec((d, n_out), lambda i: (0, 0)),
                     pl.BlockSpec((TM, LANES), lambda i: (pos(i), 0)),
                     pl.BlockSpec((TM, LANES), lambda i: (pos(i), 0))]
        out_specs.append(pl.BlockSpec((TM, n_store), lambda i: (blk(i), 0)))
        out_shape.append(jax.ShapeDtypeStruct((x.shape[0], n_store), BF16))
    body = functools.partial(_lnqkv_body, has_ln=has_ln, has_qkv=has_qkv, alpha=alpha,
                             n_rope=n_rope, dup_from=dup_from, n_out=n_out)
    return pl.pallas_call(
        body, grid=(n_tiles,), in_specs=in_specs, out_specs=out_specs, out_shape=out_shape,
        scratch_shapes=scratch, compiler_params=_cparams("arbitrary"),
        name=("ln_" if has_ln else "") + ("qkv" if has_qkv else "out"),
    )(*args)


def _softmax_parts(scores, biases, sink):
    scores = [s if bias is None else s + bias[...] for s, bias in zip(scores, biases)]
    m = None
    for s in scores:
        sm = jnp.max(s, axis=-1, keepdims=True)
        m = sm if m is None else jnp.maximum(m, sm)
    if sink is not None:
        m = jnp.maximum(m, sink)
    probs = []
    denom = None
    for s in scores:
        e = jnp.exp2(s - m)
        es = jnp.sum(e, axis=-1, keepdims=True)
        denom = es if denom is None else denom + es
        probs.append(e.astype(BF16))
    if sink is not None:
        denom = denom + jnp.exp2(sink - m)
    return probs, 1.0 / denom


def _attn_a_body(sink_ref, q_ref, k_ref, v_ref, o_ref, bias_ref, *, seq, tiles_lat):
    qi = pl.program_id(1)
    lane = lax.broadcasted_iota(jnp.int32, (TM, LANES), 1)
    low_half = lane < HEAD_DIM
    span = TM + 2 * WINDOW
    rows = lax.broadcasted_iota(jnp.int32, (A_GROUP * TM, 1), 0)

    def run(parts):
        all_scores = []
        for h in range(A_KV_HEADS):
            qs = []
            for g in range(A_GROUP):
                q2 = q_ref[:, (h * A_GROUP + g) // 2 * LANES:((h * A_GROUP + g) // 2 + 1) * LANES]
                keep = low_half if g % 2 == 0 else jnp.logical_not(low_half)
                qs.append(jnp.where(keep, q2, jnp.zeros_like(q2)))
            qh = jnp.concatenate(qs, axis=0)
            scores = []
            for (r0, nr, _) in parts:
                kk = k_ref[pl.ds(r0, nr), h * LANES:(h + 1) * LANES]
                scores.append(lax.dot_general(qh, kk, NT_DIMS, preferred_element_type=F32))
            all_scores.append(scores)
        for h in range(A_KV_HEADS):
            sink = jnp.zeros((A_GROUP * TM, 1), F32)
            for g in range(A_GROUP):
                sink = jnp.where(rows // TM == g, sink_ref[h * A_GROUP + g] * LOG2E, sink)
            probs, inv = _softmax_parts(all_scores[h], [p[2] for p in parts], sink)
            acc = None
            for p, (r0, nr, _) in zip(probs, parts):
                vv = v_ref[pl.ds(r0, nr), h * LANES:(h + 1) * LANES]
                o = jnp.dot(p, vv, preferred_element_type=F32)
                acc = o if acc is None else acc + o
            acc = acc * inv
            for gp in range(A_GROUP // 2):
                grp = h * (A_GROUP // 2) + gp
                o_ref[:, grp * LANES:(grp + 1) * LANES] = jnp.where(
                    low_half, acc[2 * gp * TM:(2 * gp + 1) * TM], acc[(2 * gp + 1) * TM:(2 * gp + 2) * TM]
                ).astype(BF16)

    @pl.when(qi < tiles_lat)
    def _():
        start = pl.multiple_of(jnp.clip(qi * TM - WINDOW, 0, seq - span), WINDOW)
        qpos = qi * TM + lax.broadcasted_iota(jnp.int32, (TM, span), 0)
        kpos = start + lax.broadcasted_iota(jnp.int32, (TM, span), 1)
        bias = jnp.where(jnp.abs(kpos - qpos) <= WINDOW, 0.0, NEG_INF)
        for g in range(A_GROUP):
            bias_ref[g * TM:(g + 1) * TM, :] = bias
        run([(start, span, bias_ref), (seq, k_ref.shape[0] - seq, None)])

    @pl.when(qi >= tiles_lat)
    def _():
        run([(seq, k_ref.shape[0] - seq, None)])


def _attn_a(qkv, sink, *, n_batch, seq, ctx_len, with_ctx):
    tiles_lat = seq // TM
    tpb = tiles_lat + ctx_len // TM
    rows_b = seq + ctx_len
    q_cols = A_Q_HEADS * HEAD_DIM
    kv_cols = 2 * A_KV_HEADS * HEAD_DIM
    nq = tpb if with_ctx else tiles_lat
    body = functools.partial(_attn_a_body, seq=seq, tiles_lat=tiles_lat)
    return pl.pallas_call(
        body,
        grid=(n_batch, nq),
        in_specs=[
            pl.BlockSpec(memory_space=pltpu.SMEM),
            pl.BlockSpec((TM, q_cols), lambda b, i: (b * tpb + i, 0)),
            pl.BlockSpec((rows_b, kv_cols), lambda b, i: (b, q_cols // kv_cols)),
            pl.BlockSpec((rows_b, kv_cols), lambda b, i: (b, q_cols // kv_cols + 1)),
        ],
        out_specs=pl.BlockSpec((TM, q_cols), lambda b, i: (b * nq + i, 0)),
        out_shape=jax.ShapeDtypeStruct((n_batch * nq * TM, q_cols), BF16),
        scratch_shapes=[pltpu.VMEM((A_GROUP * TM, TM + 2 * WINDOW), F32)],
        compiler_params=_cparams("arbitrary", "arbitrary"),
        name="attn_window",
    )(sink, qkv, qkv, qkv)


def _attn_b_body(lam_ref, g_ref, q_ref, k_ref, v_ref, o_ref, vt_ref, *, seq, tiles_lat, lam_init, heads):
    qi = pl.program_id(2)
    lp = lam_ref[...]
    lam = (jnp.exp(jnp.sum(lp[0:1] * lp[1:2], axis=-1, keepdims=True))
           - jnp.exp(jnp.sum(lp[2:3] * lp[3:4], axis=-1, keepdims=True)) + lam_init)
    lane = lax.broadcasted_iota(jnp.int32, (TM, LANES), 1)
    low_half = lane < HEAD_DIM
    n_keys = k_ref.shape[0]

    @pl.when(qi == 0)
    def _():
        for hh in range(heads):
            vt_ref[hh, :LANES, :] = v_ref[:, hh * LANES:(hh + 1) * LANES].astype(F32).T.astype(BF16)
            row = lax.broadcasted_iota(jnp.int32, (VT_ROWS - LANES, n_keys), 0)
            vt_ref[hh, LANES:, :] = jnp.where(row == 0, 1.0, 0.0).astype(BF16)

    def run(r0, nr):
        scores = []
        for hh in range(heads):
            cols = slice(hh * LANES, (hh + 1) * LANES)
            q = q_ref[:, cols]
            kk = k_ref[pl.ds(r0, nr), cols]
            for qm in (jnp.where(low_half, q, jnp.zeros_like(q)), jnp.where(low_half, jnp.zeros_like(q), q)):
                scores.append(lax.dot_general(kk, qm, NT_DIMS, preferred_element_type=F32))
        for hh in range(heads):
            vt = vt_ref[hh, :, pl.ds(r0, nr)]
            outs = []
            for s in scores[2 * hh:2 * hh + 2]:
                e = jnp.exp2(s - jnp.max(s, axis=0, keepdims=True)).astype(BF16)
                ov = jnp.dot(vt, e, preferred_element_type=F32)
                outs.append((ov[:LANES], 1.0 / ov[LANES:LANES + 1]))
            o = outs[0][0] * outs[0][1] - outs[1][0] * (lam * outs[1][1])
            ms = jnp.mean(o * o, axis=0, keepdims=True)
            o = o * lax.rsqrt(ms + SUBLN_EPS) * g_ref[...] * (1.0 - lam_init)
            o_ref[:, hh * LANES:(hh + 1) * LANES] = o.T.astype(BF16)

    @pl.when(qi < tiles_lat)
    def _():
        run(0, k_ref.shape[0])

    @pl.when(qi >= tiles_lat)
    def _():
        run(seq, k_ref.shape[0] - seq)


def _attn_b(qkv, lam_params, subln_g, *, n_batch, seq, ctx_len, with_ctx, lam_init):
    tiles_lat = seq // TM
    tpb = tiles_lat + ctx_len // TM
    rows_b = seq + ctx_len
    n_heads = qkv.shape[1] // (3 * LANES)
    n_hgrp = n_heads // DIFF_HEADS_PER_STEP
    width = DIFF_HEADS_PER_STEP * LANES
    nq = tpb if with_ctx else tiles_lat
    body = functools.partial(_attn_b_body, seq=seq, tiles_lat=tiles_lat, lam_init=lam_init,
                             heads=DIFF_HEADS_PER_STEP)
    return pl.pallas_call(
        body,
        grid=(n_batch, n_hgrp, nq),
        in_specs=[
            pl.BlockSpec((4, HEAD_DIM), lambda b, h, i: (0, 0)),
            pl.BlockSpec((LANES, 1), lambda b, h, i: (0, 0)),
            pl.BlockSpec((TM, width), lambda b, h, i: (b * tpb + i, h)),
            pl.BlockSpec((rows_b, width), lambda b, h, i: (b, n_hgrp + h)),
            pl.BlockSpec((rows_b, width), lambda b, h, i: (b, 2 * n_hgrp + h)),
        ],
        out_specs=pl.BlockSpec((TM, width), lambda b, h, i: (b * nq + i, h)),
        out_shape=jax.ShapeDtypeStruct((n_batch * nq * TM, n_heads * LANES), BF16),
        compiler_params=_cparams("arbitrary", "arbitrary", "arbitrary"),
        scratch_shapes=[pltpu.VMEM((DIFF_HEADS_PER_STEP, VT_ROWS, rows_b), BF16)],
        name="attn_diff",
    )(lam_params, subln_g.reshape(LANES, 1), qkv, qkv, qkv)


def _route_rows(logits, bias):
    s = jax.nn.sigmoid(logits)
    biased = s + bias
    b = [biased[e:e + 1, :] for e in range(N_EXPERTS)]
    u = [s[e:e + 1, :] for e in range(N_EXPERTS)]
    gscore = []
    for g in range(N_GROUPS):
        best = None
        for (i, j) in PAIRS:
            t = b[GROUP_SIZE * g + i] + b[GROUP_SIZE * g + j]
            best = t if best is None else jnp.maximum(best, t)
        gscore.append(best)
    gsel = jnp.zeros(gscore[0].shape, jnp.int32)
    gbest = gscore[0]
    for g in range(1, N_GROUPS):
        better = gscore[g] > gbest
        gsel = jnp.where(better, g, gsel)
        gbest = jnp.where(better, gscore[g], gbest)

    def pick(rows, k):
        out = rows[GROUP_SIZE * (N_GROUPS - 1) + k]
        for g in range(N_GROUPS - 2, -1, -1):
            out = jnp.where(gsel == g, rows[GROUP_SIZE * g + k], out)
        return out

    v = [pick(b, k) for k in range(GROUP_SIZE)]
    w = [pick(u, k) for k in range(GROUP_SIZE)]
    sel = []
    for k in range(GROUP_SIZE):
        cnt = jnp.zeros(gsel.shape, jnp.int32)
        for j in range(GROUP_SIZE):
            if j == k:
                continue
            beats = (v[j] >= v[k]) if j < k else (v[j] > v[k])
            cnt = cnt + jnp.where(beats, 1, 0)
        sel.append(cnt < 2)
    pidx = jnp.zeros(gsel.shape, jnp.int32)
    u_lo = jnp.zeros(gbest.shape, F32)
    u_hi = jnp.zeros(gbest.shape, F32)
    for idx, (i, j) in enumerate(PAIRS):
        both = jnp.logical_and(sel[i], sel[j])
        pidx = jnp.where(both, idx, pidx)
        u_lo = jnp.where(both, w[i], u_lo)
        u_hi = jnp.where(both, w[j], u_hi)
    tot = u_lo + u_hi
    bucket = (gsel * len(PAIRS) + pidx).astype(F32)
    return bucket, u_lo / tot, u_hi / tot


def _oproj_body(a_ref, wo_ref, x_ref, g1_ref, lng_ref, lnb_ref, sc_ref, sh_ref, wr2_ref, wrh_ref, rb_ref,
                xo_ref, h2p_ref, r_ref, cnt_ref, *, alpha):
    i = pl.program_id(0)
    d = x_ref.shape[1]
    al = jnp.dot(a_ref[...], wo_ref[...], preferred_element_type=F32)
    z = alpha * x_ref[...] + g1_ref[0] * al
    xn = _layer_norm(z, lng_ref[...], lnb_ref[...])
    xo_ref[...] = xn
    h2 = xn * (1.0 + sc_ref[0]) + sh_ref[0]
    hi = h2.astype(BF16)
    lo = (h2 - hi.astype(F32)).astype(BF16)
    l2 = lax.dot_general(wr2_ref[...], hi, NT_DIMS, preferred_element_type=F32)
    l1 = lax.dot_general(wrh_ref[...], lo, NT_DIMS, preferred_element_type=F32)
    logits = l2[:N_EXPERTS] + l2[N_EXPERTS:] + l1
    bucket, w_lo, w_hi = _route_rows(logits, rb_ref[...])

    @pl.when(i == 0)
    def _():
        cnt_ref[...] = jnp.zeros_like(cnt_ref)

    onehot = jnp.where(lax.broadcasted_iota(jnp.int32, (BUCKET_ROWS, TM), 0).astype(F32) == bucket, 1.0, 0.0)
    earlier = (lax.broadcasted_iota(jnp.int32, (TM, TM), 0) < lax.broadcasted_iota(jnp.int32, (TM, TM), 1))
    before = jnp.dot(onehot.astype(BF16), jnp.where(earlier, 1.0, 0.0).astype(BF16),
                     preferred_element_type=F32)
    cnt = cnt_ref[...]
    rank = jnp.sum(onehot * (before + cnt[:, 0:1]), axis=0, keepdims=True)
    cnt_ref[...] = cnt + jnp.sum(onehot, axis=1, keepdims=True)

    r_ref[0:1, :] = bucket
    r_ref[1:2, :] = w_lo
    r_ref[2:3, :] = w_hi
    r_ref[3:4, :] = rank
    r_ref[4:ROUTE_ROWS, :] = jnp.zeros((ROUTE_ROWS - 4, TM), F32)
    rec = jnp.concatenate([r_ref[...], jnp.zeros((LANES - ROUTE_ROWS, TM), F32)], axis=0).T
    h2p_ref[:, :d // 2] = _pack_bf16_pairs(h2)
    h2p_ref[:, d // 2:] = pltpu.bitcast(rec, jnp.int32)


def _oproj(a, wo, x, mod, ln_g, ln_b, wr2, wrh, rbias, *, mode_in, n_batch, tiles_lat, alpha):
    d = x.shape[1]
    compact = mode_in == "lat"
    n_tiles = n_batch * tiles_lat if compact else n_batch * (tiles_lat + 1)
    blk, modrow, _ = _tile_maps(mode_in, tiles_lat, n_batch)
    row_in = pl.BlockSpec((TM, d), lambda i: (blk(i), 0))
    row_out = pl.BlockSpec((TM, d), lambda i: (i, 0))
    vec = pl.BlockSpec((1, d), lambda i: (0, 0))

    def modspec(chunk):
        return pl.BlockSpec((1, 1, d), lambda i: (modrow(i), 0, chunk))

    body = functools.partial(_oproj_body, alpha=alpha)
    return pl.pallas_call(
        body,
        grid=(n_tiles,),
        in_specs=[row_out, pl.BlockSpec((d, d), lambda i: (0, 0)), row_in, modspec(2), vec, vec,
                  modspec(4), modspec(3),
                  pl.BlockSpec((2 * N_EXPERTS, d), lambda i: (0, 0)),
                  pl.BlockSpec((N_EXPERTS, d), lambda i: (0, 0)),
                  pl.BlockSpec((N_EXPERTS, 1), lambda i: (0, 0))],
        out_specs=[row_out, pl.BlockSpec((TM, d // 2 + LANES), lambda i: (i, 0)),
                   pl.BlockSpec((ROUTE_ROWS, TM), lambda i: (0, i)),
                   pl.BlockSpec((BUCKET_ROWS, LANES), lambda i: (0, 0))],
        out_shape=[jax.ShapeDtypeStruct((n_tiles * TM, d), F32),
                   jax.ShapeDtypeStruct((n_tiles * TM, d // 2 + LANES), jnp.int32),
                   jax.ShapeDtypeStruct((ROUTE_ROWS, n_tiles * TM), F32),
                   jax.ShapeDtypeStruct((BUCKET_ROWS, LANES), F32)],
        compiler_params=_cparams("arbitrary"),
        name="oproj_ln_route",
    )(a, wo, x, mod, ln_g.reshape(1, d), ln_b.reshape(1, d), mod, mod, wr2, wrh, rbias)


def _scatter_body(pos_ref, h_ref, init_ref, xs_ref, stage, sem):
    del init_ref
    i = pl.program_id(0)
    slot = i % 2

    @pl.when(i >= 2)
    def _():
        _row_dma_wait(xs_ref, stage.at[slot], sem.at[slot])

    stage[slot] = h_ref[...]

    def body(r, carry):
        pltpu.make_async_copy(stage.at[slot, pl.ds(r, 1)], xs_ref.at[pl.ds(pos_ref[0, 0, r], 1)],
                              sem.at[slot]).start()
        return carry
    lax.fori_loop(0, TM, body, 0, unroll=8)

    @pl.when(i == pl.num_programs(0) - 1)
    def _():
        @pl.when(i >= 1)
        def _():
            _row_dma_wait(xs_ref, stage.at[1 - slot], sem.at[1 - slot])
        _row_dma_wait(xs_ref, stage.at[slot], sem.at[slot])


def _scatter_rows(h2p, pos3, n_rows_sorted):
    n_tiles = pos3.shape[0]
    width = h2p.shape[1]
    return pl.pallas_call(
        _scatter_body,
        grid=(n_tiles,),
        in_specs=[pl.BlockSpec((1, 1, TM), lambda i: (i, 0, 0), memory_space=pltpu.SMEM),
                  pl.BlockSpec((TM, width), lambda i: (i, 0)), pl.BlockSpec(memory_space=pl.ANY)],
        out_specs=pl.BlockSpec(memory_space=pl.ANY),
        out_shape=jax.ShapeDtypeStruct((n_rows_sorted, width), jnp.int32),
        scratch_shapes=[pltpu.VMEM((2, TM, width), jnp.int32), pltpu.SemaphoreType.DMA((2,))],
        input_output_aliases={2: 0},
        compiler_params=_cparams("arbitrary"),
        name="moe_scatter",
    )(pos3, h2p, jnp.zeros((n_rows_sorted, width), jnp.int32))


def _moe_body(ea_ref, eb_ref, act_ref, new_ref, x_ref, wg_a, wu_a, wd_a, wg_b, wu_b, wd_b, y_ref,
              wgu_s, wd_s, *, d_exp):
    i = pl.program_id(0)
    half = wd_a.shape[3] // 2

    @pl.when(new_ref[i] > 0)
    def _():
        for e, (wg, wu, wd) in enumerate(((wg_a, wu_a, wd_a), (wg_b, wu_b, wd_b))):
            wgu_s[e, :, :d_exp] = wg[0, 0].astype(BF16)
            wgu_s[e, :, d_exp:] = wu[0, 0].astype(BF16)
            wd_s[e] = wd[0, 0].astype(BF16)

    @pl.when(act_ref[i] > 0)
    def _():
        xw = x_ref[...]
        x = _unpack_bf16_pairs(xw[:, :half]).astype(BF16)
        rec = pltpu.bitcast(xw[:, half:], F32)

        def expert(e):
            gu = jnp.dot(x, wgu_s[e], preferred_element_type=F32)
            gate = gu[:, :d_exp]
            h = gate * jax.nn.sigmoid(gate) * gu[:, d_exp:] * rec[:, 1 + e:2 + e]
            return jnp.dot(h.astype(BF16), wd_s[e], preferred_element_type=F32)

        y_ref[...] = _pack_bf16_pairs(expert(0) + expert(1))

    @pl.when(act_ref[i] == 0)
    def _():
        y_ref[...] = jnp.zeros_like(y_ref)


def _moe(xs, w_gate, w_up, w_down, layer, ea, eb, act, new):
    n_rows, width = xs.shape
    d_exp, d = w_down.shape[2:]
    n_tiles = n_rows // TM
    body = functools.partial(_moe_body, d_exp=d_exp)
    in_w = lambda sel: [
        pl.BlockSpec((1, 1, d, d_exp), lambda i, ea, eb, act, new: (layer, sel(ea, eb)[i], 0, 0)),
        pl.BlockSpec((1, 1, d, d_exp), lambda i, ea, eb, act, new: (layer, sel(ea, eb)[i], 0, 0)),
        pl.BlockSpec((1, 1, d_exp, d), lambda i, ea, eb, act, new: (layer, sel(ea, eb)[i], 0, 0))]
    grid_spec = pltpu.PrefetchScalarGridSpec(
        num_scalar_prefetch=4,
        grid=(n_tiles,),
        in_specs=[pl.BlockSpec((TM, width), lambda i, ea, eb, act, new: (i, 0))]
        + in_w(lambda ea, eb: ea) + in_w(lambda ea, eb: eb),
        out_specs=pl.BlockSpec((TM, d // 2), lambda i, ea, eb, act, new: (i, 0)),
        scratch_shapes=[pltpu.VMEM((2, d, 2 * d_exp), BF16), pltpu.VMEM((2, d_exp, d), BF16)],
    )
    return pl.pallas_call(
        body, grid_spec=grid_spec,
        out_shape=jax.ShapeDtypeStruct((n_rows, d // 2), jnp.int32),
        compiler_params=_cparams("arbitrary"),
        name="moe_pairs",
    )(ea, eb, act, new, xs, w_gate, w_up, w_down, w_gate, w_up, w_down)


def _dispatch(route, counts, n_tiles):
    n = route.shape[1]
    bucket = route[0].astype(jnp.int32)
    rank = route[3].astype(jnp.int32)
    counts = counts[:N_BUCKETS, 0].astype(jnp.int32)
    padded = ((counts + TM - 1) // TM) * TM
    ends = jnp.cumsum(padded)
    off = ends - padded

    def lookup(idx, table):
        return jnp.sum(jnp.where(idx[:, None] == jnp.arange(table.shape[0])[None, :], table[None, :], 0), axis=1)

    pos = lookup(bucket, off) + rank
    tile_start = jnp.arange(n_tiles, dtype=jnp.int32) * TM
    active = tile_start < ends[-1]
    last_start = jnp.maximum(ends[-1] - TM, 0)
    tb = jnp.sum((jnp.where(active, tile_start, last_start)[:, None] >= ends[None, :]).astype(jnp.int32), axis=1)
    tb = jnp.minimum(tb, N_BUCKETS - 1)
    ea = (tb // len(PAIRS)) * GROUP_SIZE + lookup(tb % len(PAIRS), jnp.array([p[0] for p in PAIRS], jnp.int32))
    eb = (tb // len(PAIRS)) * GROUP_SIZE + lookup(tb % len(PAIRS), jnp.array([p[1] for p in PAIRS], jnp.int32))
    new = jnp.concatenate([jnp.ones((1,), jnp.int32), (tb[1:] != tb[:-1]).astype(jnp.int32)])
    return pos.reshape(n // TM, 1, TM), ea, eb, active.astype(jnp.int32), new


def _rope_tables(seq, ctx_len):
    t = jnp.arange(seq)
    quarter = HEAD_DIM // 4
    inv_freq = ROPE_THETA ** (-jnp.arange(quarter, dtype=F32) / quarter)
    ang_r = (t // GRID_W).astype(F32)[:, None] * inv_freq
    ang_c = (t % GRID_W).astype(F32)[:, None] * inv_freq
    cos_h = jnp.concatenate([jnp.cos(ang_r)] * 2 + [jnp.cos(ang_c)] * 2, axis=-1)
    sin_h = jnp.concatenate([-jnp.sin(ang_r), jnp.sin(ang_r), -jnp.sin(ang_c), jnp.sin(ang_c)], axis=-1)
    cos_t = jnp.concatenate([cos_h, cos_h], axis=-1)
    sin_t = jnp.concatenate([sin_h, sin_h], axis=-1)
    cos_t = jnp.concatenate([cos_t, jnp.ones((ctx_len, LANES), F32)], axis=0)
    sin_t = jnp.concatenate([sin_t, jnp.zeros((ctx_len, LANES), F32)], axis=0)
    return cos_t, sin_t


def kernel(x, c, ctx, c_ctx, w_ada, b_ada, wqkv_a, wo_a, sink_a, wqkv_b, wo_b, lambda_b, subln_b, ln_attn_g, ln_attn_b, ln_ffn_g, ln_ffn_b, w_router, router_bias, w_gate, w_up, w_down):
    n_batch, seq, d = x.shape
    ctx_len = ctx.shape[1]
    depth = w_ada.shape[0]
    assert seq % TM == 0 and ctx_len == TM and seq >= TM + 2 * WINDOW
    assert n_batch + 1 <= ADA_ROWS and d == A_Q_HEADS * HEAD_DIM
    tiles_lat = seq // TM
    alpha = (2 * depth) ** 0.25
    q_cols = A_Q_HEADS * HEAD_DIM
    kv_cols = A_KV_HEADS * HEAD_DIM

    cc = jnp.concatenate([c, c_ctx[None, :], jnp.zeros((ADA_ROWS - n_batch - 1, d), F32)], axis=0)
    mods = _ada_modulation(cc, w_ada, b_ada).reshape(depth, ADA_ROWS, 1, 6 * d)
    cos_t, sin_t = _rope_tables(seq, ctx_len)

    xs = jnp.concatenate([x, ctx], axis=1).reshape(n_batch * (seq + ctx_len), d)
    wr_hi = w_router.astype(BF16)
    wr_lo = (w_router - wr_hi.astype(F32)).astype(BF16)
    wr2 = jnp.concatenate([wr_hi.T, wr_lo.T], axis=0)
    wrh = wr_hi.T
    rbias = router_bias.reshape(N_EXPERTS, 1)

    y = None
    mode = "all"
    for i in range(depth):
        last = i == depth - 1
        j = i // 2
        if i % 2 == 0:
            w = wqkv_a[j].astype(BF16)
            n_rope, dup_from = q_cols + kv_cols, q_cols
        else:
            w = wqkv_b[j].astype(BF16)
            n_rope, dup_from = 2 * q_cols, w.shape[1]
        outs = _lnqkv(xs, y, mods[i - 1] if i else None, ln_ffn_g[i - 1] if i else None,
                      ln_ffn_b[i - 1] if i else None, mods[i], w, cos_t, sin_t,
                      mode_in="all", n_batch=n_batch, tiles_lat=tiles_lat,
                      alpha=alpha, n_rope=n_rope, dup_from=dup_from)
        if i:
            xs, qkv = outs
        else:
            (qkv,) = outs
        if i % 2 == 0:
            att = _attn_a(qkv, sink_a[j], n_batch=n_batch, seq=seq, ctx_len=ctx_len, with_ctx=not last)
            wo = wo_a[j].astype(BF16)
        else:
            lam_init = 0.8 - 0.6 * math.exp(-0.3 * i)
            att = _attn_b(qkv, lambda_b[j], subln_b[j], n_batch=n_batch, seq=seq, ctx_len=ctx_len,
                          with_ctx=not last, lam_init=lam_init)
            wo = wo_b[j].astype(BF16)
        mode = "lat" if last else "all"
        xs, h2p, route, counts = _oproj(att, wo, xs, mods[i], ln_attn_g[i], ln_attn_b[i], wr2, wrh, rbias,
                                        mode_in=mode, n_batch=n_batch, tiles_lat=tiles_lat, alpha=alpha)
        n_tiles = h2p.shape[0] // TM + N_BUCKETS
        pos3, ea, eb, act, new = _dispatch(route, counts, n_tiles)
        x_sorted = _scatter_rows(h2p, pos3, n_tiles * TM)
        y = (_moe(x_sorted, w_gate, w_up, w_down, i, ea, eb, act, new), pos3)

    (out,) = _lnqkv(xs, y, mods[depth - 1], ln_ffn_g[depth - 1], ln_ffn_b[depth - 1], None, None, None, None,
                    mode_in="compact", n_batch=n_batch, tiles_lat=tiles_lat, alpha=alpha)
    return out.reshape(n_batch, seq, d)
```

```python
import functools
import math

import jax
import jax.numpy as jnp
from jax import lax
from jax.experimental import pallas as pl
from jax.experimental.pallas import tpu as pltpu

F32 = jnp.float32
BF16 = jnp.bfloat16

HEAD_DIM = 64
A_Q_HEADS = 16
A_KV_HEADS = 4
A_GROUP = A_Q_HEADS // A_KV_HEADS
GRID_W = 64
WINDOW = 128
N_EXPERTS = 16
N_GROUPS = 4
GROUP_SIZE = N_EXPERTS // N_GROUPS
ROPE_THETA = 10000.0
LN_EPS = 1e-6
SUBLN_EPS = 1e-5
NEG_INF = -1e30
ATTN_SCALE = HEAD_DIM ** -0.5
LOG2E = math.log2(math.e)
VT_ROWS = 144

LANES = 128
TM = 256
DIFF_HEADS_PER_STEP = 4
ADA_ROWS = 24
VMEM_LIMIT = 56 * 1024 * 1024

PAIRS = ((0, 1), (0, 2), (0, 3), (1, 2), (1, 3), (2, 3))
N_BUCKETS = N_GROUPS * len(PAIRS)

BUCKET_ROWS = 32
ROUTE_ROWS = 8
HIGH16 = -65536

NT_DIMS = (((1,), (1,)), ((), ()))


def _cparams(*sem):
    return pltpu.CompilerParams(dimension_semantics=sem, vmem_limit_bytes=VMEM_LIMIT)


def _ada_body(c_ref, w_ref, b_ref, o_ref):
    c = c_ref[...]
    sc = c * jax.nn.sigmoid(c)
    o_ref[0] = jnp.dot(sc, w_ref[0], precision=lax.Precision.HIGHEST,
                       preferred_element_type=F32) + b_ref[0]


def _ada_modulation(cc, w_ada, b_ada):
    depth, d, d6 = w_ada.shape
    tn = 1536
    return pl.pallas_call(
        _ada_body,
        grid=(depth, d6 // tn),
        in_specs=[
            pl.BlockSpec((ADA_ROWS, d), lambda l, n: (0, 0)),
            pl.BlockSpec((1, d, tn), lambda l, n: (l, 0, n)),
            pl.BlockSpec((1, 1, tn), lambda l, n: (l, 0, n)),
        ],
        out_specs=pl.BlockSpec((1, ADA_ROWS, tn), lambda l, n: (l, 0, n)),
        out_shape=jax.ShapeDtypeStruct((depth, ADA_ROWS, d6), F32),
        compiler_params=_cparams("arbitrary", "arbitrary"),
        name="ada_mod",
    )(cc, w_ada, b_ada.reshape(depth, 1, d6))


def _tile_maps(mode, tiles_lat, n_batch):
    tpb = tiles_lat + 1
    if mode == "all":
        blk = lambda i: i
        modrow = lambda i: jnp.where(i % tpb == tiles_lat, n_batch, i // tpb)
        pos = lambda i: i % tpb
    elif mode == "lat":
        blk = lambda i: (i // tiles_lat) * tpb + i % tiles_lat
        modrow = lambda i: i // tiles_lat
        pos = lambda i: i % tiles_lat
    else:
        blk = lambda i: i
        modrow = lambda i: i // tiles_lat
        pos = lambda i: i % tiles_lat
    return blk, modrow, pos


def _layer_norm(z, g, b):
    mu = jnp.mean(z, axis=-1, keepdims=True)
    zc = z - mu
    var = jnp.mean(zc * zc, axis=-1, keepdims=True)
    return zc * lax.rsqrt(var + LN_EPS) * g + b


def _pack_bf16_pairs(v):
    half = v.shape[1] // 2
    bits = pltpu.bitcast(v.astype(BF16).astype(F32), jnp.int32)
    return lax.shift_right_logical(bits[:, :half], 16) | (bits[:, half:] & HIGH16)


def _unpack_bf16_pairs(w):
    return jnp.concatenate([pltpu.bitcast(lax.shift_left(w, 16), F32),
                            pltpu.bitcast(w & HIGH16, F32)], axis=1)


def _row_dma_wait(hbm_ref, buf_ref, sem):
    pltpu.make_async_copy(hbm_ref.at[pl.ds(0, TM)], buf_ref, sem).wait()


def _lnqkv_body(*refs, has_ln, has_qkv, alpha, n_rope, dup_from, n_out):
    refs = list(refs)
    x_ref = refs.pop(0)
    if has_ln:
        pos_ref, posn_ref, ys_ref, g2_ref, lng_ref, lnb_ref = refs[:6]
        refs = refs[6:]
        ybuf, ysem = refs[-2:]
        refs = refs[:-2]
    if has_qkv:
        sc_ref, sh_ref, w_ref, cos_ref, sin_ref = refs[:5]
        refs = refs[5:]
    x = x_ref[...]
    if has_ln:
        xo_ref = refs.pop(0)
        i = pl.program_id(0)
        slot = i % 2

        def fetch_row(p_ref, s, r):
            pltpu.make_async_copy(ys_ref.at[pl.ds(p_ref[0, 0, r], 1)], ybuf.at[s, pl.ds(r, 1)],
                                  ysem.at[s]).start()

        @pl.when(i == 0)
        def _():
            lax.fori_loop(0, TM, lambda r, c: (fetch_row(pos_ref, 0, r), c)[1], 0, unroll=8)

        _row_dma_wait(ys_ref, ybuf.at[slot], ysem.at[slot])
        for r in range(TM):
            fetch_row(posn_ref, 1 - slot, r)
        z = alpha * x + g2_ref[0] * _unpack_bf16_pairs(ybuf[slot])
        x = _layer_norm(z, lng_ref[...], lnb_ref[...])
        xo_ref[...] = x

    def finish():
        if has_ln:
            @pl.when(pl.program_id(0) == pl.num_programs(0) - 1)
            def _():
                _row_dma_wait(ys_ref, ybuf.at[1 - slot], ysem.at[1 - slot])

    if not has_qkv:
        finish()
        return
    qkv_ref = refs.pop(0)
    h = (x * (1.0 + sc_ref[0]) + sh_ref[0]).astype(BF16)
    r = jnp.dot(h, w_ref[...], preferred_element_type=F32)
    cos = cos_ref[...]
    sin = sin_ref[...]
    lane = lax.broadcasted_iota(jnp.int32, (TM, LANES), 1)
    first16 = (lane % 32) < 16
    low_half = lane < HEAD_DIM
    q_cols = A_Q_HEADS * HEAD_DIM
    dst = 0
    for c in range(n_out // LANES):
        seg = r[:, c * LANES:(c + 1) * LANES]
        col = c * LANES
        if col < q_cols:
            seg = seg * (ATTN_SCALE * LOG2E)
        if col < n_rope:
            rot = jnp.where(first16, pltpu.roll(seg, LANES - 16, 1), pltpu.roll(seg, 16, 1))
            seg = seg * cos + rot * sin
        if col >= dup_from:
            swapped = pltpu.roll(seg, HEAD_DIM, 1)
            qkv_ref[:, dst:dst + LANES] = jnp.where(low_half, seg, swapped).astype(BF16)
            qkv_ref[:, dst + LANES:dst + 2 * LANES] = jnp.where(low_half, swapped, seg).astype(BF16)
            dst += 2 * LANES
        else:
            qkv_ref[:, dst:dst + LANES] = seg.astype(BF16)
            dst += LANES
    finish()


def _lnqkv(x, y, mod_ln, ln_g, ln_b, mod_qkv, w, cos_t, sin_t, *, mode_in, n_batch, tiles_lat,
           alpha, n_rope=0, dup_from=0):
    d = x.shape[1]
    has_ln = y is not None
    has_qkv = w is not None
    n_tiles = (n_batch * (tiles_lat + 1)) if mode_in == "all" else n_batch * tiles_lat
    blk, modrow, pos = _tile_maps(mode_in, tiles_lat, n_batch)
    row = pl.BlockSpec((TM, d), lambda i: (blk(i), 0))
    vec = pl.BlockSpec((1, d), lambda i: (0, 0))

    def modspec(chunk):
        return pl.BlockSpec((1, 1, d), lambda i: (modrow(i), 0, chunk))

    args, in_specs, out_specs, out_shape, scratch = [x], [row], [], [], []
    n_out = 0
    if has_ln:
        ys, pos3 = y
        args += [pos3, pos3, ys, mod_ln, ln_g.reshape(1, d), ln_b.reshape(1, d)]
        in_specs += [pl.BlockSpec((1, 1, TM), lambda i: (i, 0, 0), memory_space=pltpu.SMEM),
                     pl.BlockSpec((1, 1, TM), lambda i: (jnp.minimum(i + 1, n_tiles - 1), 0, 0),
                                  memory_space=pltpu.SMEM),
                     pl.BlockSpec(memory_space=pl.ANY), modspec(5), vec, vec]
        out_specs.append(row)
        out_shape.append(jax.ShapeDtypeStruct(x.shape, F32))
        scratch = [pltpu.VMEM((2, TM) + ys.shape[1:], jnp.int32), pltpu.SemaphoreType.DMA((2,))]
    if has_qkv:
        n_out = w.shape[1]
        n_store = n_out + (n_out - dup_from)
        args += [mod_qkv, mod_qkv, w, cos_t, sin_t]
        in_specs += [modspec(1), modspec(0),
                     pl.BlockSpec((d, n_out), lambda i: (0, 0)),
                     pl.BlockSpec((TM, LANES), lambda i: (pos(i), 0)),
                     pl.BlockSpec((TM, LANES), lambda i: (pos(i), 0))]
        out_specs.append(pl.BlockSpec((TM, n_store), lambda i: (blk(i), 0)))
        out_shape.append(jax.ShapeDtypeStruct((x.shape[0], n_store), BF16))
    body = functools.partial(_lnqkv_body, has_ln=has_ln, has_qkv=has_qkv, alpha=alpha,
                             n_rope=n_rope, dup_from=dup_from, n_out=n_out)
    return pl.pallas_call(
        body, grid=(n_tiles,), in_specs=in_specs, out_specs=out_specs, out_shape=out_shape,
        scratch_shapes=scratch, compiler_params=_cparams("arbitrary"),
        name=("ln_" if has_ln else "") + ("qkv" if has_qkv else "out"),
    )(*args)


def _softmax_parts(scores, biases, sink):
    scores = [s if bias is None else s + bias[...] for s, bias in zip(scores, biases)]
    m = None
    for s in scores:
        sm = jnp.max(s, axis=-1, keepdims=True)
        m = sm if m is None else jnp.maximum(m, sm)
    if sink is not None:
        m = jnp.maximum(m, sink)
    probs = []
    denom = None
    for s in scores:
        e = jnp.exp2(s - m)
        es = jnp.sum(e, axis=-1, keepdims=True)
        denom = es if denom is None else denom + es
        probs.append(e.astype(BF16))
    if sink is not None:
        denom = denom + jnp.exp2(sink - m)
    return probs, 1.0 / denom


def _attn_a_body(sink_ref, q_ref, k_ref, v_ref, o_ref, bias_ref, *, seq, tiles_lat):
    qi = pl.program_id(1)
    lane = lax.broadcasted_iota(jnp.int32, (TM, LANES), 1)
    low_half = lane < HEAD_DIM
    span = TM + 2 * WINDOW
    rows = lax.broadcasted_iota(jnp.int32, (A_GROUP * TM, 1), 0)

    def run(parts):
        all_scores = []
        for h in range(A_KV_HEADS):
            qs = []
            for g in range(A_GROUP):
                q2 = q_ref[:, (h * A_GROUP + g) // 2 * LANES:((h * A_GROUP + g) // 2 + 1) * LANES]
                keep = low_half if g % 2 == 0 else jnp.logical_not(low_half)
                qs.append(jnp.where(keep, q2, jnp.zeros_like(q2)))
            qh = jnp.concatenate(qs, axis=0)
            scores = []
            for (r0, nr, _) in parts:
                kk = k_ref[pl.ds(r0, nr), h * LANES:(h + 1) * LANES]
                scores.append(lax.dot_general(qh, kk, NT_DIMS, preferred_element_type=F32))
            all_scores.append(scores)
        for h in range(A_KV_HEADS):
            sink = jnp.zeros((A_GROUP * TM, 1), F32)
            for g in range(A_GROUP):
                sink = jnp.where(rows // TM == g, sink_ref[h * A_GROUP + g] * LOG2E, sink)
            probs, inv = _softmax_parts(all_scores[h], [p[2] for p in parts], sink)
            acc = None
            for p, (r0, nr, _) in zip(probs, parts):
                vv = v_ref[pl.ds(r0, nr), h * LANES:(h + 1) * LANES]
                o = jnp.dot(p, vv, preferred_element_type=F32)
                acc = o if acc is None else acc + o
            acc = acc * inv
            for gp in range(A_GROUP // 2):
                grp = h * (A_GROUP // 2) + gp
                o_ref[:, grp * LANES:(grp + 1) * LANES] = jnp.where(
                    low_half, acc[2 * gp * TM:(2 * gp + 1) * TM], acc[(2 * gp + 1) * TM:(2 * gp + 2) * TM]
                ).astype(BF16)

    @pl.when(qi < tiles_lat)
    def _():
        start = pl.multiple_of(jnp.clip(qi * TM - WINDOW, 0, seq - span), WINDOW)
        qpos = qi * TM + lax.broadcasted_iota(jnp.int32, (TM, span), 0)
        kpos = start + lax.broadcasted_iota(jnp.int32, (TM, span), 1)
        bias = jnp.where(jnp.abs(kpos - qpos) <= WINDOW, 0.0, NEG_INF)
        for g in range(A_GROUP):
            bias_ref[g * TM:(g + 1) * TM, :] = bias
        run([(start, span, bias_ref), (seq, k_ref.shape[0] - seq, None)])

    @pl.when(qi >= tiles_lat)
    def _():
        run([(seq, k_ref.shape[0] - seq, None)])


def _attn_a(qkv, sink, *, n_batch, seq, ctx_len, with_ctx):
    tiles_lat = seq // TM
    tpb = tiles_lat + ctx_len // TM
    rows_b = seq + ctx_len
    q_cols = A_Q_HEADS * HEAD_DIM
    kv_cols = 2 * A_KV_HEADS * HEAD_DIM
    nq = tpb if with_ctx else tiles_lat
    body = functools.partial(_attn_a_body, seq=seq, tiles_lat=tiles_lat)
    return pl.pallas_call(
        body,
        grid=(n_batch, nq),
        in_specs=[
            pl.BlockSpec(memory_space=pltpu.SMEM),
            pl.BlockSpec((TM, q_cols), lambda b, i: (b * tpb + i, 0)),
            pl.BlockSpec((rows_b, kv_cols), lambda b, i: (b, q_cols // kv_cols)),
            pl.BlockSpec((rows_b, kv_cols), lambda b, i: (b, q_cols // kv_cols + 1)),
        ],
        out_specs=pl.BlockSpec((TM, q_cols), lambda b, i: (b * nq + i, 0)),
        out_shape=jax.ShapeDtypeStruct((n_batch * nq * TM, q_cols), BF16),
        scratch_shapes=[pltpu.VMEM((A_GROUP * TM, TM + 2 * WINDOW), F32)],
        compiler_params=_cparams("arbitrary", "arbitrary"),
        name="attn_window",
    )(sink, qkv, qkv, qkv)


def _attn_b_body(lam_ref, g_ref, q_ref, k_ref, v_ref, o_ref, vt_ref, *, seq, tiles_lat, lam_init, heads):
    qi = pl.program_id(2)
    lp = lam_ref[...]
    lam = (jnp.exp(jnp.sum(lp[0:1] * lp[1:2], axis=-1, keepdims=True))
           - jnp.exp(jnp.sum(lp[2:3] * lp[3:4], axis=-1, keepdims=True)) + lam_init)
    lane = lax.broadcasted_iota(jnp.int32, (TM, LANES), 1)
    low_half = lane < HEAD_DIM
    n_keys = k_ref.shape[0]

    @pl.when(qi == 0)
    def _():
        for hh in range(heads):
            vt_ref[hh, :LANES, :] = v_ref[:, hh * LANES:(hh + 1) * LANES].astype(F32).T.astype(BF16)
            row = lax.broadcasted_iota(jnp.int32, (VT_ROWS - LANES, n_keys), 0)
            vt_ref[hh, LANES:, :] = jnp.where(row == 0, 1.0, 0.0).astype(BF16)

    def run(r0, nr):
        scores = []
        for hh in range(heads):
            cols = slice(hh * LANES, (hh + 1) * LANES)
            q = q_ref[:, cols]
            kk = k_ref[pl.ds(r0, nr), cols]
            for qm in (jnp.where(low_half, q, jnp.zeros_like(q)), jnp.where(low_half, jnp.zeros_like(q), q)):
                scores.append(lax.dot_general(kk, qm, NT_DIMS, preferred_element_type=F32))
        for hh in range(heads):
            vt = vt_ref[hh, :, pl.ds(r0, nr)]
            outs = []
            for s in scores[2 * hh:2 * hh + 2]:
                e = jnp.exp2(s - jnp.max(s, axis=0, keepdims=True)).astype(BF16)
                ov = jnp.dot(vt, e, preferred_element_type=F32)
                outs.append((ov[:LANES], 1.0 / ov[LANES:LANES + 1]))
            o = outs[0][0] * outs[0][1] - outs[1][0] * (lam * outs[1][1])
            ms = jnp.mean(o * o, axis=0, keepdims=True)
            o = o * lax.rsqrt(ms + SUBLN_EPS) * g_ref[...] * (1.0 - lam_init)
            o_ref[:, hh * LANES:(hh + 1) * LANES] = o.T.astype(BF16)

    @pl.when(qi < tiles_lat)
    def _():
        run(0, k_ref.shape[0])

    @pl.when(qi >= tiles_lat)
    def _():
        run(seq, k_ref.shape[0] - seq)


def _attn_b(qkv, lam_params, subln_g, *, n_batch, seq, ctx_len, with_ctx, lam_init):
    tiles_lat = seq // TM
    tpb = tiles_lat + ctx_len // TM
    rows_b = seq + ctx_len
    n_heads = qkv.shape[1] // (3 * LANES)
    n_hgrp = n_heads // DIFF_HEADS_PER_STEP
    width = DIFF_HEADS_PER_STEP * LANES
    nq = tpb if with_ctx else tiles_lat
    body = functools.partial(_attn_b_body, seq=seq, tiles_lat=tiles_lat, lam_init=lam_init,
                             heads=DIFF_HEADS_PER_STEP)
    return pl.pallas_call(
        body,
        grid=(n_batch, n_hgrp, nq),
        in_specs=[
            pl.BlockSpec((4, HEAD_DIM), lambda b, h, i: (0, 0)),
            pl.BlockSpec((LANES, 1), lambda b, h, i: (0, 0)),
            pl.BlockSpec((TM, width), lambda b, h, i: (b * tpb + i, h)),
            pl.BlockSpec((rows_b, width), lambda b, h, i: (b, n_hgrp + h)),
            pl.BlockSpec((rows_b, width), lambda b, h, i: (b, 2 * n_hgrp + h)),
        ],
        out_specs=pl.BlockSpec((TM, width), lambda b, h, i: (b * nq + i, h)),
        out_shape=jax.ShapeDtypeStruct((n_batch * nq * TM, n_heads * LANES), BF16),
        compiler_params=_cparams("arbitrary", "arbitrary", "arbitrary"),
        scratch_shapes=[pltpu.VMEM((DIFF_HEADS_PER_STEP, VT_ROWS, rows_b), BF16)],
        name="attn_diff",
    )(lam_params, subln_g.reshape(LANES, 1), qkv, qkv, qkv)


def _route_rows(logits, bias):
    s = jax.nn.sigmoid(logits)
    biased = s + bias
    b = [biased[e:e + 1, :] for e in range(N_EXPERTS)]
    u = [s[e:e + 1, :] for e in range(N_EXPERTS)]
    gscore = []
    for g in range(N_GROUPS):
        best = None
        for (i, j) in PAIRS:
            t = b[GROUP_SIZE * g + i] + b[GROUP_SIZE * g + j]
            best = t if best is None else jnp.maximum(best, t)
        gscore.append(best)
    gsel = jnp.zeros(gscore[0].shape, jnp.int32)
    gbest = gscore[0]
    for g in range(1, N_GROUPS):
        better = gscore[g] > gbest
        gsel = jnp.where(better, g, gsel)
        gbest = jnp.where(better, gscore[g], gbest)

    def pick(rows, k):
        out = rows[GROUP_SIZE * (N_GROUPS - 1) + k]
        for g in range(N_GROUPS - 2, -1, -1):
            out = jnp.where(gsel == g, rows[GROUP_SIZE * g + k], out)
        return out

    v = [pick(b, k) for k in range(GROUP_SIZE)]
    w = [pick(u, k) for k in range(GROUP_SIZE)]
    sel = []
    for k in range(GROUP_SIZE):
        cnt = jnp.zeros(gsel.shape, jnp.int32)
        for j in range(GROUP_SIZE):
            if j == k:
                continue
            beats = (v[j] >= v[k]) if j < k else (v[j] > v[k])
            cnt = cnt + jnp.where(beats, 1, 0)
        sel.append(cnt < 2)
    pidx = jnp.zeros(gsel.shape, jnp.int32)
    u_lo = jnp.zeros(gbest.shape, F32)
    u_hi = jnp.zeros(gbest.shape, F32)
    for idx, (i, j) in enumerate(PAIRS):
        both = jnp.logical_and(sel[i], sel[j])
        pidx = jnp.where(both, idx, pidx)
        u_lo = jnp.where(both, w[i], u_lo)
        u_hi = jnp.where(both, w[j], u_hi)
    tot = u_lo + u_hi
    bucket = (gsel * len(PAIRS) + pidx).astype(F32)
    return bucket, u_lo / tot, u_hi / tot


def _oproj_body(a_ref, wo_ref, x_ref, g1_ref, lng_ref, lnb_ref, sc_ref, sh_ref, wr2_ref, wrh_ref, rb_ref,
                xo_ref, h2p_ref, r_ref, cnt_ref, *, alpha):
    i = pl.program_id(0)
    d = x_ref.shape[1]
    al = jnp.dot(a_ref[...], wo_ref[...], preferred_element_type=F32)
    z = alpha * x_ref[...] + g1_ref[0] * al
    xn = _layer_norm(z, lng_ref[...], lnb_ref[...])
    xo_ref[...] = xn
    h2 = xn * (1.0 + sc_ref[0]) + sh_ref[0]
    hi = h2.astype(BF16)
    lo = (h2 - hi.astype(F32)).astype(BF16)
    l2 = lax.dot_general(wr2_ref[...], hi, NT_DIMS, preferred_element_type=F32)
    l1 = lax.dot_general(wrh_ref[...], lo, NT_DIMS, preferred_element_type=F32)
    logits = l2[:N_EXPERTS] + l2[N_EXPERTS:] + l1
    bucket, w_lo, w_hi = _route_rows(logits, rb_ref[...])

    @pl.when(i == 0)
    def _():
        cnt_ref[...] = jnp.zeros_like(cnt_ref)

    onehot = jnp.where(lax.broadcasted_iota(jnp.int32, (BUCKET_ROWS, TM), 0).astype(F32) == bucket, 1.0, 0.0)
    earlier = (lax.broadcasted_iota(jnp.int32, (TM, TM), 0) < lax.broadcasted_iota(jnp.int32, (TM, TM), 1))
    before = jnp.dot(onehot.astype(BF16), jnp.where(earlier, 1.0, 0.0).astype(BF16),
                     preferred_element_type=F32)
    cnt = cnt_ref[...]
    rank = jnp.sum(onehot * (before + cnt[:, 0:1]), axis=0, keepdims=True)
    cnt_ref[...] = cnt + jnp.sum(onehot, axis=1, keepdims=True)

    r_ref[0:1, :] = bucket
    r_ref[1:2, :] = w_lo
    r_ref[2:3, :] = w_hi
    r_ref[3:4, :] = rank
    r_ref[4:ROUTE_ROWS, :] = jnp.zeros((ROUTE_ROWS - 4, TM), F32)
    rec = jnp.concatenate([r_ref[...], jnp.zeros((LANES - ROUTE_ROWS, TM), F32)], axis=0).T
    h2p_ref[:, :d // 2] = _pack_bf16_pairs(h2)
    h2p_ref[:, d // 2:] = pltpu.bitcast(rec, jnp.int32)


def _oproj(a, wo, x, mod, ln_g, ln_b, wr2, wrh, rbias, *, mode_in, n_batch, tiles_lat, alpha):
    d = x.shape[1]
    compact = mode_in == "lat"
    n_tiles = n_batch * tiles_lat if compact else n_batch * (tiles_lat + 1)
    blk, modrow, _ = _tile_maps(mode_in, tiles_lat, n_batch)
    row_in = pl.BlockSpec((TM, d), lambda i: (blk(i), 0))
    row_out = pl.BlockSpec((TM, d), lambda i: (i, 0))
    vec = pl.BlockSpec((1, d), lambda i: (0, 0))

    def modspec(chunk):
        return pl.BlockSpec((1, 1, d), lambda i: (modrow(i), 0, chunk))

    body = functools.partial(_oproj_body, alpha=alpha)
    return pl.pallas_call(
        body,
        grid=(n_tiles,),
        in_specs=[row_out, pl.BlockSpec((d, d), lambda i: (0, 0)), row_in, modspec(2), vec, vec,
                  modspec(4), modspec(3),
                  pl.BlockSpec((2 * N_EXPERTS, d), lambda i: (0, 0)),
                  pl.BlockSpec((N_EXPERTS, d), lambda i: (0, 0)),
                  pl.BlockSpec((N_EXPERTS, 1), lambda i: (0, 0))],
        out_specs=[row_out, pl.BlockSpec((TM, d // 2 + LANES), lambda i: (i, 0)),
                   pl.BlockSpec((ROUTE_ROWS, TM), lambda i: (0, i)),
                   pl.BlockSpec((BUCKET_ROWS, LANES), lambda i: (0, 0))],
        out_shape=[jax.ShapeDtypeStruct((n_tiles * TM, d), F32),
                   jax.ShapeDtypeStruct((n_tiles * TM, d // 2 + LANES), jnp.int32),
                   jax.ShapeDtypeStruct((ROUTE_ROWS, n_tiles * TM), F32),
                   jax.ShapeDtypeStruct((BUCKET_ROWS, LANES), F32)],
        compiler_params=_cparams("arbitrary"),
        name="oproj_ln_route",
    )(a, wo, x, mod, ln_g.reshape(1, d), ln_b.reshape(1, d), mod, mod, wr2, wrh, rbias)


def _scatter_body(pos_ref, h_ref, init_ref, xs_ref, stage, sem):
    del init_ref
    i = pl.program_id(0)
    slot = i % 2

    @pl.when(i >= 2)
    def _():
        _row_dma_wait(xs_ref, stage.at[slot], sem.at[slot])

    stage[slot] = h_ref[...]

    for r in range(TM):
        pltpu.make_async_copy(stage.at[slot, pl.ds(r, 1)], xs_ref.at[pl.ds(pos_ref[0, 0, r], 1)],
                              sem.at[slot]).start()

    @pl.when(i == pl.num_programs(0) - 1)
    def _():
        @pl.when(i >= 1)
        def _():
            _row_dma_wait(xs_ref, stage.at[1 - slot], sem.at[1 - slot])
        _row_dma_wait(xs_ref, stage.at[slot], sem.at[slot])


def _scatter_rows(h2p, pos3, n_rows_sorted):
    n_tiles = pos3.shape[0]
    width = h2p.shape[1]
    return pl.pallas_call(
        _scatter_body,
        grid=(n_tiles,),
        in_specs=[pl.BlockSpec((1, 1, TM), lambda i: (i, 0, 0), memory_space=pltpu.SMEM),
                  pl.BlockSpec((TM, width), lambda i: (i, 0)), pl.BlockSpec(memory_space=pl.ANY)],
        out_specs=pl.BlockSpec(memory_space=pl.ANY),
        out_shape=jax.ShapeDtypeStruct((n_rows_sorted, width), jnp.int32),
        scratch_shapes=[pltpu.VMEM((2, TM, width), jnp.int32), pltpu.SemaphoreType.DMA((2,))],
        input_output_aliases={2: 0},
        compiler_params=_cparams("arbitrary"),
        name="moe_scatter",
    )(pos3, h2p, jnp.zeros((n_rows_sorted, width), jnp.int32))


def _moe_body(ea_ref, eb_ref, act_ref, new_ref, par_ref, hasn_ref, nea_ref, neb_ref,
              x_ref, wg_hbm, wu_hbm, wd_hbm, y_ref, wg_f, wu_f, wd_f, wgu_s, wd_s, wsem, *, d_exp, layer):
    i = pl.program_id(0)
    half = wd_s.shape[2] // 2

    def weight_copies(slot, experts):
        return [pltpu.make_async_copy(hbm.at[layer, ex], buf.at[slot, e], wsem.at[slot])
                for e, ex in enumerate(experts) for hbm, buf in ((wg_hbm, wg_f), (wu_hbm, wu_f), (wd_hbm, wd_f))]

    @pl.when(i == 0)
    def _():
        for cp in weight_copies(par_ref[0], (ea_ref[0], eb_ref[0])):
            cp.start()

    @pl.when(new_ref[i] > 0)
    def _():
        slot = par_ref[i]
        for cp in weight_copies(slot, (ea_ref[i], eb_ref[i])):
            cp.wait()

        @pl.when(hasn_ref[i] > 0)
        def _():
            for cp in weight_copies(1 - slot, (nea_ref[i], neb_ref[i])):
                cp.start()

        for e in range(2):
            wgu_s[e, :, :d_exp] = wg_f[slot, e].astype(BF16)
            wgu_s[e, :, d_exp:] = wu_f[slot, e].astype(BF16)
            wd_s[e] = wd_f[slot, e].astype(BF16)

    @pl.when(act_ref[i] > 0)
    def _():
        xw = x_ref[...]
        x = _unpack_bf16_pairs(xw[:, :half]).astype(BF16)
        rec = pltpu.bitcast(xw[:, half:], F32)

        def expert(e):
            gu = jnp.dot(x, wgu_s[e], preferred_element_type=F32)
            gate = gu[:, :d_exp]
            h = gate * jax.nn.sigmoid(gate) * gu[:, d_exp:] * rec[:, 1 + e:2 + e]
            return jnp.dot(h.astype(BF16), wd_s[e], preferred_element_type=F32)

        y_ref[...] = _pack_bf16_pairs(expert(0) + expert(1))

    @pl.when(act_ref[i] == 0)
    def _():
        y_ref[...] = jnp.zeros_like(y_ref)


def _moe(xs, w_gate, w_up, w_down, layer, sched):
    n_rows, width = xs.shape
    d_exp, d = w_down.shape[2:]
    n_tiles = n_rows // TM
    body = functools.partial(_moe_body, d_exp=d_exp, layer=layer)
    grid_spec = pltpu.PrefetchScalarGridSpec(
        num_scalar_prefetch=len(sched),
        grid=(n_tiles,),
        in_specs=[pl.BlockSpec((TM, width), lambda i, *_: (i, 0)),
                  pl.BlockSpec(memory_space=pl.ANY), pl.BlockSpec(memory_space=pl.ANY),
                  pl.BlockSpec(memory_space=pl.ANY)],
        out_specs=pl.BlockSpec((TM, d // 2), lambda i, *_: (i, 0)),
        scratch_shapes=[pltpu.VMEM((2, 2, d, d_exp), F32), pltpu.VMEM((2, 2, d, d_exp), F32),
                        pltpu.VMEM((2, 2, d_exp, d), F32),
                        pltpu.VMEM((2, d, 2 * d_exp), BF16), pltpu.VMEM((2, d_exp, d), BF16),
                        pltpu.SemaphoreType.DMA((2,))],
    )
    return pl.pallas_call(
        body, grid_spec=grid_spec,
        out_shape=jax.ShapeDtypeStruct((n_rows, d // 2), jnp.int32),
        compiler_params=_cparams("arbitrary"),
        name="moe_pairs",
    )(*sched, xs, w_gate, w_up, w_down)


def _dispatch(route, counts, n_tiles):
    n = route.shape[1]
    bucket = route[0].astype(jnp.int32)
    rank = route[3].astype(jnp.int32)
    counts = counts[:N_BUCKETS, 0].astype(jnp.int32)
    padded = ((counts + TM - 1) // TM) * TM
    ends = jnp.cumsum(padded)
    off = ends - padded

    def lookup(idx, table):
        return jnp.sum(jnp.where(idx[:, None] == jnp.arange(table.shape[0])[None, :], table[None, :], 0), axis=1)

    pos = lookup(bucket, off) + rank
    tile_start = jnp.arange(n_tiles, dtype=jnp.int32) * TM
    active = tile_start < ends[-1]
    last_start = jnp.maximum(ends[-1] - TM, 0)
    tb = jnp.sum((jnp.where(active, tile_start, last_start)[:, None] >= ends[None, :]).astype(jnp.int32), axis=1)
    tb = jnp.minimum(tb, N_BUCKETS - 1)
    ea = (tb // len(PAIRS)) * GROUP_SIZE + lookup(tb % len(PAIRS), jnp.array([p[0] for p in PAIRS], jnp.int32))
    eb = (tb // len(PAIRS)) * GROUP_SIZE + lookup(tb % len(PAIRS), jnp.array([p[1] for p in PAIRS], jnp.int32))
    tiles = jnp.arange(n_tiles, dtype=jnp.int32)
    new = jnp.concatenate([jnp.ones((1,), jnp.int32), (tb[1:] != tb[:-1]).astype(jnp.int32)])
    first_at = jnp.where(new > 0, tiles, n_tiles)
    nxt = jnp.concatenate([lax.cummin(first_at, reverse=True)[1:], jnp.full((1,), n_tiles, jnp.int32)])
    has_next = (nxt < n_tiles).astype(jnp.int32)
    nxt = jnp.minimum(nxt, n_tiles - 1)
    parity = (jnp.cumsum(new) - 1) % 2
    sched = (ea, eb, active.astype(jnp.int32), new, parity.astype(jnp.int32), has_next,
             lookup(nxt, ea), lookup(nxt, eb))
    return pos.reshape(n // TM, 1, TM), sched


def _rope_tables(seq, ctx_len):
    t = jnp.arange(seq)
    quarter = HEAD_DIM // 4
    inv_freq = ROPE_THETA ** (-jnp.arange(quarter, dtype=F32) / quarter)
    ang_r = (t // GRID_W).astype(F32)[:, None] * inv_freq
    ang_c = (t % GRID_W).astype(F32)[:, None] * inv_freq
    cos_h = jnp.concatenate([jnp.cos(ang_r)] * 2 + [jnp.cos(ang_c)] * 2, axis=-1)
    sin_h = jnp.concatenate([-jnp.sin(ang_r), jnp.sin(ang_r), -jnp.sin(ang_c), jnp.sin(ang_c)], axis=-1)
    cos_t = jnp.concatenate([cos_h, cos_h], axis=-1)
    sin_t = jnp.concatenate([sin_h, sin_h], axis=-1)
    cos_t = jnp.concatenate([cos_t, jnp.ones((ctx_len, LANES), F32)], axis=0)
    sin_t = jnp.concatenate([sin_t, jnp.zeros((ctx_len, LANES), F32)], axis=0)
    return cos_t, sin_t


def kernel(x, c, ctx, c_ctx, w_ada, b_ada, wqkv_a, wo_a, sink_a, wqkv_b, wo_b, lambda_b, subln_b, ln_attn_g, ln_attn_b, ln_ffn_g, ln_ffn_b, w_router, router_bias, w_gate, w_up, w_down):
    n_batch, seq, d = x.shape
    ctx_len = ctx.shape[1]
    depth = w_ada.shape[0]
    assert seq % TM == 0 and ctx_len == TM and seq >= TM + 2 * WINDOW
    assert n_batch + 1 <= ADA_ROWS and d == A_Q_HEADS * HEAD_DIM
    tiles_lat = seq // TM
    alpha = (2 * depth) ** 0.25
    q_cols = A_Q_HEADS * HEAD_DIM
    kv_cols = A_KV_HEADS * HEAD_DIM

    cc = jnp.concatenate([c, c_ctx[None, :], jnp.zeros((ADA_ROWS - n_batch - 1, d), F32)], axis=0)
    mods = _ada_modulation(cc, w_ada, b_ada).reshape(depth, ADA_ROWS, 1, 6 * d)
    cos_t, sin_t = _rope_tables(seq, ctx_len)

    xs = jnp.concatenate([x, ctx], axis=1).reshape(n_batch * (seq + ctx_len), d)
    wr_hi = w_router.astype(BF16)
    wr_lo = (w_router - wr_hi.astype(F32)).astype(BF16)
    wr2 = jnp.concatenate([wr_hi.T, wr_lo.T], axis=0)
    wrh = wr_hi.T
    rbias = router_bias.reshape(N_EXPERTS, 1)

    y = None
    mode = "all"
    for i in range(depth):
        last = i == depth - 1
        j = i // 2
        if i % 2 == 0:
            w = wqkv_a[j].astype(BF16)
            n_rope, dup_from = q_cols + kv_cols, q_cols
        else:
            w = wqkv_b[j].astype(BF16)
            n_rope, dup_from = 2 * q_cols, w.shape[1]
        outs = _lnqkv(xs, y, mods[i - 1] if i else None, ln_ffn_g[i - 1] if i else None,
                      ln_ffn_b[i - 1] if i else None, mods[i], w, cos_t, sin_t,
                      mode_in="all", n_batch=n_batch, tiles_lat=tiles_lat,
                      alpha=alpha, n_rope=n_rope, dup_from=dup_from)
        if i:
            xs, qkv = outs
        else:
            (qkv,) = outs
        if i % 2 == 0:
            att = _attn_a(qkv, sink_a[j], n_batch=n_batch, seq=seq, ctx_len=ctx_len, with_ctx=not last)
            wo = wo_a[j].astype(BF16)
        else:
            lam_init = 0.8 - 0.6 * math.exp(-0.3 * i)
            att = _attn_b(qkv, lambda_b[j], subln_b[j], n_batch=n_batch, seq=seq, ctx_len=ctx_len,
                          with_ctx=not last, lam_init=lam_init)
            wo = wo_b[j].astype(BF16)
        mode = "lat" if last else "all"
        xs, h2p, route, counts = _oproj(att, wo, xs, mods[i], ln_attn_g[i], ln_attn_b[i], wr2, wrh, rbias,
                                        mode_in=mode, n_batch=n_batch, tiles_lat=tiles_lat, alpha=alpha)
        n_tiles = h2p.shape[0] // TM + N_BUCKETS
        pos3, sched = _dispatch(route, counts, n_tiles)
        x_sorted = _scatter_rows(h2p, pos3, n_tiles * TM)
        y = (_moe(x_sorted, w_gate, w_up, w_down, i, sched), pos3)

    (out,) = _lnqkv(xs, y, mods[depth - 1], ln_ffn_g[depth - 1], ln_ffn_b[depth - 1], None, None, None, None,
                    mode_in="compact", n_batch=n_batch, tiles_lat=tiles_lat, alpha=alpha)
    return out.reshape(n_batch, seq, d)
```

```python
import functools
import math

import jax
import jax.numpy as jnp
from jax import lax
from jax.experimental import pallas as pl
from jax.experimental.pallas import tpu as pltpu

F32 = jnp.float32
BF16 = jnp.bfloat16

HEAD_DIM = 64
A_Q_HEADS = 16
A_KV_HEADS = 4
A_GROUP = A_Q_HEADS // A_KV_HEADS
GRID_W = 64
WINDOW = 128
N_EXPERTS = 16
N_GROUPS = 4
GROUP_SIZE = N_EXPERTS // N_GROUPS
ROPE_THETA = 10000.0
LN_EPS = 1e-6
SUBLN_EPS = 1e-5
NEG_INF = -1e30
ATTN_SCALE = HEAD_DIM ** -0.5
LOG2E = math.log2(math.e)
VT_ROWS = 144

LANES = 128
TM = 256
DIFF_HEADS_PER_STEP = 4
ADA_ROWS = 24
VMEM_LIMIT = 56 * 1024 * 1024

PAIRS = ((0, 1), (0, 2), (0, 3), (1, 2), (1, 3), (2, 3))
N_BUCKETS = N_GROUPS * len(PAIRS)

BUCKET_ROWS = 32
ROUTE_ROWS = 8
HIGH16 = -65536

NT_DIMS = (((1,), (1,)), ((), ()))


def _cparams(*sem):
    return pltpu.CompilerParams(dimension_semantics=sem, vmem_limit_bytes=VMEM_LIMIT)


def _ada_body(c_ref, w_ref, b_ref, o_ref):
    c = c_ref[...]
    sc = c * jax.nn.sigmoid(c)
    o_ref[0] = jnp.dot(sc, w_ref[0], precision=lax.Precision.HIGHEST,
                       preferred_element_type=F32) + b_ref[0]


def _ada_modulation(cc, w_ada, b_ada):
    depth, d, d6 = w_ada.shape
    tn = 1536
    return pl.pallas_call(
        _ada_body,
        grid=(depth, d6 // tn),
        in_specs=[
            pl.BlockSpec((ADA_ROWS, d), lambda l, n: (0, 0)),
            pl.BlockSpec((1, d, tn), lambda l, n: (l, 0, n)),
            pl.BlockSpec((1, 1, tn), lambda l, n: (l, 0, n)),
        ],
        out_specs=pl.BlockSpec((1, ADA_ROWS, tn), lambda l, n: (l, 0, n)),
        out_shape=jax.ShapeDtypeStruct((depth, ADA_ROWS, d6), F32),
        compiler_params=_cparams("arbitrary", "arbitrary"),
        name="ada_mod",
    )(cc, w_ada, b_ada.reshape(depth, 1, d6))


def _tile_maps(mode, tiles_lat, n_batch):
    tpb = tiles_lat + 1
    if mode == "all":
        blk = lambda i: i
        modrow = lambda i: jnp.where(i % tpb == tiles_lat, n_batch, i // tpb)
        pos = lambda i: i % tpb
    elif mode == "lat":
        blk = lambda i: (i // tiles_lat) * tpb + i % tiles_lat
        modrow = lambda i: i // tiles_lat
        pos = lambda i: i % tiles_lat
    else:
        blk = lambda i: i
        modrow = lambda i: i // tiles_lat
        pos = lambda i: i % tiles_lat
    return blk, modrow, pos


def _layer_norm(z, g, b):
    mu = jnp.mean(z, axis=-1, keepdims=True)
    zc = z - mu
    var = jnp.mean(zc * zc, axis=-1, keepdims=True)
    return zc * lax.rsqrt(var + LN_EPS) * g + b


def _pack_bf16_pairs(v):
    half = v.shape[1] // 2
    bits = pltpu.bitcast(v.astype(BF16).astype(F32), jnp.int32)
    return lax.shift_right_logical(bits[:, :half], 16) | (bits[:, half:] & HIGH16)


def _unpack_bf16_pairs(w):
    return jnp.concatenate([pltpu.bitcast(lax.shift_left(w, 16), F32),
                            pltpu.bitcast(w & HIGH16, F32)], axis=1)


def _row_dma_wait(hbm_ref, buf_ref, sem):
    pltpu.make_async_copy(hbm_ref.at[pl.ds(0, TM)], buf_ref, sem).wait()


def _lnqkv_body(*refs, has_ln, has_qkv, alpha, n_rope, dup_from, n_out):
    refs = list(refs)
    x_ref = refs.pop(0)
    if has_ln:
        pos_ref, posn_ref, ys_ref, g2_ref, lng_ref, lnb_ref = refs[:6]
        refs = refs[6:]
        ybuf, ysem = refs[-2:]
        refs = refs[:-2]
    if has_qkv:
        sc_ref, sh_ref, w_ref, cos_ref, sin_ref = refs[:5]
        refs = refs[5:]
    x = x_ref[...]
    if has_ln:
        xo_ref = refs.pop(0)
        i = pl.program_id(0)
        slot = i % 2

        def fetch_row(p_ref, s, r, priority=0):
            pltpu.make_async_copy(ys_ref.at[pl.ds(p_ref[0, 0, r], 1)], ybuf.at[s, pl.ds(r, 1)],
                                  ysem.at[s]).start(priority=priority)

        @pl.when(i == 0)
        def _():
            lax.fori_loop(0, TM, lambda r, c: (fetch_row(pos_ref, 0, r), c)[1], 0, unroll=8)

        _row_dma_wait(ys_ref, ybuf.at[slot], ysem.at[slot])
        for r in range(TM):
            fetch_row(posn_ref, 1 - slot, r, priority=r % 2)
        z = alpha * x + g2_ref[0] * _unpack_bf16_pairs(ybuf[slot])
        x = _layer_norm(z, lng_ref[...], lnb_ref[...])
        xo_ref[...] = x

    def finish():
        if has_ln:
            @pl.when(pl.program_id(0) == pl.num_programs(0) - 1)
            def _():
                _row_dma_wait(ys_ref, ybuf.at[1 - slot], ysem.at[1 - slot])

    if not has_qkv:
        finish()
        return
    qkv_ref = refs.pop(0)
    h = (x * (1.0 + sc_ref[0]) + sh_ref[0]).astype(BF16)
    r = jnp.dot(h, w_ref[...], preferred_element_type=F32)
    cos = cos_ref[...]
    sin = sin_ref[...]
    lane = lax.broadcasted_iota(jnp.int32, (TM, LANES), 1)
    first16 = (lane % 32) < 16
    low_half = lane < HEAD_DIM
    q_cols = A_Q_HEADS * HEAD_DIM
    dst = 0
    for c in range(n_out // LANES):
        seg = r[:, c * LANES:(c + 1) * LANES]
        col = c * LANES
        if col < q_cols:
            seg = seg * (ATTN_SCALE * LOG2E)
        if col < n_rope:
            rot = jnp.where(first16, pltpu.roll(seg, LANES - 16, 1), pltpu.roll(seg, 16, 1))
            seg = seg * cos + rot * sin
        if col >= dup_from:
            swapped = pltpu.roll(seg, HEAD_DIM, 1)
            qkv_ref[:, dst:dst + LANES] = jnp.where(low_half, seg, swapped).astype(BF16)
            qkv_ref[:, dst + LANES:dst + 2 * LANES] = jnp.where(low_half, swapped, seg).astype(BF16)
            dst += 2 * LANES
        else:
            qkv_ref[:, dst:dst + LANES] = seg.astype(BF16)
            dst += LANES
    finish()


def _lnqkv(x, y, mod_ln, ln_g, ln_b, mod_qkv, w, cos_t, sin_t, *, mode_in, n_batch, tiles_lat,
           alpha, n_rope=0, dup_from=0):
    d = x.shape[1]
    has_ln = y is not None
    has_qkv = w is not None
    n_tiles = (n_batch * (tiles_lat + 1)) if mode_in == "all" else n_batch * tiles_lat
    blk, modrow, pos = _tile_maps(mode_in, tiles_lat, n_batch)
    row = pl.BlockSpec((TM, d), lambda i: (blk(i), 0))
    vec = pl.BlockSpec((1, d), lambda i: (0, 0))

    def modspec(chunk):
        return pl.BlockSpec((1, 1, d), lambda i: (modrow(i), 0, chunk))

    args, in_specs, out_specs, out_shape, scratch = [x], [row], [], [], []
    n_out = 0
    if has_ln:
        ys, pos3 = y
        args += [pos3, pos3, ys, mod_ln, ln_g.reshape(1, d), ln_b.reshape(1, d)]
        in_specs += [pl.BlockSpec((1, 1, TM), lambda i: (i, 0, 0), memory_space=pltpu.SMEM),
                     pl.BlockSpec((1, 1, TM), lambda i: (jnp.minimum(i + 1, n_tiles - 1), 0, 0),
                                  memory_space=pltpu.SMEM),
                     pl.BlockSpec(memory_space=pl.ANY), modspec(5), vec, vec]
        out_specs.append(row)
        out_shape.append(jax.ShapeDtypeStruct(x.shape, F32))
        scratch = [pltpu.VMEM((2, TM) + ys.shape[1:], jnp.int32), pltpu.SemaphoreType.DMA((2,))]
    if has_qkv:
        n_out = w.shape[1]
        n_store = n_out + (n_out - dup_from)
        args += [mod_qkv, mod_qkv, w, cos_t, sin_t]
        in_specs += [modspec(1), modspec(0),
                     pl.BlockSpec((d, n_out), lambda i: (0, 0)),
                     pl.BlockSpec((TM, LANES), lambda i: (pos(i), 0)),
                     pl.BlockSpec((TM, LANES), lambda i: (pos(i), 0))]
        out_specs.append(pl.BlockSpec((TM, n_store), lambda i: (blk(i), 0)))
        out_shape.append(jax.ShapeDtypeStruct((x.shape[0], n_store), BF16))
    body = functools.partial(_lnqkv_body, has_ln=has_ln, has_qkv=has_qkv, alpha=alpha,
                             n_rope=n_rope, dup_from=dup_from, n_out=n_out)
    return pl.pallas_call(
        body, grid=(n_tiles,), in_specs=in_specs, out_specs=out_specs, out_shape=out_shape,
        scratch_shapes=scratch, compiler_params=_cparams("arbitrary"),
        name=("ln_" if has_ln else "") + ("qkv" if has_qkv else "out"),
    )(*args)


def _softmax_parts(scores, biases, sink):
    scores = [s if bias is None else s + bias[...] for s, bias in zip(scores, biases)]
    m = None
    for s in scores:
        sm = jnp.max(s, axis=-1, keepdims=True)
        m = sm if m is None else jnp.maximum(m, sm)
    if sink is not None:
        m = jnp.maximum(m, sink)
    probs = []
    denom = None
    for s in scores:
        e = jnp.exp2(s - m)
        es = jnp.sum(e, axis=-1, keepdims=True)
        denom = es if denom is None else denom + es
        probs.append(e.astype(BF16))
    if sink is not None:
        denom = denom + jnp.exp2(sink - m)
    return probs, 1.0 / denom


def _attn_a_body(sink_ref, q_ref, k_ref, v_ref, o_ref, bias_ref, *, seq, tiles_lat):
    qi = pl.program_id(1)
    lane = lax.broadcasted_iota(jnp.int32, (TM, LANES), 1)
    low_half = lane < HEAD_DIM
    span = TM + 2 * WINDOW
    rows = lax.broadcasted_iota(jnp.int32, (A_GROUP * TM, 1), 0)

    def run(parts):
        all_scores = []
        for h in range(A_KV_HEADS):
            qs = []
            for g in range(A_GROUP):
                q2 = q_ref[:, (h * A_GROUP + g) // 2 * LANES:((h * A_GROUP + g) // 2 + 1) * LANES]
                keep = low_half if g % 2 == 0 else jnp.logical_not(low_half)
                qs.append(jnp.where(keep, q2, jnp.zeros_like(q2)))
            qh = jnp.concatenate(qs, axis=0)
            scores = []
            for (r0, nr, _) in parts:
                kk = k_ref[pl.ds(r0, nr), h * LANES:(h + 1) * LANES]
                scores.append(lax.dot_general(qh, kk, NT_DIMS, preferred_element_type=F32))
            all_scores.append(scores)
        for h in range(A_KV_HEADS):
            sink = jnp.zeros((A_GROUP * TM, 1), F32)
            for g in range(A_GROUP):
                sink = jnp.where(rows // TM == g, sink_ref[h * A_GROUP + g] * LOG2E, sink)
            probs, inv = _softmax_parts(all_scores[h], [p[2] for p in parts], sink)
            acc = None
            for p, (r0, nr, _) in zip(probs, parts):
                vv = v_ref[pl.ds(r0, nr), h * LANES:(h + 1) * LANES]
                o = jnp.dot(p, vv, preferred_element_type=F32)
                acc = o if acc is None else acc + o
            acc = acc * inv
            for gp in range(A_GROUP // 2):
                grp = h * (A_GROUP // 2) + gp
                o_ref[:, grp * LANES:(grp + 1) * LANES] = jnp.where(
                    low_half, acc[2 * gp * TM:(2 * gp + 1) * TM], acc[(2 * gp + 1) * TM:(2 * gp + 2) * TM]
                ).astype(BF16)

    @pl.when(qi < tiles_lat)
    def _():
        start = pl.multiple_of(jnp.clip(qi * TM - WINDOW, 0, seq - span), WINDOW)
        qpos = qi * TM + lax.broadcasted_iota(jnp.int32, (TM, span), 0)
        kpos = start + lax.broadcasted_iota(jnp.int32, (TM, span), 1)
        bias = jnp.where(jnp.abs(kpos - qpos) <= WINDOW, 0.0, NEG_INF)
        for g in range(A_GROUP):
            bias_ref[g * TM:(g + 1) * TM, :] = bias
        run([(start, span, bias_ref), (seq, k_ref.shape[0] - seq, None)])

    @pl.when(qi >= tiles_lat)
    def _():
        run([(seq, k_ref.shape[0] - seq, None)])


def _attn_a(qkv, sink, *, n_batch, seq, ctx_len, with_ctx):
    tiles_lat = seq // TM
    tpb = tiles_lat + ctx_len // TM
    rows_b = seq + ctx_len
    q_cols = A_Q_HEADS * HEAD_DIM
    kv_cols = 2 * A_KV_HEADS * HEAD_DIM
    nq = tpb if with_ctx else tiles_lat
    body = functools.partial(_attn_a_body, seq=seq, tiles_lat=tiles_lat)
    return pl.pallas_call(
        body,
        grid=(n_batch, nq),
        in_specs=[
            pl.BlockSpec(memory_space=pltpu.SMEM),
            pl.BlockSpec((TM, q_cols), lambda b, i: (b * tpb + i, 0)),
            pl.BlockSpec((rows_b, kv_cols), lambda b, i: (b, q_cols // kv_cols)),
            pl.BlockSpec((rows_b, kv_cols), lambda b, i: (b, q_cols // kv_cols + 1)),
        ],
        out_specs=pl.BlockSpec((TM, q_cols), lambda b, i: (b * nq + i, 0)),
        out_shape=jax.ShapeDtypeStruct((n_batch * nq * TM, q_cols), BF16),
        scratch_shapes=[pltpu.VMEM((A_GROUP * TM, TM + 2 * WINDOW), F32)],
        compiler_params=_cparams("arbitrary", "arbitrary"),
        name="attn_window",
    )(sink, qkv, qkv, qkv)


def _attn_b_body(lam_ref, g_ref, q_ref, k_ref, v_ref, o_ref, vt_ref, *, seq, tiles_lat, lam_init, heads):
    qi = pl.program_id(2)
    lp = lam_ref[...]
    lam = (jnp.exp(jnp.sum(lp[0:1] * lp[1:2], axis=-1, keepdims=True))
           - jnp.exp(jnp.sum(lp[2:3] * lp[3:4], axis=-1, keepdims=True)) + lam_init)
    lane = lax.broadcasted_iota(jnp.int32, (TM, LANES), 1)
    low_half = lane < HEAD_DIM
    n_keys = k_ref.shape[0]

    @pl.when(qi == 0)
    def _():
        for hh in range(heads):
            vt_ref[hh, :LANES, :] = v_ref[:, hh * LANES:(hh + 1) * LANES].astype(F32).T.astype(BF16)
            row = lax.broadcasted_iota(jnp.int32, (VT_ROWS - LANES, n_keys), 0)
            vt_ref[hh, LANES:, :] = jnp.where(row == 0, 1.0, 0.0).astype(BF16)

    def run(r0, nr):
        scores = []
        for hh in range(heads):
            cols = slice(hh * LANES, (hh + 1) * LANES)
            q = q_ref[:, cols]
            kk = k_ref[pl.ds(r0, nr), cols]
            for qm in (jnp.where(low_half, q, jnp.zeros_like(q)), jnp.where(low_half, jnp.zeros_like(q), q)):
                scores.append(lax.dot_general(kk, qm, NT_DIMS, preferred_element_type=F32))
        for hh in range(heads):
            vt = vt_ref[hh, :, pl.ds(r0, nr)]
            outs = []
            for s in scores[2 * hh:2 * hh + 2]:
                e = jnp.exp2(s - jnp.max(s, axis=0, keepdims=True)).astype(BF16)
                ov = jnp.dot(vt, e, preferred_element_type=F32)
                outs.append((ov[:LANES], 1.0 / ov[LANES:LANES + 1]))
            o = outs[0][0] * outs[0][1] - outs[1][0] * (lam * outs[1][1])
            ms = jnp.mean(o * o, axis=0, keepdims=True)
            o = o * lax.rsqrt(ms + SUBLN_EPS) * g_ref[...] * (1.0 - lam_init)
            o_ref[:, hh * LANES:(hh + 1) * LANES] = o.T.astype(BF16)

    @pl.when(qi < tiles_lat)
    def _():
        run(0, k_ref.shape[0])

    @pl.when(qi >= tiles_lat)
    def _():
        run(seq, k_ref.shape[0] - seq)


def _attn_b(qkv, lam_params, subln_g, *, n_batch, seq, ctx_len, with_ctx, lam_init):
    tiles_lat = seq // TM
    tpb = tiles_lat + ctx_len // TM
    rows_b = seq + ctx_len
    n_heads = qkv.shape[1] // (3 * LANES)
    n_hgrp = n_heads // DIFF_HEADS_PER_STEP
    width = DIFF_HEADS_PER_STEP * LANES
    nq = tpb if with_ctx else tiles_lat
    body = functools.partial(_attn_b_body, seq=seq, tiles_lat=tiles_lat, lam_init=lam_init,
                             heads=DIFF_HEADS_PER_STEP)
    return pl.pallas_call(
        body,
        grid=(n_batch, n_hgrp, nq),
        in_specs=[
            pl.BlockSpec((4, HEAD_DIM), lambda b, h, i: (0, 0)),
            pl.BlockSpec((LANES, 1), lambda b, h, i: (0, 0)),
            pl.BlockSpec((TM, width), lambda b, h, i: (b * tpb + i, h)),
            pl.BlockSpec((rows_b, width), lambda b, h, i: (b, n_hgrp + h)),
            pl.BlockSpec((rows_b, width), lambda b, h, i: (b, 2 * n_hgrp + h)),
        ],
        out_specs=pl.BlockSpec((TM, width), lambda b, h, i: (b * nq + i, h)),
        out_shape=jax.ShapeDtypeStruct((n_batch * nq * TM, n_heads * LANES), BF16),
        compiler_params=_cparams("arbitrary", "arbitrary", "arbitrary"),
        scratch_shapes=[pltpu.VMEM((DIFF_HEADS_PER_STEP, VT_ROWS, rows_b), BF16)],
        name="attn_diff",
    )(lam_params, subln_g.reshape(LANES, 1), qkv, qkv, qkv)


def _route_rows(logits, bias):
    s = jax.nn.sigmoid(logits)
    biased = s + bias
    b = [biased[e:e + 1, :] for e in range(N_EXPERTS)]
    u = [s[e:e + 1, :] for e in range(N_EXPERTS)]
    gscore = []
    for g in range(N_GROUPS):
        best = None
        for (i, j) in PAIRS:
            t = b[GROUP_SIZE * g + i] + b[GROUP_SIZE * g + j]
            best = t if best is None else jnp.maximum(best, t)
        gscore.append(best)
    gsel = jnp.zeros(gscore[0].shape, jnp.int32)
    gbest = gscore[0]
    for g in range(1, N_GROUPS):
        better = gscore[g] > gbest
        gsel = jnp.where(better, g, gsel)
        gbest = jnp.where(better, gscore[g], gbest)

    def pick(rows, k):
        out = rows[GROUP_SIZE * (N_GROUPS - 1) + k]
        for g in range(N_GROUPS - 2, -1, -1):
            out = jnp.where(gsel == g, rows[GROUP_SIZE * g + k], out)
        return out

    v = [pick(b, k) for k in range(GROUP_SIZE)]
    w = [pick(u, k) for k in range(GROUP_SIZE)]
    sel = []
    for k in range(GROUP_SIZE):
        cnt = jnp.zeros(gsel.shape, jnp.int32)
        for j in range(GROUP_SIZE):
            if j == k:
                continue
            beats = (v[j] >= v[k]) if j < k else (v[j] > v[k])
            cnt = cnt + jnp.where(beats, 1, 0)
        sel.append(cnt < 2)
    pidx = jnp.zeros(gsel.shape, jnp.int32)
    u_lo = jnp.zeros(gbest.shape, F32)
    u_hi = jnp.zeros(gbest.shape, F32)
    for idx, (i, j) in enumerate(PAIRS):
        both = jnp.logical_and(sel[i], sel[j])
        pidx = jnp.where(both, idx, pidx)
        u_lo = jnp.where(both, w[i], u_lo)
        u_hi = jnp.where(both, w[j], u_hi)
    tot = u_lo + u_hi
    bucket = (gsel * len(PAIRS) + pidx).astype(F32)
    return bucket, u_lo / tot, u_hi / tot


def _oproj_body(a_ref, wo_ref, x_ref, g1_ref, lng_ref, lnb_ref, sc_ref, sh_ref, wr2_ref, wrh_ref, rb_ref,
                xo_ref, h2p_ref, r_ref, cnt_ref, *, alpha):
    i = pl.program_id(0)
    d = x_ref.shape[1]
    al = jnp.dot(a_ref[...], wo_ref[...], preferred_element_type=F32)
    z = alpha * x_ref[...] + g1_ref[0] * al
    xn = _layer_norm(z, lng_ref[...], lnb_ref[...])
    xo_ref[...] = xn
    h2 = xn * (1.0 + sc_ref[0]) + sh_ref[0]
    hi = h2.astype(BF16)
    lo = (h2 - hi.astype(F32)).astype(BF16)
    l2 = lax.dot_general(wr2_ref[...], hi, NT_DIMS, preferred_element_type=F32)
    l1 = lax.dot_general(wrh_ref[...], lo, NT_DIMS, preferred_element_type=F32)
    logits = l2[:N_EXPERTS] + l2[N_EXPERTS:] + l1
    bucket, w_lo, w_hi = _route_rows(logits, rb_ref[...])

    @pl.when(i == 0)
    def _():
        cnt_ref[...] = jnp.zeros_like(cnt_ref)

    onehot = jnp.where(lax.broadcasted_iota(jnp.int32, (BUCKET_ROWS, TM), 0).astype(F32) == bucket, 1.0, 0.0)
    earlier = (lax.broadcasted_iota(jnp.int32, (TM, TM), 0) < lax.broadcasted_iota(jnp.int32, (TM, TM), 1))
    before = jnp.dot(onehot.astype(BF16), jnp.where(earlier, 1.0, 0.0).astype(BF16),
                     preferred_element_type=F32)
    cnt = cnt_ref[...]
    rank = jnp.sum(onehot * (before + cnt[:, 0:1]), axis=0, keepdims=True)
    cnt_ref[...] = cnt + jnp.sum(onehot, axis=1, keepdims=True)

    r_ref[0:1, :] = bucket
    r_ref[1:2, :] = w_lo
    r_ref[2:3, :] = w_hi
    r_ref[3:4, :] = rank
    r_ref[4:ROUTE_ROWS, :] = jnp.zeros((ROUTE_ROWS - 4, TM), F32)
    rec = jnp.concatenate([r_ref[...], jnp.zeros((LANES - ROUTE_ROWS, TM), F32)], axis=0).T
    h2p_ref[:, :d // 2] = _pack_bf16_pairs(h2)
    h2p_ref[:, d // 2:] = pltpu.bitcast(rec, jnp.int32)


def _oproj(a, wo, x, mod, ln_g, ln_b, wr2, wrh, rbias, *, mode_in, n_batch, tiles_lat, alpha):
    d = x.shape[1]
    compact = mode_in == "lat"
    n_tiles = n_batch * tiles_lat if compact else n_batch * (tiles_lat + 1)
    blk, modrow, _ = _tile_maps(mode_in, tiles_lat, n_batch)
    row_in = pl.BlockSpec((TM, d), lambda i: (blk(i), 0))
    row_out = pl.BlockSpec((TM, d), lambda i: (i, 0))
    vec = pl.BlockSpec((1, d), lambda i: (0, 0))

    def modspec(chunk):
        return pl.BlockSpec((1, 1, d), lambda i: (modrow(i), 0, chunk))

    body = functools.partial(_oproj_body, alpha=alpha)
    return pl.pallas_call(
        body,
        grid=(n_tiles,),
        in_specs=[row_out, pl.BlockSpec((d, d), lambda i: (0, 0)), row_in, modspec(2), vec, vec,
                  modspec(4), modspec(3),
                  pl.BlockSpec((2 * N_EXPERTS, d), lambda i: (0, 0)),
                  pl.BlockSpec((N_EXPERTS, d), lambda i: (0, 0)),
                  pl.BlockSpec((N_EXPERTS, 1), lambda i: (0, 0))],
        out_specs=[row_out, pl.BlockSpec((TM, d // 2 + LANES), lambda i: (i, 0)),
                   pl.BlockSpec((ROUTE_ROWS, TM), lambda i: (0, i)),
                   pl.BlockSpec((BUCKET_ROWS, LANES), lambda i: (0, 0))],
        out_shape=[jax.ShapeDtypeStruct((n_tiles * TM, d), F32),
                   jax.ShapeDtypeStruct((n_tiles * TM, d // 2 + LANES), jnp.int32),
                   jax.ShapeDtypeStruct((ROUTE_ROWS, n_tiles * TM), F32),
                   jax.ShapeDtypeStruct((BUCKET_ROWS, LANES), F32)],
        compiler_params=_cparams("arbitrary"),
        name="oproj_ln_route",
    )(a, wo, x, mod, ln_g.reshape(1, d), ln_b.reshape(1, d), mod, mod, wr2, wrh, rbias)


def _scatter_body(pos_ref, tail_ref, h_ref, xs_ref, stage, sem):
    i = pl.program_id(0)
    slot = i % 2

    @pl.when(i == 0)
    def _():
        stage[1] = jnp.zeros(stage.shape[1:], stage.dtype)

        def fill(row):
            return pltpu.make_async_copy(stage.at[1], xs_ref.at[pl.ds(pl.multiple_of(row, TM), TM)], sem.at[1])

        unused = [tail_ref[N_BUCKETS] + k * TM for k in range(N_BUCKETS)]
        for k in range(N_BUCKETS):
            fill(tail_ref[k]).start()
            pl.when(unused[k] < xs_ref.shape[0])(lambda k=k: fill(unused[k]).start())
        for k in range(N_BUCKETS):
            fill(tail_ref[k]).wait()
            pl.when(unused[k] < xs_ref.shape[0])(lambda k=k: fill(unused[k]).wait())

    @pl.when(i >= 2)
    def _():
        _row_dma_wait(xs_ref, stage.at[slot], sem.at[slot])

    stage[slot] = h_ref[...]

    for r in range(TM):
        pltpu.make_async_copy(stage.at[slot, pl.ds(r, 1)], xs_ref.at[pl.ds(pos_ref[0, 0, r], 1)],
                              sem.at[slot]).start(priority=r % 2)

    @pl.when(i == pl.num_programs(0) - 1)
    def _():
        @pl.when(i >= 1)
        def _():
            _row_dma_wait(xs_ref, stage.at[1 - slot], sem.at[1 - slot])
        _row_dma_wait(xs_ref, stage.at[slot], sem.at[slot])


def _scatter_rows(h2p, pos3, tail_rows, n_rows_sorted):
    n_tiles = pos3.shape[0]
    width = h2p.shape[1]
    return pl.pallas_call(
        _scatter_body,
        grid=(n_tiles,),
        in_specs=[pl.BlockSpec((1, 1, TM), lambda i: (i, 0, 0), memory_space=pltpu.SMEM),
                  pl.BlockSpec(memory_space=pltpu.SMEM),
                  pl.BlockSpec((TM, width), lambda i: (i, 0))],
        out_specs=pl.BlockSpec(memory_space=pl.ANY),
        out_shape=jax.ShapeDtypeStruct((n_rows_sorted, width), jnp.int32),
        scratch_shapes=[pltpu.VMEM((2, TM, width), jnp.int32), pltpu.SemaphoreType.DMA((2,))],
        compiler_params=_cparams("arbitrary"),
        name="moe_scatter",
    )(pos3, tail_rows, h2p)


def _moe_body(ea_ref, eb_ref, act_ref, new_ref, par_ref, hasn_ref, nea_ref, neb_ref, xblk_ref,
              x_ref, wg_hbm, wu_hbm, wd_hbm, y_ref, wg_f, wu_f, wd_f, wgu_s, wd_s, wsem, *, d_exp, layer):
    i = pl.program_id(0)
    half = wd_s.shape[2] // 2

    def weight_copies(slot, experts):
        return [pltpu.make_async_copy(hbm.at[layer, ex], buf.at[slot, e], wsem.at[slot])
                for e, ex in enumerate(experts) for hbm, buf in ((wg_hbm, wg_f), (wu_hbm, wu_f), (wd_hbm, wd_f))]

    @pl.when(i == 0)
    def _():
        for cp in weight_copies(par_ref[0], (ea_ref[0], eb_ref[0])):
            cp.start()

    @pl.when(new_ref[i] > 0)
    def _():
        slot = par_ref[i]
        for cp in weight_copies(slot, (ea_ref[i], eb_ref[i])):
            cp.wait()

        @pl.when(hasn_ref[i] > 0)
        def _():
            for cp in weight_copies(1 - slot, (nea_ref[i], neb_ref[i])):
                cp.start()

        for e in range(2):
            wgu_s[e, :, :d_exp] = wg_f[slot, e].astype(BF16)
            wgu_s[e, :, d_exp:] = wu_f[slot, e].astype(BF16)
            wd_s[e] = wd_f[slot, e].astype(BF16)

    @pl.when(act_ref[i] > 0)
    def _():
        xw = x_ref[...]
        x = _unpack_bf16_pairs(xw[:, :half]).astype(BF16)
        rec = pltpu.bitcast(xw[:, half:], F32)

        def expert(e):
            gu = jnp.dot(x, wgu_s[e], preferred_element_type=F32)
            gate = gu[:, :d_exp]
            h = gate * jax.nn.sigmoid(gate) * gu[:, d_exp:] * rec[:, 1 + e:2 + e]
            return jnp.dot(h.astype(BF16), wd_s[e], preferred_element_type=F32)

        y_ref[...] = _pack_bf16_pairs(expert(0) + expert(1))

    @pl.when(act_ref[i] == 0)
    def _():
        y_ref[...] = jnp.zeros_like(y_ref)


def _moe(xs, w_gate, w_up, w_down, layer, sched):
    n_rows, width = xs.shape
    d_exp, d = w_down.shape[2:]
    n_tiles = n_rows // TM
    body = functools.partial(_moe_body, d_exp=d_exp, layer=layer)
    grid_spec = pltpu.PrefetchScalarGridSpec(
        num_scalar_prefetch=len(sched),
        grid=(n_tiles,),
        in_specs=[pl.BlockSpec((TM, width), lambda i, *s: (s[-1][i], 0)),
                  pl.BlockSpec(memory_space=pl.ANY), pl.BlockSpec(memory_space=pl.ANY),
                  pl.BlockSpec(memory_space=pl.ANY)],
        out_specs=pl.BlockSpec((TM, d // 2), lambda i, *_: (i, 0)),
        scratch_shapes=[pltpu.VMEM((2, 2, d, d_exp), F32), pltpu.VMEM((2, 2, d, d_exp), F32),
                        pltpu.VMEM((2, 2, d_exp, d), F32),
                        pltpu.VMEM((2, d, 2 * d_exp), BF16), pltpu.VMEM((2, d_exp, d), BF16),
                        pltpu.SemaphoreType.DMA((2,))],
    )
    return pl.pallas_call(
        body, grid_spec=grid_spec,
        out_shape=jax.ShapeDtypeStruct((n_rows, d // 2), jnp.int32),
        compiler_params=_cparams("arbitrary"),
        name="moe_pairs",
    )(*sched, xs, w_gate, w_up, w_down)


def _dispatch(route, counts, n_tiles):
    n = route.shape[1]
    bucket = route[0].astype(jnp.int32)
    rank = route[3].astype(jnp.int32)
    counts = counts[:N_BUCKETS, 0].astype(jnp.int32)
    padded = ((counts + TM - 1) // TM) * TM
    ends = jnp.cumsum(padded)
    off = ends - padded

    def lookup(idx, table):
        return jnp.sum(jnp.where(idx[:, None] == jnp.arange(table.shape[0])[None, :], table[None, :], 0), axis=1)

    pos = lookup(bucket, off) + rank
    tile_start = jnp.arange(n_tiles, dtype=jnp.int32) * TM
    active = tile_start < ends[-1]
    last_start = jnp.maximum(ends[-1] - TM, 0)
    tb = jnp.sum((jnp.where(active, tile_start, last_start)[:, None] >= ends[None, :]).astype(jnp.int32), axis=1)
    tb = jnp.minimum(tb, N_BUCKETS - 1)
    ea = (tb // len(PAIRS)) * GROUP_SIZE + lookup(tb % len(PAIRS), jnp.array([p[0] for p in PAIRS], jnp.int32))
    eb = (tb // len(PAIRS)) * GROUP_SIZE + lookup(tb % len(PAIRS), jnp.array([p[1] for p in PAIRS], jnp.int32))
    tiles = jnp.arange(n_tiles, dtype=jnp.int32)
    new = jnp.concatenate([jnp.ones((1,), jnp.int32), (tb[1:] != tb[:-1]).astype(jnp.int32)])
    first_at = jnp.where(new > 0, tiles, n_tiles)
    nxt = jnp.concatenate([lax.cummin(first_at, reverse=True)[1:], jnp.full((1,), n_tiles, jnp.int32)])
    has_next = (nxt < n_tiles).astype(jnp.int32)
    nxt = jnp.minimum(nxt, n_tiles - 1)
    parity = (jnp.cumsum(new) - 1) % 2
    last_used = jnp.maximum(ends[-1] // TM - 1, 0)
    sched = (ea, eb, active.astype(jnp.int32), new, parity.astype(jnp.int32), has_next,
             lookup(nxt, ea), lookup(nxt, eb), jnp.minimum(tiles, last_used))
    tail_rows = jnp.concatenate([jnp.where(padded > 0, ends - TM, 0), ends[-1:]]).astype(jnp.int32)
    return pos.reshape(n // TM, 1, TM), tail_rows, sched


def _rope_tables(seq, ctx_len):
    t = jnp.arange(seq)
    quarter = HEAD_DIM // 4
    inv_freq = ROPE_THETA ** (-jnp.arange(quarter, dtype=F32) / quarter)
    ang_r = (t // GRID_W).astype(F32)[:, None] * inv_freq
    ang_c = (t % GRID_W).astype(F32)[:, None] * inv_freq
    cos_h = jnp.concatenate([jnp.cos(ang_r)] * 2 + [jnp.cos(ang_c)] * 2, axis=-1)
    sin_h = jnp.concatenate([-jnp.sin(ang_r), jnp.sin(ang_r), -jnp.sin(ang_c), jnp.sin(ang_c)], axis=-1)
    cos_t = jnp.concatenate([cos_h, cos_h], axis=-1)
    sin_t = jnp.concatenate([sin_h, sin_h], axis=-1)
    cos_t = jnp.concatenate([cos_t, jnp.ones((ctx_len, LANES), F32)], axis=0)
    sin_t = jnp.concatenate([sin_t, jnp.zeros((ctx_len, LANES), F32)], axis=0)
    return cos_t, sin_t


def kernel(x, c, ctx, c_ctx, w_ada, b_ada, wqkv_a, wo_a, sink_a, wqkv_b, wo_b, lambda_b, subln_b, ln_attn_g, ln_attn_b, ln_ffn_g, ln_ffn_b, w_router, router_bias, w_gate, w_up, w_down):
    n_batch, seq, d = x.shape
    ctx_len = ctx.shape[1]
    depth = w_ada.shape[0]
    assert seq % TM == 0 and ctx_len == TM and seq >= TM + 2 * WINDOW
    assert n_batch + 1 <= ADA_ROWS and d == A_Q_HEADS * HEAD_DIM
    tiles_lat = seq // TM
    alpha = (2 * depth) ** 0.25
    q_cols = A_Q_HEADS * HEAD_DIM
    kv_cols = A_KV_HEADS * HEAD_DIM

    cc = jnp.concatenate([c, c_ctx[None, :], jnp.zeros((ADA_ROWS - n_batch - 1, d), F32)], axis=0)
    mods = _ada_modulation(cc, w_ada, b_ada).reshape(depth, ADA_ROWS, 1, 6 * d)
    cos_t, sin_t = _rope_tables(seq, ctx_len)

    xs = jnp.concatenate([x, ctx], axis=1).reshape(n_batch * (seq + ctx_len), d)
    wr_hi = w_router.astype(BF16)
    wr_lo = (w_router - wr_hi.astype(F32)).astype(BF16)
    wr2 = jnp.concatenate([wr_hi.T, wr_lo.T], axis=0)
    wrh = wr_hi.T
    rbias = router_bias.reshape(N_EXPERTS, 1)

    y = None
    mode = "all"
    for i in range(depth):
        last = i == depth - 1
        j = i // 2
        if i % 2 == 0:
            w = wqkv_a[j].astype(BF16)
            n_rope, dup_from = q_cols + kv_cols, q_cols
        else:
            w = wqkv_b[j].astype(BF16)
            n_rope, dup_from = 2 * q_cols, w.shape[1]
        outs = _lnqkv(xs, y, mods[i - 1] if i else None, ln_ffn_g[i - 1] if i else None,
                      ln_ffn_b[i - 1] if i else None, mods[i], w, cos_t, sin_t,
                      mode_in="all", n_batch=n_batch, tiles_lat=tiles_lat,
                      alpha=alpha, n_rope=n_rope, dup_from=dup_from)
        if i:
            xs, qkv = outs
        else:
            (qkv,) = outs
        if i % 2 == 0:
            att = _attn_a(qkv, sink_a[j], n_batch=n_batch, seq=seq, ctx_len=ctx_len, with_ctx=not last)
            wo = wo_a[j].astype(BF16)
        else:
            lam_init = 0.8 - 0.6 * math.exp(-0.3 * i)
            att = _attn_b(qkv, lambda_b[j], subln_b[j], n_batch=n_batch, seq=seq, ctx_len=ctx_len,
                          with_ctx=not last, lam_init=lam_init)
            wo = wo_b[j].astype(BF16)
        mode = "lat" if last else "all"
        xs, h2p, route, counts = _oproj(att, wo, xs, mods[i], ln_attn_g[i], ln_attn_b[i], wr2, wrh, rbias,
                                        mode_in=mode, n_batch=n_batch, tiles_lat=tiles_lat, alpha=alpha)
        n_tiles = h2p.shape[0] // TM + N_BUCKETS
        pos3, tail_rows, sched = _dispatch(route, counts, n_tiles)
        x_sorted = _scatter_rows(h2p, pos3, tail_rows, n_tiles * TM)
        y = (_moe(x_sorted, w_gate, w_up, w_down, i, sched), pos3)

    (out,) = _lnqkv(xs, y, mods[depth - 1], ln_ffn_g[depth - 1], ln_ffn_b[depth - 1], None, None, None, None,
                    mode_in="compact", n_batch=n_batch, tiles_lat=tiles_lat, alpha=alpha)
    return out.reshape(n_batch, seq, d)
```

```python
import functools
import math

import jax
import jax.numpy as jnp
from jax import lax
from jax.experimental import pallas as pl
from jax.experimental.pallas import tpu as pltpu

F32 = jnp.float32
BF16 = jnp.bfloat16

HEAD_DIM = 64
A_Q_HEADS = 16
A_KV_HEADS = 4
A_GROUP = A_Q_HEADS // A_KV_HEADS
GRID_W = 64
WINDOW = 128
N_EXPERTS = 16
N_GROUPS = 4
GROUP_SIZE = N_EXPERTS // N_GROUPS
ROPE_THETA = 10000.0
LN_EPS = 1e-6
SUBLN_EPS = 1e-5
NEG_INF = -1e30
ATTN_SCALE = HEAD_DIM ** -0.5
LOG2E = math.log2(math.e)
VT_ROWS = 144

LANES = 128
TM = 256
DIFF_HEADS_PER_STEP = 4
SCATTER_TILES = 2
ADA_ROWS = 24
VMEM_LIMIT = 56 * 1024 * 1024

PAIRS = ((0, 1), (0, 2), (0, 3), (1, 2), (1, 3), (2, 3))
N_BUCKETS = N_GROUPS * len(PAIRS)

BUCKET_ROWS = 32
ROUTE_ROWS = 8
HIGH16 = -65536

NT_DIMS = (((1,), (1,)), ((), ()))


def _cparams(*sem):
    return pltpu.CompilerParams(dimension_semantics=sem, vmem_limit_bytes=VMEM_LIMIT)


def _ada_body(c_ref, w_ref, b_ref, o_ref):
    c = c_ref[...]
    sc = c * jax.nn.sigmoid(c)
    o_ref[0] = jnp.dot(sc, w_ref[0], precision=lax.Precision.HIGHEST,
                       preferred_element_type=F32) + b_ref[0]


def _ada_modulation(cc, w_ada, b_ada):
    depth, d, d6 = w_ada.shape
    tn = 1536
    return pl.pallas_call(
        _ada_body,
        grid=(depth, d6 // tn),
        in_specs=[
            pl.BlockSpec((ADA_ROWS, d), lambda l, n: (0, 0)),
            pl.BlockSpec((1, d, tn), lambda l, n: (l, 0, n)),
            pl.BlockSpec((1, 1, tn), lambda l, n: (l, 0, n)),
        ],
        out_specs=pl.BlockSpec((1, ADA_ROWS, tn), lambda l, n: (l, 0, n)),
        out_shape=jax.ShapeDtypeStruct((depth, ADA_ROWS, d6), F32),
        compiler_params=_cparams("arbitrary", "arbitrary"),
        name="ada_mod",
    )(cc, w_ada, b_ada.reshape(depth, 1, d6))


def _tile_maps(mode, tiles_lat, n_batch):
    tpb = tiles_lat + 1
    if mode == "all":
        blk = lambda i: i
        modrow = lambda i: jnp.where(i % tpb == tiles_lat, n_batch, i // tpb)
        pos = lambda i: i % tpb
    elif mode == "lat":
        blk = lambda i: (i // tiles_lat) * tpb + i % tiles_lat
        modrow = lambda i: i // tiles_lat
        pos = lambda i: i % tiles_lat
    else:
        blk = lambda i: i
        modrow = lambda i: i // tiles_lat
        pos = lambda i: i % tiles_lat
    return blk, modrow, pos


def _mod_chunk(mod_ref, k):
    d = mod_ref.shape[2] // 6
    return mod_ref[0, :, k * d:(k + 1) * d]


def _layer_norm(z, g, b):
    mu = jnp.mean(z, axis=-1, keepdims=True)
    zc = z - mu
    var = jnp.mean(zc * zc, axis=-1, keepdims=True)
    return zc * lax.rsqrt(var + LN_EPS) * g + b


def _pack_bf16_pairs(v):
    half = v.shape[1] // 2
    bits = pltpu.bitcast(v.astype(BF16).astype(F32), jnp.int32)
    return lax.shift_right_logical(bits[:, :half], 16) | (bits[:, half:] & HIGH16)


def _unpack_bf16_pairs(w):
    return jnp.concatenate([pltpu.bitcast(lax.shift_left(w, 16), F32),
                            pltpu.bitcast(w & HIGH16, F32)], axis=1)


def _row_dma_wait(hbm_ref, buf_ref, sem):
    pltpu.make_async_copy(hbm_ref.at[pl.ds(0, buf_ref.shape[0])], buf_ref, sem).wait()


def _lnqkv_body(*refs, has_ln, has_qkv, alpha, n_rope, dup_from, n_out):
    refs = list(refs)
    x_ref = refs.pop(0)
    if has_ln:
        pos_ref, posn_ref, ys_ref, g2_ref, lng_ref, lnb_ref = refs[:6]
        refs = refs[6:]
        ybuf, ysem = refs[-2:]
        refs = refs[:-2]
    if has_qkv:
        mod_ref, w_ref, cos_ref, sin_ref = refs[:4]
        refs = refs[4:]
    x = x_ref[...]
    if has_ln:
        xo_ref = refs.pop(0)
        i = pl.program_id(0)
        slot = i % 2

        def fetch_row(p_ref, s, r, priority=0):
            pltpu.make_async_copy(ys_ref.at[pl.ds(p_ref[0, 0, r], 1)], ybuf.at[s, pl.ds(r, 1)],
                                  ysem.at[s]).start(priority=priority)

        @pl.when(i == 0)
        def _():
            lax.fori_loop(0, TM, lambda r, c: (fetch_row(pos_ref, 0, r), c)[1], 0, unroll=8)

        _row_dma_wait(ys_ref, ybuf.at[slot], ysem.at[slot])
        for r in range(TM):
            fetch_row(posn_ref, 1 - slot, r, priority=r % 2)
        z = alpha * x + _mod_chunk(g2_ref, 5) * _unpack_bf16_pairs(ybuf[slot])
        x = _layer_norm(z, lng_ref[...], lnb_ref[...])
        xo_ref[...] = x

    def finish():
        if has_ln:
            @pl.when(pl.program_id(0) == pl.num_programs(0) - 1)
            def _():
                _row_dma_wait(ys_ref, ybuf.at[1 - slot], ysem.at[1 - slot])

    if not has_qkv:
        finish()
        return
    qkv_ref = refs.pop(0)
    h = (x * (1.0 + _mod_chunk(mod_ref, 1)) + _mod_chunk(mod_ref, 0)).astype(BF16)
    r = jnp.dot(h, w_ref[...], preferred_element_type=F32)
    cos = cos_ref[...]
    sin = sin_ref[...]
    lane = lax.broadcasted_iota(jnp.int32, (TM, LANES), 1)
    first16 = (lane % 32) < 16
    low_half = lane < HEAD_DIM
    q_cols = A_Q_HEADS * HEAD_DIM
    dst = 0
    for c in range(n_out // LANES):
        seg = r[:, c * LANES:(c + 1) * LANES]
        col = c * LANES
        if col < q_cols:
            seg = seg * (ATTN_SCALE * LOG2E)
        if col < n_rope:
            rot = jnp.where(first16, pltpu.roll(seg, LANES - 16, 1), pltpu.roll(seg, 16, 1))
            seg = seg * cos + rot * sin
        if col >= dup_from:
            swapped = pltpu.roll(seg, HEAD_DIM, 1)
            qkv_ref[:, dst:dst + LANES] = jnp.where(low_half, seg, swapped).astype(BF16)
            qkv_ref[:, dst + LANES:dst + 2 * LANES] = jnp.where(low_half, swapped, seg).astype(BF16)
            dst += 2 * LANES
        else:
            qkv_ref[:, dst:dst + LANES] = seg.astype(BF16)
            dst += LANES
    finish()


def _lnqkv(x, y, mod_ln, ln_g, ln_b, mod_qkv, w, cos_t, sin_t, *, mode_in, n_batch, tiles_lat,
           alpha, n_rope=0, dup_from=0):
    d = x.shape[1]
    has_ln = y is not None
    has_qkv = w is not None
    n_tiles = (n_batch * (tiles_lat + 1)) if mode_in == "all" else n_batch * tiles_lat
    blk, modrow, pos = _tile_maps(mode_in, tiles_lat, n_batch)
    row = pl.BlockSpec((TM, d), lambda i: (blk(i), 0))
    vec = pl.BlockSpec((1, d), lambda i: (0, 0))

    modspec = pl.BlockSpec((1, 1, 6 * d), lambda i: (modrow(i), 0, 0))

    args, in_specs, out_specs, out_shape, scratch = [x], [row], [], [], []
    n_out = 0
    if has_ln:
        ys, pos3 = y
        args += [pos3, pos3, ys, mod_ln, ln_g.reshape(1, d), ln_b.reshape(1, d)]
        in_specs += [pl.BlockSpec((1, 1, TM), lambda i: (i, 0, 0), memory_space=pltpu.SMEM),
                     pl.BlockSpec((1, 1, TM), lambda i: (jnp.minimum(i + 1, n_tiles - 1), 0, 0),
                                  memory_space=pltpu.SMEM),
                     pl.BlockSpec(memory_space=pl.ANY), modspec, vec, vec]
        out_specs.append(row)
        out_shape.append(jax.ShapeDtypeStruct(x.shape, F32))
        scratch = [pltpu.VMEM((2, TM) + ys.shape[1:], jnp.int32), pltpu.SemaphoreType.DMA((2,))]
    if has_qkv:
        n_out = w.shape[1]
        n_store = n_out + (n_out - dup_from)
        args += [mod_qkv, w, cos_t, sin_t]
        in_specs += [modspec,
                     pl.BlockSpec((d, n_out), lambda i: (0, 0)),
                     pl.BlockSpec((TM, LANES), lambda i: (pos(i), 0)),
                     pl.BlockSpec((TM, LANES), lambda i: (pos(i), 0))]
        out_specs.append(pl.BlockSpec((TM, n_store), lambda i: (blk(i), 0)))
        out_shape.append(jax.ShapeDtypeStruct((x.shape[0], n_store), BF16))
    body = functools.partial(_lnqkv_body, has_ln=has_ln, has_qkv=has_qkv, alpha=alpha,
                             n_rope=n_rope, dup_from=dup_from, n_out=n_out)
    return pl.pallas_call(
        body, grid=(n_tiles,), in_specs=in_specs, out_specs=out_specs, out_shape=out_shape,
        scratch_shapes=scratch, compiler_params=_cparams("arbitrary"),
        name=("ln_" if has_ln else "") + ("qkv" if has_qkv else "out"),
    )(*args)


def _softmax_parts(scores, biases, sink):
    scores = [s if bias is None else s + bias[...] for s, bias in zip(scores, biases)]
    m = None
    for s in scores:
        sm = jnp.max(s, axis=-1, keepdims=True)
        m = sm if m is None else jnp.maximum(m, sm)
    if sink is not None:
        m = jnp.maximum(m, sink)
    probs = []
    denom = None
    for s in scores:
        e = jnp.exp2(s - m)
        es = jnp.sum(e, axis=-1, keepdims=True)
        denom = es if denom is None else denom + es
        probs.append(e.astype(BF16))
    if sink is not None:
        denom = denom + jnp.exp2(sink - m)
    return probs, 1.0 / denom


def _attn_a_body(sink_ref, q_ref, k_ref, v_ref, o_ref, bias_ref, *, seq, tiles_lat):
    qi = pl.program_id(1)
    lane = lax.broadcasted_iota(jnp.int32, (TM, LANES), 1)
    low_half = lane < HEAD_DIM
    span = TM + 2 * WINDOW
    rows = lax.broadcasted_iota(jnp.int32, (A_GROUP * TM, 1), 0)

    def run(parts):
        all_scores = []
        for h in range(A_KV_HEADS):
            qs = []
            for g in range(A_GROUP):
                q2 = q_ref[:, (h * A_GROUP + g) // 2 * LANES:((h * A_GROUP + g) // 2 + 1) * LANES]
                keep = low_half if g % 2 == 0 else jnp.logical_not(low_half)
                qs.append(jnp.where(keep, q2, jnp.zeros_like(q2)))
            qh = jnp.concatenate(qs, axis=0)
            scores = []
            for (r0, nr, _) in parts:
                kk = k_ref[pl.ds(r0, nr), h * LANES:(h + 1) * LANES]
                scores.append(lax.dot_general(qh, kk, NT_DIMS, preferred_element_type=F32))
            all_scores.append(scores)
        for h in range(A_KV_HEADS):
            sink = jnp.zeros((A_GROUP * TM, 1), F32)
            for g in range(A_GROUP):
                sink = jnp.where(rows // TM == g, sink_ref[h * A_GROUP + g] * LOG2E, sink)
            probs, inv = _softmax_parts(all_scores[h], [p[2] for p in parts], sink)
            acc = None
            for p, (r0, nr, _) in zip(probs, parts):
                vv = v_ref[pl.ds(r0, nr), h * LANES:(h + 1) * LANES]
                o = jnp.dot(p, vv, preferred_element_type=F32)
                acc = o if acc is None else acc + o
            acc = acc * inv
            for gp in range(A_GROUP // 2):
                grp = h * (A_GROUP // 2) + gp
                o_ref[:, grp * LANES:(grp + 1) * LANES] = jnp.where(
                    low_half, acc[2 * gp * TM:(2 * gp + 1) * TM], acc[(2 * gp + 1) * TM:(2 * gp + 2) * TM]
                ).astype(BF16)

    @pl.when(qi < tiles_lat)
    def _():
        start = pl.multiple_of(jnp.clip(qi * TM - WINDOW, 0, seq - span), WINDOW)
        qpos = qi * TM + lax.broadcasted_iota(jnp.int32, (TM, span), 0)
        kpos = start + lax.broadcasted_iota(jnp.int32, (TM, span), 1)
        bias = jnp.where(jnp.abs(kpos - qpos) <= WINDOW, 0.0, NEG_INF)
        for g in range(A_GROUP):
            bias_ref[g * TM:(g + 1) * TM, :] = bias
        run([(start, span, bias_ref), (seq, k_ref.shape[0] - seq, None)])

    @pl.when(qi >= tiles_lat)
    def _():
        run([(seq, k_ref.shape[0] - seq, None)])


def _attn_a(qkv, sink, *, n_batch, seq, ctx_len, with_ctx):
    tiles_lat = seq // TM
    tpb = tiles_lat + ctx_len // TM
    rows_b = seq + ctx_len
    q_cols = A_Q_HEADS * HEAD_DIM
    kv_cols = 2 * A_KV_HEADS * HEAD_DIM
    nq = tpb if with_ctx else tiles_lat
    body = functools.partial(_attn_a_body, seq=seq, tiles_lat=tiles_lat)
    return pl.pallas_call(
        body,
        grid=(n_batch, nq),
        in_specs=[
            pl.BlockSpec(memory_space=pltpu.SMEM),
            pl.BlockSpec((TM, q_cols), lambda b, i: (b * tpb + i, 0)),
            pl.BlockSpec((rows_b, kv_cols), lambda b, i: (b, q_cols // kv_cols)),
            pl.BlockSpec((rows_b, kv_cols), lambda b, i: (b, q_cols // kv_cols + 1)),
        ],
        out_specs=pl.BlockSpec((TM, q_cols), lambda b, i: (b * nq + i, 0)),
        out_shape=jax.ShapeDtypeStruct((n_batch * nq * TM, q_cols), BF16),
        scratch_shapes=[pltpu.VMEM((A_GROUP * TM, TM + 2 * WINDOW), F32)],
        compiler_params=_cparams("arbitrary", "arbitrary"),
        name="attn_window",
    )(sink, qkv, qkv, qkv)


def _attn_b_body(lam_ref, g_ref, q_ref, k_ref, v_ref, o_ref, vt_ref, *, seq, tiles_lat, lam_init, heads):
    qi = pl.program_id(2)
    lp = lam_ref[...]
    lam = (jnp.exp(jnp.sum(lp[0:1] * lp[1:2], axis=-1, keepdims=True))
           - jnp.exp(jnp.sum(lp[2:3] * lp[3:4], axis=-1, keepdims=True)) + lam_init)
    lane = lax.broadcasted_iota(jnp.int32, (TM, LANES), 1)
    low_half = lane < HEAD_DIM
    n_keys = k_ref.shape[0]

    @pl.when(qi == 0)
    def _():
        for hh in range(heads):
            vt_ref[hh, :LANES, :] = v_ref[:, hh * LANES:(hh + 1) * LANES].astype(F32).T.astype(BF16)
            row = lax.broadcasted_iota(jnp.int32, (VT_ROWS - LANES, n_keys), 0)
            vt_ref[hh, LANES:, :] = jnp.where(row == 0, 1.0, 0.0).astype(BF16)

    def run(r0, nr):
        scores = []
        for hh in range(heads):
            cols = slice(hh * LANES, (hh + 1) * LANES)
            q = q_ref[:, cols]
            kk = k_ref[pl.ds(r0, nr), cols]
            for qm in (jnp.where(low_half, q, jnp.zeros_like(q)), jnp.where(low_half, jnp.zeros_like(q), q)):
                scores.append(lax.dot_general(kk, qm, NT_DIMS, preferred_element_type=F32))
        for hh in range(heads):
            vt = vt_ref[hh, :, pl.ds(r0, nr)]
            outs = []
            for s in scores[2 * hh:2 * hh + 2]:
                e = jnp.exp2(s - jnp.max(s, axis=0, keepdims=True)).astype(BF16)
                ov = jnp.dot(vt, e, preferred_element_type=F32)
                outs.append((ov[:LANES], 1.0 / ov[LANES:LANES + 1]))
            o = outs[0][0] * outs[0][1] - outs[1][0] * (lam * outs[1][1])
            ms = jnp.mean(o * o, axis=0, keepdims=True)
            o = o * lax.rsqrt(ms + SUBLN_EPS) * g_ref[...] * (1.0 - lam_init)
            o_ref[:, hh * LANES:(hh + 1) * LANES] = o.T.astype(BF16)

    @pl.when(qi < tiles_lat)
    def _():
        run(0, k_ref.shape[0])

    @pl.when(qi >= tiles_lat)
    def _():
        run(seq, k_ref.shape[0] - seq)


def _attn_b(qkv, lam_params, subln_g, *, n_batch, seq, ctx_len, with_ctx, lam_init):
    tiles_lat = seq // TM
    tpb = tiles_lat + ctx_len // TM
    rows_b = seq + ctx_len
    n_heads = qkv.shape[1] // (3 * LANES)
    n_hgrp = n_heads // DIFF_HEADS_PER_STEP
    width = DIFF_HEADS_PER_STEP * LANES
    nq = tpb if with_ctx else tiles_lat
    body = functools.partial(_attn_b_body, seq=seq, tiles_lat=tiles_lat, lam_init=lam_init,
                             heads=DIFF_HEADS_PER_STEP)
    return pl.pallas_call(
        body,
        grid=(n_batch, n_hgrp, nq),
        in_specs=[
            pl.BlockSpec((4, HEAD_DIM), lambda b, h, i: (0, 0)),
            pl.BlockSpec((LANES, 1), lambda b, h, i: (0, 0)),
            pl.BlockSpec((TM, width), lambda b, h, i: (b * tpb + i, h)),
            pl.BlockSpec((rows_b, width), lambda b, h, i: (b, n_hgrp + h)),
            pl.BlockSpec((rows_b, width), lambda b, h, i: (b, 2 * n_hgrp + h)),
        ],
        out_specs=pl.BlockSpec((TM, width), lambda b, h, i: (b * nq + i, h)),
        out_shape=jax.ShapeDtypeStruct((n_batch * nq * TM, n_heads * LANES), BF16),
        compiler_params=_cparams("arbitrary", "arbitrary", "arbitrary"),
        scratch_shapes=[pltpu.VMEM((DIFF_HEADS_PER_STEP, VT_ROWS, rows_b), BF16)],
        name="attn_diff",
    )(lam_params, subln_g.reshape(LANES, 1), qkv, qkv, qkv)


def _route_rows(logits, bias):
    s = jax.nn.sigmoid(logits)
    biased = s + bias
    b = [biased[e:e + 1, :] for e in range(N_EXPERTS)]
    u = [s[e:e + 1, :] for e in range(N_EXPERTS)]
    gscore = []
    for g in range(N_GROUPS):
        best = None
        for (i, j) in PAIRS:
            t = b[GROUP_SIZE * g + i] + b[GROUP_SIZE * g + j]
            best = t if best is None else jnp.maximum(best, t)
        gscore.append(best)
    gsel = jnp.zeros(gscore[0].shape, jnp.int32)
    gbest = gscore[0]
    for g in range(1, N_GROUPS):
        better = gscore[g] > gbest
        gsel = jnp.where(better, g, gsel)
        gbest = jnp.where(better, gscore[g], gbest)

    def pick(rows, k):
        out = rows[GROUP_SIZE * (N_GROUPS - 1) + k]
        for g in range(N_GROUPS - 2, -1, -1):
            out = jnp.where(gsel == g, rows[GROUP_SIZE * g + k], out)
        return out

    v = [pick(b, k) for k in range(GROUP_SIZE)]
    w = [pick(u, k) for k in range(GROUP_SIZE)]
    sel = []
    for k in range(GROUP_SIZE):
        cnt = jnp.zeros(gsel.shape, jnp.int32)
        for j in range(GROUP_SIZE):
            if j == k:
                continue
            beats = (v[j] >= v[k]) if j < k else (v[j] > v[k])
            cnt = cnt + jnp.where(beats, 1, 0)
        sel.append(cnt < 2)
    pidx = jnp.zeros(gsel.shape, jnp.int32)
    u_lo = jnp.zeros(gbest.shape, F32)
    u_hi = jnp.zeros(gbest.shape, F32)
    for idx, (i, j) in enumerate(PAIRS):
        both = jnp.logical_and(sel[i], sel[j])
        pidx = jnp.where(both, idx, pidx)
        u_lo = jnp.where(both, w[i], u_lo)
        u_hi = jnp.where(both, w[j], u_hi)
    tot = u_lo + u_hi
    bucket = (gsel * len(PAIRS) + pidx).astype(F32)
    return bucket, u_lo / tot, u_hi / tot


def _oproj_body(a_ref, wo_ref, x_ref, mod_ref, lng_ref, lnb_ref, wr2_ref, wrh_ref, rb_ref,
                xo_ref, h2p_ref, r_ref, cnt_ref, *, alpha):
    i = pl.program_id(0)
    d = x_ref.shape[1]
    al = jnp.dot(a_ref[...], wo_ref[...], preferred_element_type=F32)
    z = alpha * x_ref[...] + _mod_chunk(mod_ref, 2) * al
    xn = _layer_norm(z, lng_ref[...], lnb_ref[...])
    xo_ref[...] = xn
    h2 = xn * (1.0 + _mod_chunk(mod_ref, 4)) + _mod_chunk(mod_ref, 3)
    hi = h2.astype(BF16)
    lo = (h2 - hi.astype(F32)).astype(BF16)
    l2 = lax.dot_general(wr2_ref[...], hi, NT_DIMS, preferred_element_type=F32)
    l1 = lax.dot_general(wrh_ref[...], lo, NT_DIMS, preferred_element_type=F32)
    logits = l2[:N_EXPERTS] + l2[N_EXPERTS:] + l1
    bucket, w_lo, w_hi = _route_rows(logits, rb_ref[...])

    @pl.when(i == 0)
    def _():
        cnt_ref[...] = jnp.zeros_like(cnt_ref)

    onehot = jnp.where(lax.broadcasted_iota(jnp.int32, (BUCKET_ROWS, TM), 0).astype(F32) == bucket, 1.0, 0.0)
    earlier = (lax.broadcasted_iota(jnp.int32, (TM, TM), 0) < lax.broadcasted_iota(jnp.int32, (TM, TM), 1))
    before = jnp.dot(onehot.astype(BF16), jnp.where(earlier, 1.0, 0.0).astype(BF16),
                     preferred_element_type=F32)
    cnt = cnt_ref[...]
    rank = jnp.sum(onehot * (before + cnt[:, 0:1]), axis=0, keepdims=True)
    cnt_ref[...] = cnt + jnp.sum(onehot, axis=1, keepdims=True)

    r_ref[0:1, :] = bucket
    r_ref[1:2, :] = w_lo
    r_ref[2:3, :] = w_hi
    r_ref[3:4, :] = rank
    r_ref[4:ROUTE_ROWS, :] = jnp.zeros((ROUTE_ROWS - 4, TM), F32)
    rec = jnp.concatenate([r_ref[...], jnp.zeros((LANES - ROUTE_ROWS, TM), F32)], axis=0).T
    h2p_ref[:, :d // 2] = _pack_bf16_pairs(h2)
    h2p_ref[:, d // 2:] = pltpu.bitcast(rec, jnp.int32)


def _oproj(a, wo, x, mod, ln_g, ln_b, wr2, wrh, rbias, *, mode_in, n_batch, tiles_lat, alpha):
    d = x.shape[1]
    compact = mode_in == "lat"
    n_tiles = n_batch * tiles_lat if compact else n_batch * (tiles_lat + 1)
    blk, modrow, _ = _tile_maps(mode_in, tiles_lat, n_batch)
    row_in = pl.BlockSpec((TM, d), lambda i: (blk(i), 0))
    row_out = pl.BlockSpec((TM, d), lambda i: (i, 0))
    vec = pl.BlockSpec((1, d), lambda i: (0, 0))

    modspec = pl.BlockSpec((1, 1, 6 * d), lambda i: (modrow(i), 0, 0))

    body = functools.partial(_oproj_body, alpha=alpha)
    return pl.pallas_call(
        body,
        grid=(n_tiles,),
        in_specs=[row_out, pl.BlockSpec((d, d), lambda i: (0, 0)), row_in, modspec, vec, vec,
                  pl.BlockSpec((2 * N_EXPERTS, d), lambda i: (0, 0)),
                  pl.BlockSpec((N_EXPERTS, d), lambda i: (0, 0)),
                  pl.BlockSpec((N_EXPERTS, 1), lambda i: (0, 0))],
        out_specs=[row_out, pl.BlockSpec((TM, d // 2 + LANES), lambda i: (i, 0)),
                   pl.BlockSpec((ROUTE_ROWS, TM), lambda i: (0, i)),
                   pl.BlockSpec((BUCKET_ROWS, LANES), lambda i: (0, 0))],
        out_shape=[jax.ShapeDtypeStruct((n_tiles * TM, d), F32),
                   jax.ShapeDtypeStruct((n_tiles * TM, d // 2 + LANES), jnp.int32),
                   jax.ShapeDtypeStruct((ROUTE_ROWS, n_tiles * TM), F32),
                   jax.ShapeDtypeStruct((BUCKET_ROWS, LANES), F32)],
        compiler_params=_cparams("arbitrary"),
        name="oproj_ln_route",
    )(a, wo, x, mod, ln_g.reshape(1, d), ln_b.reshape(1, d), wr2, wrh, rbias)


def _scatter_body(pos_ref, tail_ref, h_ref, xs_ref, stage, sem):
    i = pl.program_id(0)
    slot = i % 2

    @pl.when(i == 0)
    def _():
        stage[1, :TM] = jnp.zeros((TM,) + stage.shape[2:], stage.dtype)

        def fill(row):
            return pltpu.make_async_copy(stage.at[1, pl.ds(0, TM)],
                                         xs_ref.at[pl.ds(pl.multiple_of(row, TM), TM)], sem.at[1])

        unused = [tail_ref[N_BUCKETS] + k * TM for k in range(N_BUCKETS)]
        for k in range(N_BUCKETS):
            fill(tail_ref[k]).start()
            pl.when(unused[k] < xs_ref.shape[0])(lambda k=k: fill(unused[k]).start())
        for k in range(N_BUCKETS):
            fill(tail_ref[k]).wait()
            pl.when(unused[k] < xs_ref.shape[0])(lambda k=k: fill(unused[k]).wait())

    @pl.when(i >= 2)
    def _():
        _row_dma_wait(xs_ref, stage.at[slot], sem.at[slot])

    stage[slot] = h_ref[...]

    for r in range(stage.shape[1]):
        pltpu.make_async_copy(stage.at[slot, pl.ds(r, 1)], xs_ref.at[pl.ds(pos_ref[0, 0, r], 1)],
                              sem.at[slot]).start(priority=r % 2)

    @pl.when(i == pl.num_programs(0) - 1)
    def _():
        @pl.when(i >= 1)
        def _():
            _row_dma_wait(xs_ref, stage.at[1 - slot], sem.at[1 - slot])
        _row_dma_wait(xs_ref, stage.at[slot], sem.at[slot])


def _scatter_rows(h2p, pos3, tail_rows, n_rows_sorted):
    rows = SCATTER_TILES * TM
    assert pos3.shape[0] % SCATTER_TILES == 0
    n_steps = pos3.shape[0] // SCATTER_TILES
    width = h2p.shape[1]
    return pl.pallas_call(
        _scatter_body,
        grid=(n_steps,),
        in_specs=[pl.BlockSpec((1, 1, rows), lambda i: (i, 0, 0), memory_space=pltpu.SMEM),
                  pl.BlockSpec(memory_space=pltpu.SMEM),
                  pl.BlockSpec((rows, width), lambda i: (i, 0))],
        out_specs=pl.BlockSpec(memory_space=pl.ANY),
        out_shape=jax.ShapeDtypeStruct((n_rows_sorted, width), jnp.int32),
        scratch_shapes=[pltpu.VMEM((2, rows, width), jnp.int32), pltpu.SemaphoreType.DMA((2,))],
        compiler_params=_cparams("arbitrary"),
        name="moe_scatter",
    )(pos3.reshape(n_steps, 1, rows), tail_rows, h2p)


def _moe_body(ea_ref, eb_ref, act_ref, new_ref, par_ref, hasn_ref, nea_ref, neb_ref, xblk_ref,
              x_ref, wg_hbm, wu_hbm, wd_hbm, y_ref, wg_f, wu_f, wd_f, wgu_s, wd_s, wsem, *, d_exp, layer):
    i = pl.program_id(0)
    half = wd_s.shape[2] // 2

    def weight_copies(slot, experts):
        return [pltpu.make_async_copy(hbm.at[layer, ex], buf.at[slot, e], wsem.at[slot])
                for e, ex in enumerate(experts) for hbm, buf in ((wg_hbm, wg_f), (wu_hbm, wu_f), (wd_hbm, wd_f))]

    @pl.when(i == 0)
    def _():
        for cp in weight_copies(par_ref[0], (ea_ref[0], eb_ref[0])):
            cp.start()

    @pl.when(new_ref[i] > 0)
    def _():
        slot = par_ref[i]
        for cp in weight_copies(slot, (ea_ref[i], eb_ref[i])):
            cp.wait()

        @pl.when(hasn_ref[i] > 0)
        def _():
            for cp in weight_copies(1 - slot, (nea_ref[i], neb_ref[i])):
                cp.start()

        for e in range(2):
            wgu_s[e, :, :d_exp] = wg_f[slot, e].astype(BF16)
            wgu_s[e, :, d_exp:] = wu_f[slot, e].astype(BF16)
            wd_s[e] = wd_f[slot, e].astype(BF16)

    @pl.when(act_ref[i] > 0)
    def _():
        xw = x_ref[...]
        x = _unpack_bf16_pairs(xw[:, :half]).astype(BF16)
        rec = pltpu.bitcast(xw[:, half:], F32)

        def expert(e):
            gu = jnp.dot(x, wgu_s[e], preferred_element_type=F32)
            gate = gu[:, :d_exp]
            h = gate * jax.nn.sigmoid(gate) * gu[:, d_exp:] * rec[:, 1 + e:2 + e]
            return jnp.dot(h.astype(BF16), wd_s[e], preferred_element_type=F32)

        y_ref[...] = _pack_bf16_pairs(expert(0) + expert(1))

    @pl.when(act_ref[i] == 0)
    def _():
        y_ref[...] = jnp.zeros_like(y_ref)


def _moe(xs, w_gate, w_up, w_down, layer, sched):
    n_rows, width = xs.shape
    d_exp, d = w_down.shape[2:]
    n_tiles = n_rows // TM
    body = functools.partial(_moe_body, d_exp=d_exp, layer=layer)
    grid_spec = pltpu.PrefetchScalarGridSpec(
        num_scalar_prefetch=len(sched),
        grid=(n_tiles,),
        in_specs=[pl.BlockSpec((TM, width), lambda i, *s: (s[-1][i], 0)),
                  pl.BlockSpec(memory_space=pl.ANY), pl.BlockSpec(memory_space=pl.ANY),
                  pl.BlockSpec(memory_space=pl.ANY)],
        out_specs=pl.BlockSpec((TM, d // 2), lambda i, *_: (i, 0)),
        scratch_shapes=[pltpu.VMEM((2, 2, d, d_exp), F32), pltpu.VMEM((2, 2, d, d_exp), F32),
                        pltpu.VMEM((2, 2, d_exp, d), F32),
                        pltpu.VMEM((2, d, 2 * d_exp), BF16), pltpu.VMEM((2, d_exp, d), BF16),
                        pltpu.SemaphoreType.DMA((2,))],
    )
    return pl.pallas_call(
        body, grid_spec=grid_spec,
        out_shape=jax.ShapeDtypeStruct((n_rows, d // 2), jnp.int32),
        compiler_params=_cparams("arbitrary"),
        name="moe_pairs",
    )(*sched, xs, w_gate, w_up, w_down)


def _dispatch(route, counts, n_tiles):
    n = route.shape[1]
    bucket = route[0].astype(jnp.int32)
    rank = route[3].astype(jnp.int32)
    counts = counts[:N_BUCKETS, 0].astype(jnp.int32)
    padded = ((counts + TM - 1) // TM) * TM
    ends = jnp.cumsum(padded)
    off = ends - padded

    def lookup(idx, table):
        return jnp.sum(jnp.where(idx[:, None] == jnp.arange(table.shape[0])[None, :], table[None, :], 0), axis=1)

    pos = lookup(bucket, off) + rank
    tile_start = jnp.arange(n_tiles, dtype=jnp.int32) * TM
    active = tile_start < ends[-1]
    last_start = jnp.maximum(ends[-1] - TM, 0)
    tb = jnp.sum((jnp.where(active, tile_start, last_start)[:, None] >= ends[None, :]).astype(jnp.int32), axis=1)
    tb = jnp.minimum(tb, N_BUCKETS - 1)
    ea = (tb // len(PAIRS)) * GROUP_SIZE + lookup(tb % len(PAIRS), jnp.array([p[0] for p in PAIRS], jnp.int32))
    eb = (tb // len(PAIRS)) * GROUP_SIZE + lookup(tb % len(PAIRS), jnp.array([p[1] for p in PAIRS], jnp.int32))
    tiles = jnp.arange(n_tiles, dtype=jnp.int32)
    new = jnp.concatenate([jnp.ones((1,), jnp.int32), (tb[1:] != tb[:-1]).astype(jnp.int32)])
    first_at = jnp.where(new > 0, tiles, n_tiles)
    nxt = jnp.concatenate([lax.cummin(first_at, reverse=True)[1:], jnp.full((1,), n_tiles, jnp.int32)])
    has_next = (nxt < n_tiles).astype(jnp.int32)
    nxt = jnp.minimum(nxt, n_tiles - 1)
    parity = (jnp.cumsum(new) - 1) % 2
    last_used = jnp.maximum(ends[-1] // TM - 1, 0)
    sched = (ea, eb, active.astype(jnp.int32), new, parity.astype(jnp.int32), has_next,
             lookup(nxt, ea), lookup(nxt, eb), jnp.minimum(tiles, last_used))
    tail_rows = jnp.concatenate([jnp.where(padded > 0, ends - TM, 0), ends[-1:]]).astype(jnp.int32)
    return pos.reshape(n // TM, 1, TM), tail_rows, sched


def _rope_tables(seq, ctx_len):
    t = jnp.arange(seq)
    quarter = HEAD_DIM // 4
    inv_freq = ROPE_THETA ** (-jnp.arange(quarter, dtype=F32) / quarter)
    ang_r = (t // GRID_W).astype(F32)[:, None] * inv_freq
    ang_c = (t % GRID_W).astype(F32)[:, None] * inv_freq
    cos_h = jnp.concatenate([jnp.cos(ang_r)] * 2 + [jnp.cos(ang_c)] * 2, axis=-1)
    sin_h = jnp.concatenate([-jnp.sin(ang_r), jnp.sin(ang_r), -jnp.sin(ang_c), jnp.sin(ang_c)], axis=-1)
    cos_t = jnp.concatenate([cos_h, cos_h], axis=-1)
    sin_t = jnp.concatenate([sin_h, sin_h], axis=-1)
    cos_t = jnp.concatenate([cos_t, jnp.ones((ctx_len, LANES), F32)], axis=0)
    sin_t = jnp.concatenate([sin_t, jnp.zeros((ctx_len, LANES), F32)], axis=0)
    return cos_t, sin_t


def kernel(x, c, ctx, c_ctx, w_ada, b_ada, wqkv_a, wo_a, sink_a, wqkv_b, wo_b, lambda_b, subln_b, ln_attn_g, ln_attn_b, ln_ffn_g, ln_ffn_b, w_router, router_bias, w_gate, w_up, w_down):
    n_batch, seq, d = x.shape
    ctx_len = ctx.shape[1]
    depth = w_ada.shape[0]
    assert seq % TM == 0 and ctx_len == TM and seq >= TM + 2 * WINDOW
    assert n_batch + 1 <= ADA_ROWS and d == A_Q_HEADS * HEAD_DIM
    tiles_lat = seq // TM
    alpha = (2 * depth) ** 0.25
    q_cols = A_Q_HEADS * HEAD_DIM
    kv_cols = A_KV_HEADS * HEAD_DIM

    cc = jnp.concatenate([c, c_ctx[None, :], jnp.zeros((ADA_ROWS - n_batch - 1, d), F32)], axis=0)
    mods = _ada_modulation(cc, w_ada, b_ada).reshape(depth, ADA_ROWS, 1, 6 * d)
    cos_t, sin_t = _rope_tables(seq, ctx_len)

    xs = jnp.concatenate([x, ctx], axis=1).reshape(n_batch * (seq + ctx_len), d)
    wr_hi = w_router.astype(BF16)
    wr_lo = (w_router - wr_hi.astype(F32)).astype(BF16)
    wr2 = jnp.concatenate([wr_hi.T, wr_lo.T], axis=0)
    wrh = wr_hi.T
    rbias = router_bias.reshape(N_EXPERTS, 1)

    y = None
    mode = "all"
    for i in range(depth):
        last = i == depth - 1
        j = i // 2
        if i % 2 == 0:
            w = wqkv_a[j].astype(BF16)
            n_rope, dup_from = q_cols + kv_cols, q_cols
        else:
            w = wqkv_b[j].astype(BF16)
            n_rope, dup_from = 2 * q_cols, w.shape[1]
        outs = _lnqkv(xs, y, mods[i - 1] if i else None, ln_ffn_g[i - 1] if i else None,
                      ln_ffn_b[i - 1] if i else None, mods[i], w, cos_t, sin_t,
                      mode_in="all", n_batch=n_batch, tiles_lat=tiles_lat,
                      alpha=alpha, n_rope=n_rope, dup_from=dup_from)
        if i:
            xs, qkv = outs
        else:
            (qkv,) = outs
        if i % 2 == 0:
            att = _attn_a(qkv, sink_a[j], n_batch=n_batch, seq=seq, ctx_len=ctx_len, with_ctx=not last)
            wo = wo_a[j].astype(BF16)
        else:
            lam_init = 0.8 - 0.6 * math.exp(-0.3 * i)
            att = _attn_b(qkv, lambda_b[j], subln_b[j], n_batch=n_batch, seq=seq, ctx_len=ctx_len,
                          with_ctx=not last, lam_init=lam_init)
            wo = wo_b[j].astype(BF16)
        mode = "lat" if last else "all"
        xs, h2p, route, counts = _oproj(att, wo, xs, mods[i], ln_attn_g[i], ln_attn_b[i], wr2, wrh, rbias,
                                        mode_in=mode, n_batch=n_batch, tiles_lat=tiles_lat, alpha=alpha)
        n_tiles = h2p.shape[0] // TM + N_BUCKETS
        pos3, tail_rows, sched = _dispatch(route, counts, n_tiles)
        x_sorted = _scatter_rows(h2p, pos3, tail_rows, n_tiles * TM)
        y = (_moe(x_sorted, w_gate, w_up, w_down, i, sched), pos3)

    (out,) = _lnqkv(xs, y, mods[depth - 1], ln_ffn_g[depth - 1], ln_ffn_b[depth - 1], None, None, None, None,
                    mode_in="compact", n_batch=n_batch, tiles_lat=tiles_lat, alpha=alpha)
    return out.reshape(n_batch, seq, d)
```

```python
import functools
import math

import jax
import jax.numpy as jnp
from jax import lax
from jax.experimental import pallas as pl
from jax.experimental.pallas import tpu as pltpu

F32 = jnp.float32
BF16 = jnp.bfloat16

HEAD_DIM = 64
A_Q_HEADS = 16
A_KV_HEADS = 4
A_GROUP = A_Q_HEADS // A_KV_HEADS
GRID_W = 64
WINDOW = 128
N_EXPERTS = 16
N_GROUPS = 4
GROUP_SIZE = N_EXPERTS // N_GROUPS
ROPE_THETA = 10000.0
LN_EPS = 1e-6
SUBLN_EPS = 1e-5
NEG_INF = -1e30
ATTN_SCALE = HEAD_DIM ** -0.5
LOG2E = math.log2(math.e)
VT_ROWS = 144

LANES = 128
TM = 256
DIFF_HEADS_PER_STEP = 4
SCATTER_TILES = 2
ADA_ROWS = 24
VMEM_LIMIT = 56 * 1024 * 1024

PAIRS = ((0, 1), (0, 2), (0, 3), (1, 2), (1, 3), (2, 3))
N_BUCKETS = N_GROUPS * len(PAIRS)

BUCKET_ROWS = 32
ROUTE_ROWS = 8
HIGH16 = -65536

NT_DIMS = (((1,), (1,)), ((), ()))


def _cparams(*sem):
    return pltpu.CompilerParams(dimension_semantics=sem, vmem_limit_bytes=VMEM_LIMIT)


def _ada_body(c_ref, w_ref, b_ref, o_ref):
    c = c_ref[...]
    sc = c * jax.nn.sigmoid(c)
    o_ref[0] = jnp.dot(sc, w_ref[0], precision=lax.Precision.HIGHEST,
                       preferred_element_type=F32) + b_ref[0]


def _ada_modulation(cc, w_ada, b_ada):
    depth, d, d6 = w_ada.shape
    tn = 1536
    return pl.pallas_call(
        _ada_body,
        grid=(depth, d6 // tn),
        in_specs=[
            pl.BlockSpec((ADA_ROWS, d), lambda l, n: (0, 0)),
            pl.BlockSpec((1, d, tn), lambda l, n: (l, 0, n)),
            pl.BlockSpec((1, 1, tn), lambda l, n: (l, 0, n)),
        ],
        out_specs=pl.BlockSpec((1, ADA_ROWS, tn), lambda l, n: (l, 0, n)),
        out_shape=jax.ShapeDtypeStruct((depth, ADA_ROWS, d6), F32),
        compiler_params=_cparams("arbitrary", "arbitrary"),
        name="ada_mod",
    )(cc, w_ada, b_ada.reshape(depth, 1, d6))


def _tile_maps(mode, tiles_lat, n_batch):
    tpb = tiles_lat + 1
    if mode == "all":
        blk = lambda i: i
        modrow = lambda i: jnp.where(i % tpb == tiles_lat, n_batch, i // tpb)
        pos = lambda i: i % tpb
    elif mode == "lat":
        blk = lambda i: (i // tiles_lat) * tpb + i % tiles_lat
        modrow = lambda i: i // tiles_lat
        pos = lambda i: i % tiles_lat
    else:
        blk = lambda i: i
        modrow = lambda i: i // tiles_lat
        pos = lambda i: i % tiles_lat
    return blk, modrow, pos


def _mod_chunk(mod_ref, k):
    d = mod_ref.shape[2] // 6
    return mod_ref[0, :, k * d:(k + 1) * d]


def _layer_norm(z, g, b):
    mu = jnp.mean(z, axis=-1, keepdims=True)
    zc = z - mu
    var = jnp.mean(zc * zc, axis=-1, keepdims=True)
    return zc * lax.rsqrt(var + LN_EPS) * g + b


def _pack_bf16_pairs(v):
    half = v.shape[1] // 2
    bits = pltpu.bitcast(v.astype(BF16).astype(F32), jnp.int32)
    return lax.shift_right_logical(bits[:, :half], 16) | (bits[:, half:] & HIGH16)


def _unpack_bf16_pairs(w):
    return jnp.concatenate([pltpu.bitcast(lax.shift_left(w, 16), F32),
                            pltpu.bitcast(w & HIGH16, F32)], axis=1)


def _row_dma_wait(hbm_ref, buf_ref, sem):
    pltpu.make_async_copy(hbm_ref.at[pl.ds(0, buf_ref.shape[0])], buf_ref, sem).wait()


def _lnqkv_body(*refs, has_ln, has_qkv, alpha, n_rope, dup_from, n_out):
    refs = list(refs)
    x_ref = refs.pop(0)
    if has_ln:
        y_ref, g2_ref, lng_ref, lnb_ref = refs[:4]
        refs = refs[4:]
    if has_qkv:
        mod_ref, w_ref, cos_ref, sin_ref = refs[:4]
        refs = refs[4:]
    x = x_ref[...]
    if has_ln:
        xo_ref = refs.pop(0)
        z = alpha * x + _mod_chunk(g2_ref, 5) * _unpack_bf16_pairs(y_ref[...])
        x = _layer_norm(z, lng_ref[...], lnb_ref[...])
        xo_ref[...] = x
    if not has_qkv:
        return
    qkv_ref = refs.pop(0)
    h = (x * (1.0 + _mod_chunk(mod_ref, 1)) + _mod_chunk(mod_ref, 0)).astype(BF16)
    r = jnp.dot(h, w_ref[...], preferred_element_type=F32)
    cos = cos_ref[...]
    sin = sin_ref[...]
    lane = lax.broadcasted_iota(jnp.int32, (TM, LANES), 1)
    first16 = (lane % 32) < 16
    low_half = lane < HEAD_DIM
    q_cols = A_Q_HEADS * HEAD_DIM
    dst = 0
    for c in range(n_out // LANES):
        seg = r[:, c * LANES:(c + 1) * LANES]
        col = c * LANES
        if col < q_cols:
            seg = seg * (ATTN_SCALE * LOG2E)
        if col < n_rope:
            rot = jnp.where(first16, pltpu.roll(seg, LANES - 16, 1), pltpu.roll(seg, 16, 1))
            seg = seg * cos + rot * sin
        if col >= dup_from:
            swapped = pltpu.roll(seg, HEAD_DIM, 1)
            qkv_ref[:, dst:dst + LANES] = jnp.where(low_half, seg, swapped).astype(BF16)
            qkv_ref[:, dst + LANES:dst + 2 * LANES] = jnp.where(low_half, swapped, seg).astype(BF16)
            dst += 2 * LANES
        else:
            qkv_ref[:, dst:dst + LANES] = seg.astype(BF16)
            dst += LANES


def _lnqkv(x, y, mod_ln, ln_g, ln_b, mod_qkv, w, cos_t, sin_t, *, mode_in, n_batch, tiles_lat,
           alpha, n_rope=0, dup_from=0):
    d = x.shape[1]
    has_ln = y is not None
    has_qkv = w is not None
    n_tiles = (n_batch * (tiles_lat + 1)) if mode_in == "all" else n_batch * tiles_lat
    blk, modrow, pos = _tile_maps(mode_in, tiles_lat, n_batch)
    row = pl.BlockSpec((TM, d), lambda i: (blk(i), 0))
    vec = pl.BlockSpec((1, d), lambda i: (0, 0))

    modspec = pl.BlockSpec((1, 1, 6 * d), lambda i: (modrow(i), 0, 0))

    args, in_specs, out_specs, out_shape, scratch = [x], [row], [], [], []
    n_out = 0
    if has_ln:
        args += [y, mod_ln, ln_g.reshape(1, d), ln_b.reshape(1, d)]
        in_specs += [pl.BlockSpec((TM, y.shape[1]), lambda i: (blk(i), 0)), modspec, vec, vec]
        out_specs.append(row)
        out_shape.append(jax.ShapeDtypeStruct(x.shape, F32))
    if has_qkv:
        n_out = w.shape[1]
        n_store = n_out + (n_out - dup_from)
        args += [mod_qkv, w, cos_t, sin_t]
        in_specs += [modspec,
                     pl.BlockSpec((d, n_out), lambda i: (0, 0)),
                     pl.BlockSpec((TM, LANES), lambda i: (pos(i), 0)),
                     pl.BlockSpec((TM, LANES), lambda i: (pos(i), 0))]
        out_specs.append(pl.BlockSpec((TM, n_store), lambda i: (blk(i), 0)))
        out_shape.append(jax.ShapeDtypeStruct((x.shape[0], n_store), BF16))
    body = functools.partial(_lnqkv_body, has_ln=has_ln, has_qkv=has_qkv, alpha=alpha,
                             n_rope=n_rope, dup_from=dup_from, n_out=n_out)
    return pl.pallas_call(
        body, grid=(n_tiles,), in_specs=in_specs, out_specs=out_specs, out_shape=out_shape,
        scratch_shapes=scratch, compiler_params=_cparams("arbitrary"),
        name=("ln_" if has_ln else "") + ("qkv" if has_qkv else "out"),
    )(*args)


def _softmax_parts(scores, biases, sink):
    scores = [s if bias is None else s + bias[...] for s, bias in zip(scores, biases)]
    m = None
    for s in scores:
        sm = jnp.max(s, axis=-1, keepdims=True)
        m = sm if m is None else jnp.maximum(m, sm)
    if sink is not None:
        m = jnp.maximum(m, sink)
    probs = []
    denom = None
    for s in scores:
        e = jnp.exp2(s - m)
        es = jnp.sum(e, axis=-1, keepdims=True)
        denom = es if denom is None else denom + es
        probs.append(e.astype(BF16))
    if sink is not None:
        denom = denom + jnp.exp2(sink - m)
    return probs, 1.0 / denom


def _attn_a_body(sink_ref, q_ref, k_ref, v_ref, o_ref, bias_ref, *, seq, tiles_lat):
    qi = pl.program_id(1)
    lane = lax.broadcasted_iota(jnp.int32, (TM, LANES), 1)
    low_half = lane < HEAD_DIM
    span = TM + 2 * WINDOW
    rows = lax.broadcasted_iota(jnp.int32, (A_GROUP * TM, 1), 0)

    def run(parts):
        all_scores = []
        for h in range(A_KV_HEADS):
            qs = []
            for g in range(A_GROUP):
                q2 = q_ref[:, (h * A_GROUP + g) // 2 * LANES:((h * A_GROUP + g) // 2 + 1) * LANES]
                keep = low_half if g % 2 == 0 else jnp.logical_not(low_half)
                qs.append(jnp.where(keep, q2, jnp.zeros_like(q2)))
            qh = jnp.concatenate(qs, axis=0)
            scores = []
            for (r0, nr, _) in parts:
                kk = k_ref[pl.ds(r0, nr), h * LANES:(h + 1) * LANES]
                scores.append(lax.dot_general(qh, kk, NT_DIMS, preferred_element_type=F32))
            all_scores.append(scores)
        for h in range(A_KV_HEADS):
            sink = jnp.zeros((A_GROUP * TM, 1), F32)
            for g in range(A_GROUP):
                sink = jnp.where(rows // TM == g, sink_ref[h * A_GROUP + g] * LOG2E, sink)
            probs, inv = _softmax_parts(all_scores[h], [p[2] for p in parts], sink)
            acc = None
            for p, (r0, nr, _) in zip(probs, parts):
                vv = v_ref[pl.ds(r0, nr), h * LANES:(h + 1) * LANES]
                o = jnp.dot(p, vv, preferred_element_type=F32)
                acc = o if acc is None else acc + o
            acc = acc * inv
            for gp in range(A_GROUP // 2):
                grp = h * (A_GROUP // 2) + gp
                o_ref[:, grp * LANES:(grp + 1) * LANES] = jnp.where(
                    low_half, acc[2 * gp * TM:(2 * gp + 1) * TM], acc[(2 * gp + 1) * TM:(2 * gp + 2) * TM]
                ).astype(BF16)

    @pl.when(qi < tiles_lat)
    def _():
        start = pl.multiple_of(jnp.clip(qi * TM - WINDOW, 0, seq - span), WINDOW)
        qpos = qi * TM + lax.broadcasted_iota(jnp.int32, (TM, span), 0)
        kpos = start + lax.broadcasted_iota(jnp.int32, (TM, span), 1)
        bias = jnp.where(jnp.abs(kpos - qpos) <= WINDOW, 0.0, NEG_INF)
        for g in range(A_GROUP):
            bias_ref[g * TM:(g + 1) * TM, :] = bias
        run([(start, span, bias_ref), (seq, k_ref.shape[0] - seq, None)])

    @pl.when(qi >= tiles_lat)
    def _():
        run([(seq, k_ref.shape[0] - seq, None)])


def _attn_a(qkv, sink, *, n_batch, seq, ctx_len, with_ctx):
    tiles_lat = seq // TM
    tpb = tiles_lat + ctx_len // TM
    rows_b = seq + ctx_len
    q_cols = A_Q_HEADS * HEAD_DIM
    kv_cols = 2 * A_KV_HEADS * HEAD_DIM
    nq = tpb if with_ctx else tiles_lat
    body = functools.partial(_attn_a_body, seq=seq, tiles_lat=tiles_lat)
    return pl.pallas_call(
        body,
        grid=(n_batch, nq),
        in_specs=[
            pl.BlockSpec(memory_space=pltpu.SMEM),
            pl.BlockSpec((TM, q_cols), lambda b, i: (b * tpb + i, 0)),
            pl.BlockSpec((rows_b, kv_cols), lambda b, i: (b, q_cols // kv_cols)),
            pl.BlockSpec((rows_b, kv_cols), lambda b, i: (b, q_cols // kv_cols + 1)),
        ],
        out_specs=pl.BlockSpec((TM, q_cols), lambda b, i: (b * nq + i, 0)),
        out_shape=jax.ShapeDtypeStruct((n_batch * nq * TM, q_cols), BF16),
        scratch_shapes=[pltpu.VMEM((A_GROUP * TM, TM + 2 * WINDOW), F32)],
        compiler_params=_cparams("arbitrary", "arbitrary"),
        name="attn_window",
    )(sink, qkv, qkv, qkv)


def _attn_b_body(lam_ref, g_ref, q_ref, k_ref, v_ref, o_ref, vt_ref, *, seq, tiles_lat, lam_init, heads):
    qi = pl.program_id(2)
    lp = lam_ref[...]
    lam = (jnp.exp(jnp.sum(lp[0:1] * lp[1:2], axis=-1, keepdims=True))
           - jnp.exp(jnp.sum(lp[2:3] * lp[3:4], axis=-1, keepdims=True)) + lam_init)
    lane = lax.broadcasted_iota(jnp.int32, (TM, LANES), 1)
    low_half = lane < HEAD_DIM
    n_keys = k_ref.shape[0]

    @pl.when(qi == 0)
    def _():
        for hh in range(heads):
            vt_ref[hh, :LANES, :] = v_ref[:, hh * LANES:(hh + 1) * LANES].astype(F32).T.astype(BF16)
            row = lax.broadcasted_iota(jnp.int32, (VT_ROWS - LANES, n_keys), 0)
            vt_ref[hh, LANES:, :] = jnp.where(row == 0, 1.0, 0.0).astype(BF16)

    def run(r0, nr):
        scores = []
        for hh in range(heads):
            cols = slice(hh * LANES, (hh + 1) * LANES)
            q = q_ref[:, cols]
            kk = k_ref[pl.ds(r0, nr), cols]
            for qm in (jnp.where(low_half, q, jnp.zeros_like(q)), jnp.where(low_half, jnp.zeros_like(q), q)):
                scores.append(lax.dot_general(kk, qm, NT_DIMS, preferred_element_type=F32))
        for hh in range(heads):
            vt = vt_ref[hh, :, pl.ds(r0, nr)]
            outs = []
            for s in scores[2 * hh:2 * hh + 2]:
                e = jnp.exp2(s - jnp.max(s, axis=0, keepdims=True)).astype(BF16)
                ov = jnp.dot(vt, e, preferred_element_type=F32)
                outs.append((ov[:LANES], 1.0 / ov[LANES:LANES + 1]))
            o = outs[0][0] * outs[0][1] - outs[1][0] * (lam * outs[1][1])
            ms = jnp.mean(o * o, axis=0, keepdims=True)
            o = o * lax.rsqrt(ms + SUBLN_EPS) * g_ref[...] * (1.0 - lam_init)
            o_ref[:, hh * LANES:(hh + 1) * LANES] = o.T.astype(BF16)

    @pl.when(qi < tiles_lat)
    def _():
        run(0, k_ref.shape[0])

    @pl.when(qi >= tiles_lat)
    def _():
        run(seq, k_ref.shape[0] - seq)


def _attn_b(qkv, lam_params, subln_g, *, n_batch, seq, ctx_len, with_ctx, lam_init):
    tiles_lat = seq // TM
    tpb = tiles_lat + ctx_len // TM
    rows_b = seq + ctx_len
    n_heads = qkv.shape[1] // (3 * LANES)
    n_hgrp = n_heads // DIFF_HEADS_PER_STEP
    width = DIFF_HEADS_PER_STEP * LANES
    nq = tpb if with_ctx else tiles_lat
    body = functools.partial(_attn_b_body, seq=seq, tiles_lat=tiles_lat, lam_init=lam_init,
                             heads=DIFF_HEADS_PER_STEP)
    return pl.pallas_call(
        body,
        grid=(n_batch, n_hgrp, nq),
        in_specs=[
            pl.BlockSpec((4, HEAD_DIM), lambda b, h, i: (0, 0)),
            pl.BlockSpec((LANES, 1), lambda b, h, i: (0, 0)),
            pl.BlockSpec((TM, width), lambda b, h, i: (b * tpb + i, h)),
            pl.BlockSpec((rows_b, width), lambda b, h, i: (b, n_hgrp + h)),
            pl.BlockSpec((rows_b, width), lambda b, h, i: (b, 2 * n_hgrp + h)),
        ],
        out_specs=pl.BlockSpec((TM, width), lambda b, h, i: (b * nq + i, h)),
        out_shape=jax.ShapeDtypeStruct((n_batch * nq * TM, n_heads * LANES), BF16),
        compiler_params=_cparams("arbitrary", "arbitrary", "arbitrary"),
        scratch_shapes=[pltpu.VMEM((DIFF_HEADS_PER_STEP, VT_ROWS, rows_b), BF16)],
        name="attn_diff",
    )(lam_params, subln_g.reshape(LANES, 1), qkv, qkv, qkv)


def _route_rows(logits, bias):
    s = jax.nn.sigmoid(logits)
    biased = s + bias
    b = [biased[e:e + 1, :] for e in range(N_EXPERTS)]
    u = [s[e:e + 1, :] for e in range(N_EXPERTS)]
    gscore = []
    for g in range(N_GROUPS):
        best = None
        for (i, j) in PAIRS:
            t = b[GROUP_SIZE * g + i] + b[GROUP_SIZE * g + j]
            best = t if best is None else jnp.maximum(best, t)
        gscore.append(best)
    gsel = jnp.zeros(gscore[0].shape, jnp.int32)
    gbest = gscore[0]
    for g in range(1, N_GROUPS):
        better = gscore[g] > gbest
        gsel = jnp.where(better, g, gsel)
        gbest = jnp.where(better, gscore[g], gbest)

    def pick(rows, k):
        out = rows[GROUP_SIZE * (N_GROUPS - 1) + k]
        for g in range(N_GROUPS - 2, -1, -1):
            out = jnp.where(gsel == g, rows[GROUP_SIZE * g + k], out)
        return out

    v = [pick(b, k) for k in range(GROUP_SIZE)]
    w = [pick(u, k) for k in range(GROUP_SIZE)]
    sel = []
    for k in range(GROUP_SIZE):
        cnt = jnp.zeros(gsel.shape, jnp.int32)
        for j in range(GROUP_SIZE):
            if j == k:
                continue
            beats = (v[j] >= v[k]) if j < k else (v[j] > v[k])
            cnt = cnt + jnp.where(beats, 1, 0)
        sel.append(cnt < 2)
    pidx = jnp.zeros(gsel.shape, jnp.int32)
    u_lo = jnp.zeros(gbest.shape, F32)
    u_hi = jnp.zeros(gbest.shape, F32)
    for idx, (i, j) in enumerate(PAIRS):
        both = jnp.logical_and(sel[i], sel[j])
        pidx = jnp.where(both, idx, pidx)
        u_lo = jnp.where(both, w[i], u_lo)
        u_hi = jnp.where(both, w[j], u_hi)
    tot = u_lo + u_hi
    bucket = (gsel * len(PAIRS) + pidx).astype(F32)
    return bucket, u_lo / tot, u_hi / tot


def _oproj_body(a_ref, wo_ref, x_ref, mod_ref, lng_ref, lnb_ref, wr2_ref, wrh_ref, rb_ref,
                xo_ref, h2p_ref, r_ref, cnt_ref, *, alpha):
    i = pl.program_id(0)
    d = x_ref.shape[1]
    al = jnp.dot(a_ref[...], wo_ref[...], preferred_element_type=F32)
    z = alpha * x_ref[...] + _mod_chunk(mod_ref, 2) * al
    xn = _layer_norm(z, lng_ref[...], lnb_ref[...])
    xo_ref[...] = xn
    h2 = xn * (1.0 + _mod_chunk(mod_ref, 4)) + _mod_chunk(mod_ref, 3)
    hi = h2.astype(BF16)
    lo = (h2 - hi.astype(F32)).astype(BF16)
    l2 = lax.dot_general(wr2_ref[...], hi, NT_DIMS, preferred_element_type=F32)
    l1 = lax.dot_general(wrh_ref[...], lo, NT_DIMS, preferred_element_type=F32)
    logits = l2[:N_EXPERTS] + l2[N_EXPERTS:] + l1
    bucket, w_lo, w_hi = _route_rows(logits, rb_ref[...])

    @pl.when(i == 0)
    def _():
        cnt_ref[...] = jnp.zeros_like(cnt_ref)

    onehot = jnp.where(lax.broadcasted_iota(jnp.int32, (BUCKET_ROWS, TM), 0).astype(F32) == bucket, 1.0, 0.0)
    earlier = (lax.broadcasted_iota(jnp.int32, (TM, TM), 0) < lax.broadcasted_iota(jnp.int32, (TM, TM), 1))
    before = jnp.dot(onehot.astype(BF16), jnp.where(earlier, 1.0, 0.0).astype(BF16),
                     preferred_element_type=F32)
    cnt = cnt_ref[...]
    rank = jnp.sum(onehot * (before + cnt[:, 0:1]), axis=0, keepdims=True)
    cnt_ref[...] = cnt + jnp.sum(onehot, axis=1, keepdims=True)

    r_ref[0:1, :] = bucket
    r_ref[1:2, :] = w_lo
    r_ref[2:3, :] = w_hi
    r_ref[3:4, :] = rank
    r_ref[4:5, :] = (i * TM + lax.broadcasted_iota(jnp.int32, (1, TM), 1)).astype(F32)
    r_ref[5:6, :] = jnp.ones((1, TM), F32)
    r_ref[6:ROUTE_ROWS, :] = jnp.zeros((ROUTE_ROWS - 6, TM), F32)
    rec = jnp.concatenate([r_ref[...], jnp.zeros((LANES - ROUTE_ROWS, TM), F32)], axis=0).T
    h2p_ref[:, :d // 2] = _pack_bf16_pairs(h2)
    h2p_ref[:, d // 2:] = pltpu.bitcast(rec, jnp.int32)


def _oproj(a, wo, x, mod, ln_g, ln_b, wr2, wrh, rbias, *, mode_in, n_batch, tiles_lat, alpha):
    d = x.shape[1]
    compact = mode_in == "lat"
    n_tiles = n_batch * tiles_lat if compact else n_batch * (tiles_lat + 1)
    blk, modrow, _ = _tile_maps(mode_in, tiles_lat, n_batch)
    row_in = pl.BlockSpec((TM, d), lambda i: (blk(i), 0))
    row_out = pl.BlockSpec((TM, d), lambda i: (i, 0))
    vec = pl.BlockSpec((1, d), lambda i: (0, 0))

    modspec = pl.BlockSpec((1, 1, 6 * d), lambda i: (modrow(i), 0, 0))

    body = functools.partial(_oproj_body, alpha=alpha)
    return pl.pallas_call(
        body,
        grid=(n_tiles,),
        in_specs=[row_out, pl.BlockSpec((d, d), lambda i: (0, 0)), row_in, modspec, vec, vec,
                  pl.BlockSpec((2 * N_EXPERTS, d), lambda i: (0, 0)),
                  pl.BlockSpec((N_EXPERTS, d), lambda i: (0, 0)),
                  pl.BlockSpec((N_EXPERTS, 1), lambda i: (0, 0))],
        out_specs=[row_out, pl.BlockSpec((TM, d // 2 + LANES), lambda i: (i, 0)),
                   pl.BlockSpec((ROUTE_ROWS, TM), lambda i: (0, i)),
                   pl.BlockSpec((BUCKET_ROWS, LANES), lambda i: (0, 0))],
        out_shape=[jax.ShapeDtypeStruct((n_tiles * TM, d), F32),
                   jax.ShapeDtypeStruct((n_tiles * TM, d // 2 + LANES), jnp.int32),
                   jax.ShapeDtypeStruct((ROUTE_ROWS, n_tiles * TM), F32),
                   jax.ShapeDtypeStruct((BUCKET_ROWS, LANES), F32)],
        compiler_params=_cparams("arbitrary"),
        name="oproj_ln_route",
    )(a, wo, x, mod, ln_g.reshape(1, d), ln_b.reshape(1, d), wr2, wrh, rbias)


def _scatter_body(pos_ref, tail_ref, h_ref, xs_ref, stage, sem):
    i = pl.program_id(0)
    slot = i % 2

    @pl.when(i == 0)
    def _():
        stage[1, :TM] = jnp.zeros((TM,) + stage.shape[2:], stage.dtype)

        def fill(row):
            return pltpu.make_async_copy(stage.at[1, pl.ds(0, TM)],
                                         xs_ref.at[pl.ds(pl.multiple_of(row, TM), TM)], sem.at[1])

        unused = [tail_ref[N_BUCKETS] + k * TM for k in range(N_BUCKETS)]
        for k in range(N_BUCKETS):
            fill(tail_ref[k]).start()
            pl.when(unused[k] < xs_ref.shape[0])(lambda k=k: fill(unused[k]).start())
        for k in range(N_BUCKETS):
            fill(tail_ref[k]).wait()
            pl.when(unused[k] < xs_ref.shape[0])(lambda k=k: fill(unused[k]).wait())

    @pl.when(i >= 2)
    def _():
        _row_dma_wait(xs_ref, stage.at[slot], sem.at[slot])

    stage[slot] = h_ref[...]

    for r in range(stage.shape[1]):
        pltpu.make_async_copy(stage.at[slot, pl.ds(r, 1)], xs_ref.at[pl.ds(pos_ref[0, 0, r], 1)],
                              sem.at[slot]).start(priority=r % 2)

    @pl.when(i == pl.num_programs(0) - 1)
    def _():
        @pl.when(i >= 1)
        def _():
            _row_dma_wait(xs_ref, stage.at[1 - slot], sem.at[1 - slot])
        _row_dma_wait(xs_ref, stage.at[slot], sem.at[slot])


def _scatter_rows(h2p, pos3, tail_rows, n_rows_sorted):
    rows = SCATTER_TILES * TM
    assert pos3.shape[0] % SCATTER_TILES == 0
    n_steps = pos3.shape[0] // SCATTER_TILES
    width = h2p.shape[1]
    return pl.pallas_call(
        _scatter_body,
        grid=(n_steps,),
        in_specs=[pl.BlockSpec((1, 1, rows), lambda i: (i, 0, 0), memory_space=pltpu.SMEM),
                  pl.BlockSpec(memory_space=pltpu.SMEM),
                  pl.BlockSpec((rows, width), lambda i: (i, 0))],
        out_specs=pl.BlockSpec(memory_space=pl.ANY),
        out_shape=jax.ShapeDtypeStruct((n_rows_sorted, width), jnp.int32),
        scratch_shapes=[pltpu.VMEM((2, rows, width), jnp.int32), pltpu.SemaphoreType.DMA((2,))],
        compiler_params=_cparams("arbitrary"),
        name="moe_scatter",
    )(pos3.reshape(n_steps, 1, rows), tail_rows, h2p)


def _moe_body(ea_ref, eb_ref, act_ref, new_ref, par_ref, hasn_ref, nea_ref, neb_ref, xblk_ref,
              x_ref, wg_hbm, wu_hbm, wd_hbm, y_hbm, wg_f, wu_f, wd_f, wgu_s, wd_s, wsem,
              ystage, ids_v, ids_s, ysem, isem, *, d_exp, layer, n_tok):
    i = pl.program_id(0)
    n_steps = pl.num_programs(0)
    slot = i % 2
    half = wd_s.shape[2] // 2
    lane_ids = lax.broadcasted_iota(jnp.int32, (1, TM), 1)

    def ids_copy(s):
        return pltpu.make_async_copy(ids_v.at[s, pl.ds(0, 1)], ids_s.at[pl.ds(s, 1)], isem.at[s])

    def scatter_tile(s):
        ids_copy(s).wait()
        for r in range(TM):
            pltpu.make_async_copy(ystage.at[s, pl.ds(r, 1)], y_hbm.at[pl.ds(ids_s[s, r], 1)],
                                  ysem.at[s]).start(priority=r % 2)

    def scatter_wait(s):
        _row_dma_wait(y_hbm, ystage.at[s], ysem.at[s])

    @pl.when(i == 0)
    def _():
        ystage[1] = jnp.zeros(ystage.shape[1:], ystage.dtype)
        ids_v[1, 0:1, :] = n_tok + TM + lane_ids
        ids_copy(1).start()
        fill = pltpu.make_async_copy(ystage.at[1], y_hbm.at[pl.ds(n_tok, TM)], ysem.at[0])
        fill.start()
        fill.wait()

    def weight_copies(slot, experts):
        return [pltpu.make_async_copy(hbm.at[layer, ex], buf.at[slot, e], wsem.at[slot])
                for e, ex in enumerate(experts) for hbm, buf in ((wg_hbm, wg_f), (wu_hbm, wu_f), (wd_hbm, wd_f))]

    @pl.when(i == 0)
    def _():
        for cp in weight_copies(par_ref[0], (ea_ref[0], eb_ref[0])):
            cp.start()

    @pl.when(new_ref[i] > 0)
    def _():
        slot = par_ref[i]
        for cp in weight_copies(slot, (ea_ref[i], eb_ref[i])):
            cp.wait()

        @pl.when(hasn_ref[i] > 0)
        def _():
            for cp in weight_copies(1 - slot, (nea_ref[i], neb_ref[i])):
                cp.start()

        for e in range(2):
            wgu_s[e, :, :d_exp] = wg_f[slot, e].astype(BF16)
            wgu_s[e, :, d_exp:] = wu_f[slot, e].astype(BF16)
            wd_s[e] = wd_f[slot, e].astype(BF16)

    active = act_ref[i] > 0
    prev_active = jnp.logical_and(i > 0, act_ref[jnp.maximum(i - 1, 0)] > 0)

    @pl.when(active)
    def _():
        @pl.when(i >= 1)
        def _():
            scatter_wait(slot)

        scatter_tile(1 - slot)
        xw = x_ref[...]
        x = _unpack_bf16_pairs(xw[:, :half]).astype(BF16)
        rec = pltpu.bitcast(xw[:, half:], F32)

        def expert(e):
            gu = jnp.dot(x, wgu_s[e], preferred_element_type=F32)
            gate = gu[:, :d_exp]
            h = gate * jax.nn.sigmoid(gate) * gu[:, d_exp:] * rec[:, 1 + e:2 + e]
            return jnp.dot(h.astype(BF16), wd_s[e], preferred_element_type=F32)

        ystage[slot] = _pack_bf16_pairs(expert(0) + expert(1))
        rec_t = rec.T
        dump = n_tok + slot * TM + lane_ids
        ids_v[slot, 0:1, :] = jnp.where(rec_t[5:6] > 0.0, rec_t[4:5].astype(jnp.int32), dump)
        ids_copy(slot).start()

    @pl.when(jnp.logical_and(jnp.logical_not(active), prev_active))
    def _():
        scatter_tile(1 - slot)
        scatter_wait(1 - slot)
        scatter_wait(slot)

    @pl.when(jnp.logical_and(active, i == n_steps - 1))
    def _():
        scatter_tile(slot)
        scatter_wait(slot)
        scatter_wait(1 - slot)


def _moe(xs, w_gate, w_up, w_down, layer, sched, n_tok):
    n_rows, width = xs.shape
    d_exp, d = w_down.shape[2:]
    n_tiles = n_rows // TM
    body = functools.partial(_moe_body, d_exp=d_exp, layer=layer, n_tok=n_tok)
    grid_spec = pltpu.PrefetchScalarGridSpec(
        num_scalar_prefetch=len(sched),
        grid=(n_tiles,),
        in_specs=[pl.BlockSpec((TM, width), lambda i, *s: (s[-1][i], 0)),
                  pl.BlockSpec(memory_space=pl.ANY), pl.BlockSpec(memory_space=pl.ANY),
                  pl.BlockSpec(memory_space=pl.ANY)],
        out_specs=pl.BlockSpec(memory_space=pl.ANY),
        scratch_shapes=[pltpu.VMEM((2, 2, d, d_exp), F32), pltpu.VMEM((2, 2, d, d_exp), F32),
                        pltpu.VMEM((2, 2, d_exp, d), F32),
                        pltpu.VMEM((2, d, 2 * d_exp), BF16), pltpu.VMEM((2, d_exp, d), BF16),
                        pltpu.SemaphoreType.DMA((2,)),
                        pltpu.VMEM((2, TM, d // 2), jnp.int32), pltpu.VMEM((2, 8, TM), jnp.int32),
                        pltpu.SMEM((2, TM), jnp.int32),
                        pltpu.SemaphoreType.DMA((2,)), pltpu.SemaphoreType.DMA((2,))],
    )
    return pl.pallas_call(
        body, grid_spec=grid_spec,
        out_shape=jax.ShapeDtypeStruct((n_tok + 2 * TM, d // 2), jnp.int32),
        compiler_params=_cparams("arbitrary"),
        name="moe_pairs",
    )(*sched, xs, w_gate, w_up, w_down)


def _dispatch(route, counts, n_tiles):
    n = route.shape[1]
    bucket = route[0].astype(jnp.int32)
    rank = route[3].astype(jnp.int32)
    counts = counts[:N_BUCKETS, 0].astype(jnp.int32)
    padded = ((counts + TM - 1) // TM) * TM
    ends = jnp.cumsum(padded)
    off = ends - padded

    def lookup(idx, table):
        return jnp.sum(jnp.where(idx[:, None] == jnp.arange(table.shape[0])[None, :], table[None, :], 0), axis=1)

    pos = lookup(bucket, off) + rank
    tile_start = jnp.arange(n_tiles, dtype=jnp.int32) * TM
    active = tile_start < ends[-1]
    last_start = jnp.maximum(ends[-1] - TM, 0)
    tb = jnp.sum((jnp.where(active, tile_start, last_start)[:, None] >= ends[None, :]).astype(jnp.int32), axis=1)
    tb = jnp.minimum(tb, N_BUCKETS - 1)
    ea = (tb // len(PAIRS)) * GROUP_SIZE + lookup(tb % len(PAIRS), jnp.array([p[0] for p in PAIRS], jnp.int32))
    eb = (tb // len(PAIRS)) * GROUP_SIZE + lookup(tb % len(PAIRS), jnp.array([p[1] for p in PAIRS], jnp.int32))
    tiles = jnp.arange(n_tiles, dtype=jnp.int32)
    new = jnp.concatenate([jnp.ones((1,), jnp.int32), (tb[1:] != tb[:-1]).astype(jnp.int32)])
    first_at = jnp.where(new > 0, tiles, n_tiles)
    nxt = jnp.concatenate([lax.cummin(first_at, reverse=True)[1:], jnp.full((1,), n_tiles, jnp.int32)])
    has_next = (nxt < n_tiles).astype(jnp.int32)
    nxt = jnp.minimum(nxt, n_tiles - 1)
    parity = (jnp.cumsum(new) - 1) % 2
    last_used = jnp.maximum(ends[-1] // TM - 1, 0)
    sched = (ea, eb, active.astype(jnp.int32), new, parity.astype(jnp.int32), has_next,
             lookup(nxt, ea), lookup(nxt, eb), jnp.minimum(tiles, last_used))
    tail_rows = jnp.concatenate([jnp.where(padded > 0, ends - TM, 0), ends[-1:]]).astype(jnp.int32)
    return pos.reshape(n // TM, 1, TM), tail_rows, sched


def _rope_tables(seq, ctx_len):
    t = jnp.arange(seq)
    quarter = HEAD_DIM // 4
    inv_freq = ROPE_THETA ** (-jnp.arange(quarter, dtype=F32) / quarter)
    ang_r = (t // GRID_W).astype(F32)[:, None] * inv_freq
    ang_c = (t % GRID_W).astype(F32)[:, None] * inv_freq
    cos_h = jnp.concatenate([jnp.cos(ang_r)] * 2 + [jnp.cos(ang_c)] * 2, axis=-1)
    sin_h = jnp.concatenate([-jnp.sin(ang_r), jnp.sin(ang_r), -jnp.sin(ang_c), jnp.sin(ang_c)], axis=-1)
    cos_t = jnp.concatenate([cos_h, cos_h], axis=-1)
    sin_t = jnp.concatenate([sin_h, sin_h], axis=-1)
    cos_t = jnp.concatenate([cos_t, jnp.ones((ctx_len, LANES), F32)], axis=0)
    sin_t = jnp.concatenate([sin_t, jnp.zeros((ctx_len, LANES), F32)], axis=0)
    return cos_t, sin_t


def kernel(x, c, ctx, c_ctx, w_ada, b_ada, wqkv_a, wo_a, sink_a, wqkv_b, wo_b, lambda_b, subln_b, ln_attn_g, ln_attn_b, ln_ffn_g, ln_ffn_b, w_router, router_bias, w_gate, w_up, w_down):
    n_batch, seq, d = x.shape
    ctx_len = ctx.shape[1]
    depth = w_ada.shape[0]
    assert seq % TM == 0 and ctx_len == TM and seq >= TM + 2 * WINDOW
    assert n_batch + 1 <= ADA_ROWS and d == A_Q_HEADS * HEAD_DIM
    tiles_lat = seq // TM
    alpha = (2 * depth) ** 0.25
    q_cols = A_Q_HEADS * HEAD_DIM
    kv_cols = A_KV_HEADS * HEAD_DIM

    cc = jnp.concatenate([c, c_ctx[None, :], jnp.zeros((ADA_ROWS - n_batch - 1, d), F32)], axis=0)
    mods = _ada_modulation(cc, w_ada, b_ada).reshape(depth, ADA_ROWS, 1, 6 * d)
    cos_t, sin_t = _rope_tables(seq, ctx_len)

    xs = jnp.concatenate([x, ctx], axis=1).reshape(n_batch * (seq + ctx_len), d)
    wr_hi = w_router.astype(BF16)
    wr_lo = (w_router - wr_hi.astype(F32)).astype(BF16)
    wr2 = jnp.concatenate([wr_hi.T, wr_lo.T], axis=0)
    wrh = wr_hi.T
    rbias = router_bias.reshape(N_EXPERTS, 1)

    y = None
    mode = "all"
    for i in range(depth):
        last = i == depth - 1
        j = i // 2
        if i % 2 == 0:
            w = wqkv_a[j].astype(BF16)
            n_rope, dup_from = q_cols + kv_cols, q_cols
        else:
            w = wqkv_b[j].astype(BF16)
            n_rope, dup_from = 2 * q_cols, w.shape[1]
        outs = _lnqkv(xs, y, mods[i - 1] if i else None, ln_ffn_g[i - 1] if i else None,
                      ln_ffn_b[i - 1] if i else None, mods[i], w, cos_t, sin_t,
                      mode_in="all", n_batch=n_batch, tiles_lat=tiles_lat,
                      alpha=alpha, n_rope=n_rope, dup_from=dup_from)
        if i:
            xs, qkv = outs
        else:
            (qkv,) = outs
        if i % 2 == 0:
            att = _attn_a(qkv, sink_a[j], n_batch=n_batch, seq=seq, ctx_len=ctx_len, with_ctx=not last)
            wo = wo_a[j].astype(BF16)
        else:
            lam_init = 0.8 - 0.6 * math.exp(-0.3 * i)
            att = _attn_b(qkv, lambda_b[j], subln_b[j], n_batch=n_batch, seq=seq, ctx_len=ctx_len,
                          with_ctx=not last, lam_init=lam_init)
            wo = wo_b[j].astype(BF16)
        mode = "lat" if last else "all"
        xs, h2p, route, counts = _oproj(att, wo, xs, mods[i], ln_attn_g[i], ln_attn_b[i], wr2, wrh, rbias,
                                        mode_in=mode, n_batch=n_batch, tiles_lat=tiles_lat, alpha=alpha)
        n_tiles = h2p.shape[0] // TM + N_BUCKETS
        pos3, tail_rows, sched = _dispatch(route, counts, n_tiles)
        x_sorted = _scatter_rows(h2p, pos3, tail_rows, n_tiles * TM)
        y = _moe(x_sorted, w_gate, w_up, w_down, i, sched, h2p.shape[0])

    (out,) = _lnqkv(xs, y, mods[depth - 1], ln_ffn_g[depth - 1], ln_ffn_b[depth - 1], None, None, None, None,
                    mode_in="compact", n_batch=n_batch, tiles_lat=tiles_lat, alpha=alpha)
    return out.reshape(n_batch, seq, d)
```

```python
import functools
import math

import jax
import jax.numpy as jnp
from jax import lax
from jax.experimental import pallas as pl
from jax.experimental.pallas import tpu as pltpu

F32 = jnp.float32
BF16 = jnp.bfloat16

HEAD_DIM = 64
A_Q_HEADS = 16
A_KV_HEADS = 4
A_GROUP = A_Q_HEADS // A_KV_HEADS
GRID_W = 64
WINDOW = 128
N_EXPERTS = 16
N_GROUPS = 4
GROUP_SIZE = N_EXPERTS // N_GROUPS
ROPE_THETA = 10000.0
LN_EPS = 1e-6
SUBLN_EPS = 1e-5
NEG_INF = -1e30
ATTN_SCALE = HEAD_DIM ** -0.5
LOG2E = math.log2(math.e)
VT_ROWS = 144

LANES = 128
TM = 256
DIFF_HEADS_PER_STEP = 4
SCATTER_TILES = 2
ADA_ROWS = 24
VMEM_LIMIT = 56 * 1024 * 1024

PAIRS = ((0, 1), (0, 2), (0, 3), (1, 2), (1, 3), (2, 3))
N_BUCKETS = N_GROUPS * len(PAIRS)

BUCKET_ROWS = 32
ROUTE_ROWS = 8

NT_DIMS = (((1,), (1,)), ((), ()))


def _cparams(*sem):
    return pltpu.CompilerParams(dimension_semantics=sem, vmem_limit_bytes=VMEM_LIMIT)


def _ada_body(c_ref, w_ref, b_ref, o_ref):
    c = c_ref[...]
    sc = c * jax.nn.sigmoid(c)
    o_ref[0] = jnp.dot(sc, w_ref[0], precision=lax.Precision.HIGHEST,
                       preferred_element_type=F32) + b_ref[0]


def _ada_modulation(cc, w_ada, b_ada):
    depth, d, d6 = w_ada.shape
    tn = 1536
    return pl.pallas_call(
        _ada_body,
        grid=(depth, d6 // tn),
        in_specs=[
            pl.BlockSpec((ADA_ROWS, d), lambda l, n: (0, 0)),
            pl.BlockSpec((1, d, tn), lambda l, n: (l, 0, n)),
            pl.BlockSpec((1, 1, tn), lambda l, n: (l, 0, n)),
        ],
        out_specs=pl.BlockSpec((1, ADA_ROWS, tn), lambda l, n: (l, 0, n)),
        out_shape=jax.ShapeDtypeStruct((depth, ADA_ROWS, d6), F32),
        compiler_params=_cparams("arbitrary", "arbitrary"),
        name="ada_mod",
    )(cc, w_ada, b_ada.reshape(depth, 1, d6))


def _tile_maps(mode, tiles_lat, n_batch):
    tpb = tiles_lat + 1
    if mode == "all":
        blk = lambda i: i
        modrow = lambda i: jnp.where(i % tpb == tiles_lat, n_batch, i // tpb)
        pos = lambda i: i % tpb
    elif mode == "lat":
        blk = lambda i: (i // tiles_lat) * tpb + i % tiles_lat
        modrow = lambda i: i // tiles_lat
        pos = lambda i: i % tiles_lat
    else:
        blk = lambda i: i
        modrow = lambda i: i // tiles_lat
        pos = lambda i: i % tiles_lat
    return blk, modrow, pos


def _mod_chunk(mod_ref, k):
    d = mod_ref.shape[2] // 6
    return mod_ref[0, :, k * d:(k + 1) * d]


def _layer_norm(z, g, b):
    mu = jnp.mean(z, axis=-1, keepdims=True)
    zc = z - mu
    var = jnp.mean(zc * zc, axis=-1, keepdims=True)
    return zc * lax.rsqrt(var + LN_EPS) * g + b


def _row_dma_wait(hbm_ref, buf_ref, sem):
    pltpu.make_async_copy(hbm_ref.at[pl.ds(0, buf_ref.shape[0])], buf_ref, sem).wait()


def _lnqkv_body(*refs, has_ln, has_qkv, alpha, n_rope, dup_from, n_out):
    refs = list(refs)
    x_ref = refs.pop(0)
    if has_ln:
        y_ref, g2_ref, lng_ref, lnb_ref = refs[:4]
        refs = refs[4:]
    if has_qkv:
        mod_ref, w_ref, cos_ref, sin_ref = refs[:4]
        refs = refs[4:]
    x = x_ref[...]
    if has_ln:
        xo_ref = refs.pop(0)
        z = alpha * x + _mod_chunk(g2_ref, 5) * y_ref[...]
        x = _layer_norm(z, lng_ref[...], lnb_ref[...])
        xo_ref[...] = x
    if not has_qkv:
        return
    qkv_ref = refs.pop(0)
    h = (x * (1.0 + _mod_chunk(mod_ref, 1)) + _mod_chunk(mod_ref, 0)).astype(BF16)
    r = jnp.dot(h, w_ref[...], preferred_element_type=F32)
    cos = cos_ref[...]
    sin = sin_ref[...]
    lane = lax.broadcasted_iota(jnp.int32, (TM, LANES), 1)
    first16 = (lane % 32) < 16
    low_half = lane < HEAD_DIM
    q_cols = A_Q_HEADS * HEAD_DIM
    dst = 0
    for c in range(n_out // LANES):
        seg = r[:, c * LANES:(c + 1) * LANES]
        col = c * LANES
        if col < q_cols:
            seg = seg * (ATTN_SCALE * LOG2E)
        if col < n_rope:
            rot = jnp.where(first16, pltpu.roll(seg, LANES - 16, 1), pltpu.roll(seg, 16, 1))
            seg = seg * cos + rot * sin
        if col >= dup_from:
            swapped = pltpu.roll(seg, HEAD_DIM, 1)
            qkv_ref[:, dst:dst + LANES] = jnp.where(low_half, seg, swapped).astype(BF16)
            qkv_ref[:, dst + LANES:dst + 2 * LANES] = jnp.where(low_half, swapped, seg).astype(BF16)
            dst += 2 * LANES
        else:
            qkv_ref[:, dst:dst + LANES] = seg.astype(BF16)
            dst += LANES


def _lnqkv(x, y, mod_ln, ln_g, ln_b, mod_qkv, w, cos_t, sin_t, *, mode_in, n_batch, tiles_lat,
           alpha, n_rope=0, dup_from=0):
    d = x.shape[1]
    has_ln = y is not None
    has_qkv = w is not None
    n_tiles = (n_batch * (tiles_lat + 1)) if mode_in == "all" else n_batch * tiles_lat
    blk, modrow, pos = _tile_maps(mode_in, tiles_lat, n_batch)
    row = pl.BlockSpec((TM, d), lambda i: (blk(i), 0))
    vec = pl.BlockSpec((1, d), lambda i: (0, 0))

    modspec = pl.BlockSpec((1, 1, 6 * d), lambda i: (modrow(i), 0, 0))

    args, in_specs, out_specs, out_shape, scratch = [x], [row], [], [], []
    n_out = 0
    if has_ln:
        args += [y, mod_ln, ln_g.reshape(1, d), ln_b.reshape(1, d)]
        in_specs += [pl.BlockSpec((TM, y.shape[1]), lambda i: (blk(i), 0)), modspec, vec, vec]
        out_specs.append(row)
        out_shape.append(jax.ShapeDtypeStruct(x.shape, F32))
    if has_qkv:
        n_out = w.shape[1]
        n_store = n_out + (n_out - dup_from)
        args += [mod_qkv, w, cos_t, sin_t]
        in_specs += [modspec,
                     pl.BlockSpec((d, n_out), lambda i: (0, 0)),
                     pl.BlockSpec((TM, LANES), lambda i: (pos(i), 0)),
                     pl.BlockSpec((TM, LANES), lambda i: (pos(i), 0))]
        out_specs.append(pl.BlockSpec((TM, n_store), lambda i: (blk(i), 0)))
        out_shape.append(jax.ShapeDtypeStruct((x.shape[0], n_store), BF16))
    body = functools.partial(_lnqkv_body, has_ln=has_ln, has_qkv=has_qkv, alpha=alpha,
                             n_rope=n_rope, dup_from=dup_from, n_out=n_out)
    return pl.pallas_call(
        body, grid=(n_tiles,), in_specs=in_specs, out_specs=out_specs, out_shape=out_shape,
        scratch_shapes=scratch, compiler_params=_cparams("arbitrary"),
        name=("ln_" if has_ln else "") + ("qkv" if has_qkv else "out"),
    )(*args)


def _softmax_parts(scores, biases, sink):
    scores = [s if bias is None else s + bias[...] for s, bias in zip(scores, biases)]
    m = None
    for s in scores:
        sm = jnp.max(s, axis=-1, keepdims=True)
        m = sm if m is None else jnp.maximum(m, sm)
    if sink is not None:
        m = jnp.maximum(m, sink)
    probs = []
    denom = None
    for s in scores:
        e = jnp.exp2(s - m)
        es = jnp.sum(e, axis=-1, keepdims=True)
        denom = es if denom is None else denom + es
        probs.append(e.astype(BF16))
    if sink is not None:
        denom = denom + jnp.exp2(sink - m)
    return probs, 1.0 / denom


def _attn_a_body(sink_ref, q_ref, k_ref, v_ref, o_ref, bias_ref, *, seq, tiles_lat):
    qi = pl.program_id(1)
    lane = lax.broadcasted_iota(jnp.int32, (TM, LANES), 1)
    low_half = lane < HEAD_DIM
    span = TM + 2 * WINDOW
    rows = lax.broadcasted_iota(jnp.int32, (A_GROUP * TM, 1), 0)

    def run(parts):
        all_scores = []
        for h in range(A_KV_HEADS):
            qs = []
            for g in range(A_GROUP):
                q2 = q_ref[:, (h * A_GROUP + g) // 2 * LANES:((h * A_GROUP + g) // 2 + 1) * LANES]
                keep = low_half if g % 2 == 0 else jnp.logical_not(low_half)
                qs.append(jnp.where(keep, q2, jnp.zeros_like(q2)))
            qh = jnp.concatenate(qs, axis=0)
            scores = []
            for (r0, nr, _) in parts:
                kk = k_ref[pl.ds(r0, nr), h * LANES:(h + 1) * LANES]
                scores.append(lax.dot_general(qh, kk, NT_DIMS, preferred_element_type=F32))
            all_scores.append(scores)
        for h in range(A_KV_HEADS):
            sink = jnp.zeros((A_GROUP * TM, 1), F32)
            for g in range(A_GROUP):
                sink = jnp.where(rows // TM == g, sink_ref[h * A_GROUP + g] * LOG2E, sink)
            probs, inv = _softmax_parts(all_scores[h], [p[2] for p in parts], sink)
            acc = None
            for p, (r0, nr, _) in zip(probs, parts):
                vv = v_ref[pl.ds(r0, nr), h * LANES:(h + 1) * LANES]
                o = jnp.dot(p, vv, preferred_element_type=F32)
                acc = o if acc is None else acc + o
            acc = acc * inv
            for gp in range(A_GROUP // 2):
                grp = h * (A_GROUP // 2) + gp
                o_ref[:, grp * LANES:(grp + 1) * LANES] = jnp.where(
                    low_half, acc[2 * gp * TM:(2 * gp + 1) * TM], acc[(2 * gp + 1) * TM:(2 * gp + 2) * TM]
                ).astype(BF16)

    @pl.when(qi < tiles_lat)
    def _():
        start = pl.multiple_of(jnp.clip(qi * TM - WINDOW, 0, seq - span), WINDOW)
        qpos = qi * TM + lax.broadcasted_iota(jnp.int32, (TM, span), 0)
        kpos = start + lax.broadcasted_iota(jnp.int32, (TM, span), 1)
        bias = jnp.where(jnp.abs(kpos - qpos) <= WINDOW, 0.0, NEG_INF)
        for g in range(A_GROUP):
            bias_ref[g * TM:(g + 1) * TM, :] = bias
        run([(start, span, bias_ref), (seq, k_ref.shape[0] - seq, None)])

    @pl.when(qi >= tiles_lat)
    def _():
        run([(seq, k_ref.shape[0] - seq, None)])


def _attn_a(qkv, sink, *, n_batch, seq, ctx_len, with_ctx):
    tiles_lat = seq // TM
    tpb = tiles_lat + ctx_len // TM
    rows_b = seq + ctx_len
    q_cols = A_Q_HEADS * HEAD_DIM
    kv_cols = 2 * A_KV_HEADS * HEAD_DIM
    nq = tpb if with_ctx else tiles_lat
    body = functools.partial(_attn_a_body, seq=seq, tiles_lat=tiles_lat)
    return pl.pallas_call(
        body,
        grid=(n_batch, nq),
        in_specs=[
            pl.BlockSpec(memory_space=pltpu.SMEM),
            pl.BlockSpec((TM, q_cols), lambda b, i: (b * tpb + i, 0)),
            pl.BlockSpec((rows_b, kv_cols), lambda b, i: (b, q_cols // kv_cols)),
            pl.BlockSpec((rows_b, kv_cols), lambda b, i: (b, q_cols // kv_cols + 1)),
        ],
        out_specs=pl.BlockSpec((TM, q_cols), lambda b, i: (b * nq + i, 0)),
        out_shape=jax.ShapeDtypeStruct((n_batch * nq * TM, q_cols), BF16),
        scratch_shapes=[pltpu.VMEM((A_GROUP * TM, TM + 2 * WINDOW), F32)],
        compiler_params=_cparams("arbitrary", "arbitrary"),
        name="attn_window",
    )(sink, qkv, qkv, qkv)


def _attn_b_body(lam_ref, g_ref, q_ref, k_ref, v_ref, o_ref, vt_ref, *, seq, tiles_lat, lam_init, heads):
    qi = pl.program_id(2)
    lp = lam_ref[...]
    lam = (jnp.exp(jnp.sum(lp[0:1] * lp[1:2], axis=-1, keepdims=True))
           - jnp.exp(jnp.sum(lp[2:3] * lp[3:4], axis=-1, keepdims=True)) + lam_init)
    lane = lax.broadcasted_iota(jnp.int32, (TM, LANES), 1)
    low_half = lane < HEAD_DIM
    n_keys = k_ref.shape[0]

    @pl.when(qi == 0)
    def _():
        for hh in range(heads):
            vt_ref[hh, :LANES, :] = v_ref[:, hh * LANES:(hh + 1) * LANES].astype(F32).T.astype(BF16)
            row = lax.broadcasted_iota(jnp.int32, (VT_ROWS - LANES, n_keys), 0)
            vt_ref[hh, LANES:, :] = jnp.where(row == 0, 1.0, 0.0).astype(BF16)

    def run(r0, nr):
        scores = []
        for hh in range(heads):
            cols = slice(hh * LANES, (hh + 1) * LANES)
            q = q_ref[:, cols]
            kk = k_ref[pl.ds(r0, nr), cols]
            for qm in (jnp.where(low_half, q, jnp.zeros_like(q)), jnp.where(low_half, jnp.zeros_like(q), q)):
                scores.append(lax.dot_general(kk, qm, NT_DIMS, preferred_element_type=F32))
        for hh in range(heads):
            vt = vt_ref[hh, :, pl.ds(r0, nr)]
            outs = []
            for s in scores[2 * hh:2 * hh + 2]:
                e = jnp.exp2(s - jnp.max(s, axis=0, keepdims=True)).astype(BF16)
                ov = jnp.dot(vt, e, preferred_element_type=F32)
                outs.append((ov[:LANES], 1.0 / ov[LANES:LANES + 1]))
            o = outs[0][0] * outs[0][1] - outs[1][0] * (lam * outs[1][1])
            ms = jnp.mean(o * o, axis=0, keepdims=True)
            o = o * lax.rsqrt(ms + SUBLN_EPS) * g_ref[...] * (1.0 - lam_init)
            o_ref[:, hh * LANES:(hh + 1) * LANES] = o.T.astype(BF16)

    @pl.when(qi < tiles_lat)
    def _():
        run(0, k_ref.shape[0])

    @pl.when(qi >= tiles_lat)
    def _():
        run(seq, k_ref.shape[0] - seq)


def _attn_b(qkv, lam_params, subln_g, *, n_batch, seq, ctx_len, with_ctx, lam_init):
    tiles_lat = seq // TM
    tpb = tiles_lat + ctx_len // TM
    rows_b = seq + ctx_len
    n_heads = qkv.shape[1] // (3 * LANES)
    n_hgrp = n_heads // DIFF_HEADS_PER_STEP
    width = DIFF_HEADS_PER_STEP * LANES
    nq = tpb if with_ctx else tiles_lat
    body = functools.partial(_attn_b_body, seq=seq, tiles_lat=tiles_lat, lam_init=lam_init,
                             heads=DIFF_HEADS_PER_STEP)
    return pl.pallas_call(
        body,
        grid=(n_batch, n_hgrp, nq),
        in_specs=[
            pl.BlockSpec((4, HEAD_DIM), lambda b, h, i: (0, 0)),
            pl.BlockSpec((LANES, 1), lambda b, h, i: (0, 0)),
            pl.BlockSpec((TM, width), lambda b, h, i: (b * tpb + i, h)),
            pl.BlockSpec((rows_b, width), lambda b, h, i: (b, n_hgrp + h)),
            pl.BlockSpec((rows_b, width), lambda b, h, i: (b, 2 * n_hgrp + h)),
        ],
        out_specs=pl.BlockSpec((TM, width), lambda b, h, i: (b * nq + i, h)),
        out_shape=jax.ShapeDtypeStruct((n_batch * nq * TM, n_heads * LANES), BF16),
        compiler_params=_cparams("arbitrary", "arbitrary", "arbitrary"),
        scratch_shapes=[pltpu.VMEM((DIFF_HEADS_PER_STEP, VT_ROWS, rows_b), BF16)],
        name="attn_diff",
    )(lam_params, subln_g.reshape(LANES, 1), qkv, qkv, qkv)


def _route_rows(logits, bias):
    s = jax.nn.sigmoid(logits)
    biased = s + bias
    b = [biased[e:e + 1, :] for e in range(N_EXPERTS)]
    u = [s[e:e + 1, :] for e in range(N_EXPERTS)]
    gscore = []
    for g in range(N_GROUPS):
        best = None
        for (i, j) in PAIRS:
            t = b[GROUP_SIZE * g + i] + b[GROUP_SIZE * g + j]
            best = t if best is None else jnp.maximum(best, t)
        gscore.append(best)
    gsel = jnp.zeros(gscore[0].shape, jnp.int32)
    gbest = gscore[0]
    for g in range(1, N_GROUPS):
        better = gscore[g] > gbest
        gsel = jnp.where(better, g, gsel)
        gbest = jnp.where(better, gscore[g], gbest)

    def pick(rows, k):
        out = rows[GROUP_SIZE * (N_GROUPS - 1) + k]
        for g in range(N_GROUPS - 2, -1, -1):
            out = jnp.where(gsel == g, rows[GROUP_SIZE * g + k], out)
        return out

    v = [pick(b, k) for k in range(GROUP_SIZE)]
    w = [pick(u, k) for k in range(GROUP_SIZE)]
    sel = []
    for k in range(GROUP_SIZE):
        cnt = jnp.zeros(gsel.shape, jnp.int32)
        for j in range(GROUP_SIZE):
            if j == k:
                continue
            beats = (v[j] >= v[k]) if j < k else (v[j] > v[k])
            cnt = cnt + jnp.where(beats, 1, 0)
        sel.append(cnt < 2)
    pidx = jnp.zeros(gsel.shape, jnp.int32)
    u_lo = jnp.zeros(gbest.shape, F32)
    u_hi = jnp.zeros(gbest.shape, F32)
    for idx, (i, j) in enumerate(PAIRS):
        both = jnp.logical_and(sel[i], sel[j])
        pidx = jnp.where(both, idx, pidx)
        u_lo = jnp.where(both, w[i], u_lo)
        u_hi = jnp.where(both, w[j], u_hi)
    tot = u_lo + u_hi
    bucket = (gsel * len(PAIRS) + pidx).astype(F32)
    return bucket, u_lo / tot, u_hi / tot


def _oproj_body(a_ref, wo_ref, x_ref, mod_ref, lng_ref, lnb_ref, wr2_ref, wrh_ref, rb_ref,
                xo_ref, h2p_ref, r_ref, cnt_ref, *, alpha):
    i = pl.program_id(0)
    d = x_ref.shape[1]
    al = jnp.dot(a_ref[...], wo_ref[...], preferred_element_type=F32)
    z = alpha * x_ref[...] + _mod_chunk(mod_ref, 2) * al
    xn = _layer_norm(z, lng_ref[...], lnb_ref[...])
    xo_ref[...] = xn
    h2 = xn * (1.0 + _mod_chunk(mod_ref, 4)) + _mod_chunk(mod_ref, 3)
    hi = h2.astype(BF16)
    lo = (h2 - hi.astype(F32)).astype(BF16)
    l2 = lax.dot_general(wr2_ref[...], hi, NT_DIMS, preferred_element_type=F32)
    l1 = lax.dot_general(wrh_ref[...], lo, NT_DIMS, preferred_element_type=F32)
    logits = l2[:N_EXPERTS] + l2[N_EXPERTS:] + l1
    bucket, w_lo, w_hi = _route_rows(logits, rb_ref[...])

    @pl.when(i == 0)
    def _():
        cnt_ref[...] = jnp.zeros_like(cnt_ref)

    onehot = jnp.where(lax.broadcasted_iota(jnp.int32, (BUCKET_ROWS, TM), 0).astype(F32) == bucket, 1.0, 0.0)
    earlier = (lax.broadcasted_iota(jnp.int32, (TM, TM), 0) < lax.broadcasted_iota(jnp.int32, (TM, TM), 1))
    before = jnp.dot(onehot.astype(BF16), jnp.where(earlier, 1.0, 0.0).astype(BF16),
                     preferred_element_type=F32)
    cnt = cnt_ref[...]
    rank = jnp.sum(onehot * (before + cnt[:, 0:1]), axis=0, keepdims=True)
    cnt_ref[...] = cnt + jnp.sum(onehot, axis=1, keepdims=True)

    r_ref[0:1, :] = bucket
    r_ref[1:2, :] = w_lo
    r_ref[2:3, :] = w_hi
    r_ref[3:4, :] = rank
    r_ref[4:5, :] = (i * TM + lax.broadcasted_iota(jnp.int32, (1, TM), 1)).astype(F32)
    r_ref[5:6, :] = jnp.ones((1, TM), F32)
    r_ref[6:ROUTE_ROWS, :] = jnp.zeros((ROUTE_ROWS - 6, TM), F32)
    rec = jnp.concatenate([r_ref[...], jnp.zeros((LANES - ROUTE_ROWS, TM), F32)], axis=0).T
    h2p_ref[:, :d] = hi.astype(F32)
    h2p_ref[:, d:] = rec


def _oproj(a, wo, x, mod, ln_g, ln_b, wr2, wrh, rbias, *, mode_in, n_batch, tiles_lat, alpha):
    d = x.shape[1]
    compact = mode_in == "lat"
    n_tiles = n_batch * tiles_lat if compact else n_batch * (tiles_lat + 1)
    blk, modrow, _ = _tile_maps(mode_in, tiles_lat, n_batch)
    row_in = pl.BlockSpec((TM, d), lambda i: (blk(i), 0))
    row_out = pl.BlockSpec((TM, d), lambda i: (i, 0))
    vec = pl.BlockSpec((1, d), lambda i: (0, 0))

    modspec = pl.BlockSpec((1, 1, 6 * d), lambda i: (modrow(i), 0, 0))

    body = functools.partial(_oproj_body, alpha=alpha)
    return pl.pallas_call(
        body,
        grid=(n_tiles,),
        in_specs=[row_out, pl.BlockSpec((d, d), lambda i: (0, 0)), row_in, modspec, vec, vec,
                  pl.BlockSpec((2 * N_EXPERTS, d), lambda i: (0, 0)),
                  pl.BlockSpec((N_EXPERTS, d), lambda i: (0, 0)),
                  pl.BlockSpec((N_EXPERTS, 1), lambda i: (0, 0))],
        out_specs=[row_out, pl.BlockSpec((TM, d + LANES), lambda i: (i, 0)),
                   pl.BlockSpec((ROUTE_ROWS, TM), lambda i: (0, i)),
                   pl.BlockSpec((BUCKET_ROWS, LANES), lambda i: (0, 0))],
        out_shape=[jax.ShapeDtypeStruct((n_tiles * TM, d), F32),
                   jax.ShapeDtypeStruct((n_tiles * TM, d + LANES), F32),
                   jax.ShapeDtypeStruct((ROUTE_ROWS, n_tiles * TM), F32),
                   jax.ShapeDtypeStruct((BUCKET_ROWS, LANES), F32)],
        compiler_params=_cparams("arbitrary"),
        name="oproj_ln_route",
    )(a, wo, x, mod, ln_g.reshape(1, d), ln_b.reshape(1, d), wr2, wrh, rbias)


def _scatter_body(pos_ref, tail_ref, h_ref, xs_ref, stage, sem):
    i = pl.program_id(0)
    slot = i % 2

    @pl.when(i == 0)
    def _():
        stage[1, :TM] = jnp.zeros((TM,) + stage.shape[2:], stage.dtype)

        def fill(row):
            return pltpu.make_async_copy(stage.at[1, pl.ds(0, TM)],
                                         xs_ref.at[pl.ds(pl.multiple_of(row, TM), TM)], sem.at[1])

        unused = [tail_ref[N_BUCKETS] + k * TM for k in range(N_BUCKETS)]
        for k in range(N_BUCKETS):
            fill(tail_ref[k]).start()
            pl.when(unused[k] < xs_ref.shape[0])(lambda k=k: fill(unused[k]).start())
        for k in range(N_BUCKETS):
            fill(tail_ref[k]).wait()
            pl.when(unused[k] < xs_ref.shape[0])(lambda k=k: fill(unused[k]).wait())

    @pl.when(i >= 2)
    def _():
        _row_dma_wait(xs_ref, stage.at[slot], sem.at[slot])

    stage[slot] = h_ref[...]

    for r in range(stage.shape[1]):
        pltpu.make_async_copy(stage.at[slot, pl.ds(r, 1)], xs_ref.at[pl.ds(pos_ref[0, 0, r], 1)],
                              sem.at[slot]).start(priority=r % 2)

    @pl.when(i == pl.num_programs(0) - 1)
    def _():
        @pl.when(i >= 1)
        def _():
            _row_dma_wait(xs_ref, stage.at[1 - slot], sem.at[1 - slot])
        _row_dma_wait(xs_ref, stage.at[slot], sem.at[slot])


def _scatter_rows(h2p, pos3, tail_rows, n_rows_sorted):
    rows = SCATTER_TILES * TM
    assert pos3.shape[0] % SCATTER_TILES == 0
    n_steps = pos3.shape[0] // SCATTER_TILES
    width = h2p.shape[1]
    return pl.pallas_call(
        _scatter_body,
        grid=(n_steps,),
        in_specs=[pl.BlockSpec((1, 1, rows), lambda i: (i, 0, 0), memory_space=pltpu.SMEM),
                  pl.BlockSpec(memory_space=pltpu.SMEM),
                  pl.BlockSpec((rows, width), lambda i: (i, 0))],
        out_specs=pl.BlockSpec(memory_space=pl.ANY),
        out_shape=jax.ShapeDtypeStruct((n_rows_sorted, width), h2p.dtype),
        scratch_shapes=[pltpu.VMEM((2, rows, width), h2p.dtype), pltpu.SemaphoreType.DMA((2,))],
        compiler_params=_cparams("arbitrary"),
        name="moe_scatter",
    )(pos3.reshape(n_steps, 1, rows), tail_rows, h2p)


def _moe_body(ea_ref, eb_ref, act_ref, new_ref, par_ref, hasn_ref, nea_ref, neb_ref, xblk_ref,
              x_ref, wg_hbm, wu_hbm, wd_hbm, y_hbm, wg_f, wu_f, wd_f, wgu_s, wd_s, wsem,
              ystage, ids_v, ids_s, ysem, isem, *, d_exp, layer, n_tok):
    i = pl.program_id(0)
    slot = i % 2
    d_model = wd_s.shape[2]
    lane_ids = lax.broadcasted_iota(jnp.int32, (1, TM), 1)

    def ids_copy(s):
        return pltpu.make_async_copy(ids_v.at[s, pl.ds(0, 1)], ids_s.at[pl.ds(s, 1)], isem.at[s])

    def scatter_tile(s):
        ids_copy(s).wait()
        for r in range(TM):
            pltpu.make_async_copy(ystage.at[s, pl.ds(r, 1)], y_hbm.at[pl.ds(ids_s[s, r], 1)],
                                  ysem.at[s]).start(priority=r % 2)

    def scatter_wait(s):
        _row_dma_wait(y_hbm, ystage.at[s], ysem.at[s])

    @pl.when(i == 0)
    def _():
        ystage[1] = jnp.zeros(ystage.shape[1:], ystage.dtype)
        ids_v[1, 0:1, :] = n_tok + TM + lane_ids
        ids_copy(1).start()
        fill = pltpu.make_async_copy(ystage.at[1], y_hbm.at[pl.ds(n_tok, TM)], ysem.at[0])
        fill.start()
        fill.wait()

    def weight_copies(slot, experts):
        return [pltpu.make_async_copy(hbm.at[layer, ex], buf.at[slot, e], wsem.at[slot])
                for e, ex in enumerate(experts) for hbm, buf in ((wg_hbm, wg_f), (wu_hbm, wu_f), (wd_hbm, wd_f))]

    @pl.when(i == 0)
    def _():
        for cp in weight_copies(par_ref[0], (ea_ref[0], eb_ref[0])):
            cp.start()

    @pl.when(new_ref[i] > 0)
    def _():
        slot = par_ref[i]
        for cp in weight_copies(slot, (ea_ref[i], eb_ref[i])):
            cp.wait()

        @pl.when(hasn_ref[i] > 0)
        def _():
            for cp in weight_copies(1 - slot, (nea_ref[i], neb_ref[i])):
                cp.start()

        for e in range(2):
            wgu_s[e, :, :d_exp] = wg_f[slot, e].astype(BF16)
            wgu_s[e, :, d_exp:] = wu_f[slot, e].astype(BF16)
            wd_s[e] = wd_f[slot, e].astype(BF16)

    active = act_ref[i] > 0
    prev_active = jnp.logical_and(i > 0, act_ref[jnp.maximum(i - 1, 0)] > 0)

    @pl.when(active)
    def _():
        @pl.when(i >= 1)
        def _():
            scatter_wait(slot)

        scatter_tile(1 - slot)
        xw = x_ref[...]
        x = xw[:, :d_model].astype(BF16)
        rec = xw[:, d_model:]

        def expert(e):
            gu = jnp.dot(x, wgu_s[e], preferred_element_type=F32)
            gate = gu[:, :d_exp]
            h = gate * jax.nn.sigmoid(gate) * gu[:, d_exp:] * rec[:, 1 + e:2 + e]
            return jnp.dot(h.astype(BF16), wd_s[e], preferred_element_type=F32)

        ystage[slot] = expert(0) + expert(1)
        rec_t = rec.T
        dump = n_tok + slot * TM + lane_ids
        ids_v[slot, 0:1, :] = jnp.where(rec_t[5:6] > 0.0, rec_t[4:5].astype(jnp.int32), dump)
        ids_copy(slot).start()

    @pl.when(jnp.logical_and(jnp.logical_not(active), prev_active))
    def _():
        scatter_tile(1 - slot)
        scatter_wait(1 - slot)
        scatter_wait(slot)


def _moe(xs, w_gate, w_up, w_down, layer, sched, n_tok):
    n_rows, width = xs.shape
    d_exp, d = w_down.shape[2:]
    n_tiles = n_rows // TM
    body = functools.partial(_moe_body, d_exp=d_exp, layer=layer, n_tok=n_tok)
    grid_spec = pltpu.PrefetchScalarGridSpec(
        num_scalar_prefetch=len(sched),
        grid=(n_tiles,),
        in_specs=[pl.BlockSpec((TM, width), lambda i, *s: (s[-1][i], 0)),
                  pl.BlockSpec(memory_space=pl.ANY), pl.BlockSpec(memory_space=pl.ANY),
                  pl.BlockSpec(memory_space=pl.ANY)],
        out_specs=pl.BlockSpec(memory_space=pl.ANY),
        scratch_shapes=[pltpu.VMEM((2, 2, d, d_exp), F32), pltpu.VMEM((2, 2, d, d_exp), F32),
                        pltpu.VMEM((2, 2, d_exp, d), F32),
                        pltpu.VMEM((2, d, 2 * d_exp), BF16), pltpu.VMEM((2, d_exp, d), BF16),
                        pltpu.SemaphoreType.DMA((2,)),
                        pltpu.VMEM((2, TM, d), F32), pltpu.VMEM((2, 8, TM), jnp.int32),
                        pltpu.SMEM((2, TM), jnp.int32),
                        pltpu.SemaphoreType.DMA((2,)), pltpu.SemaphoreType.DMA((2,))],
    )
    return pl.pallas_call(
        body, grid_spec=grid_spec,
        out_shape=jax.ShapeDtypeStruct((n_tok + 2 * TM, d), F32),
        compiler_params=_cparams("arbitrary"),
        name="moe_pairs",
    )(*sched, xs, w_gate, w_up, w_down)


def _dispatch(route, counts, n_tiles):
    n = route.shape[1]
    bucket = route[0].astype(jnp.int32)
    rank = route[3].astype(jnp.int32)
    counts = counts[:N_BUCKETS, 0].astype(jnp.int32)
    padded = ((counts + TM - 1) // TM) * TM
    ends = jnp.cumsum(padded)
    off = ends - padded

    def lookup(idx, table):
        return jnp.sum(jnp.where(idx[:, None] == jnp.arange(table.shape[0])[None, :], table[None, :], 0), axis=1)

    pos = lookup(bucket, off) + rank
    tile_start = jnp.arange(n_tiles, dtype=jnp.int32) * TM
    active = tile_start < ends[-1]
    last_start = jnp.maximum(ends[-1] - TM, 0)
    tb = jnp.sum((jnp.where(active, tile_start, last_start)[:, None] >= ends[None, :]).astype(jnp.int32), axis=1)
    tb = jnp.minimum(tb, N_BUCKETS - 1)
    ea = (tb // len(PAIRS)) * GROUP_SIZE + lookup(tb % len(PAIRS), jnp.array([p[0] for p in PAIRS], jnp.int32))
    eb = (tb // len(PAIRS)) * GROUP_SIZE + lookup(tb % len(PAIRS), jnp.array([p[1] for p in PAIRS], jnp.int32))
    tiles = jnp.arange(n_tiles, dtype=jnp.int32)
    new = jnp.concatenate([jnp.ones((1,), jnp.int32), (tb[1:] != tb[:-1]).astype(jnp.int32)])
    first_at = jnp.where(new > 0, tiles, n_tiles)
    nxt = jnp.concatenate([lax.cummin(first_at, reverse=True)[1:], jnp.full((1,), n_tiles, jnp.int32)])
    has_next = (nxt < n_tiles).astype(jnp.int32)
    nxt = jnp.minimum(nxt, n_tiles - 1)
    parity = (jnp.cumsum(new) - 1) % 2
    last_used = jnp.maximum(ends[-1] // TM - 1, 0)
    sched = (ea, eb, active.astype(jnp.int32), new, parity.astype(jnp.int32), has_next,
             lookup(nxt, ea), lookup(nxt, eb), jnp.minimum(tiles, last_used))
    tail_rows = jnp.concatenate([jnp.where(padded > 0, ends - TM, 0), ends[-1:]]).astype(jnp.int32)
    return pos.reshape(n // TM, 1, TM), tail_rows, sched


def _rope_tables(seq, ctx_len):
    t = jnp.arange(seq)
    quarter = HEAD_DIM // 4
    inv_freq = ROPE_THETA ** (-jnp.arange(quarter, dtype=F32) / quarter)
    ang_r = (t // GRID_W).astype(F32)[:, None] * inv_freq
    ang_c = (t % GRID_W).astype(F32)[:, None] * inv_freq
    cos_h = jnp.concatenate([jnp.cos(ang_r)] * 2 + [jnp.cos(ang_c)] * 2, axis=-1)
    sin_h = jnp.concatenate([-jnp.sin(ang_r), jnp.sin(ang_r), -jnp.sin(ang_c), jnp.sin(ang_c)], axis=-1)
    cos_t = jnp.concatenate([cos_h, cos_h], axis=-1)
    sin_t = jnp.concatenate([sin_h, sin_h], axis=-1)
    cos_t = jnp.concatenate([cos_t, jnp.ones((ctx_len, LANES), F32)], axis=0)
    sin_t = jnp.concatenate([sin_t, jnp.zeros((ctx_len, LANES), F32)], axis=0)
    return cos_t, sin_t


def kernel(x, c, ctx, c_ctx, w_ada, b_ada, wqkv_a, wo_a, sink_a, wqkv_b, wo_b, lambda_b, subln_b, ln_attn_g, ln_attn_b, ln_ffn_g, ln_ffn_b, w_router, router_bias, w_gate, w_up, w_down):
    n_batch, seq, d = x.shape
    ctx_len = ctx.shape[1]
    depth = w_ada.shape[0]
    assert seq % TM == 0 and ctx_len == TM and seq >= TM + 2 * WINDOW
    assert n_batch + 1 <= ADA_ROWS and d == A_Q_HEADS * HEAD_DIM
    tiles_lat = seq // TM
    alpha = (2 * depth) ** 0.25
    q_cols = A_Q_HEADS * HEAD_DIM
    kv_cols = A_KV_HEADS * HEAD_DIM

    cc = jnp.concatenate([c, c_ctx[None, :], jnp.zeros((ADA_ROWS - n_batch - 1, d), F32)], axis=0)
    mods = _ada_modulation(cc, w_ada, b_ada).reshape(depth, ADA_ROWS, 1, 6 * d)
    cos_t, sin_t = _rope_tables(seq, ctx_len)

    xs = jnp.concatenate([x, ctx], axis=1).reshape(n_batch * (seq + ctx_len), d)
    wr_hi = w_router.astype(BF16)
    wr_lo = (w_router - wr_hi.astype(F32)).astype(BF16)
    wr2 = jnp.concatenate([wr_hi.T, wr_lo.T], axis=0)
    wrh = wr_hi.T
    rbias = router_bias.reshape(N_EXPERTS, 1)

    y = None
    mode = "all"
    for i in range(depth):
        last = i == depth - 1
        j = i // 2
        if i % 2 == 0:
            w = wqkv_a[j].astype(BF16)
            n_rope, dup_from = q_cols + kv_cols, q_cols
        else:
            w = wqkv_b[j].astype(BF16)
            n_rope, dup_from = 2 * q_cols, w.shape[1]
        outs = _lnqkv(xs, y, mods[i - 1] if i else None, ln_ffn_g[i - 1] if i else None,
                      ln_ffn_b[i - 1] if i else None, mods[i], w, cos_t, sin_t,
                      mode_in="all", n_batch=n_batch, tiles_lat=tiles_lat,
                      alpha=alpha, n_rope=n_rope, dup_from=dup_from)
        if i:
            xs, qkv = outs
        else:
            (qkv,) = outs
        if i % 2 == 0:
            att = _attn_a(qkv, sink_a[j], n_batch=n_batch, seq=seq, ctx_len=ctx_len, with_ctx=not last)
            wo = wo_a[j].astype(BF16)
        else:
            lam_init = 0.8 - 0.6 * math.exp(-0.3 * i)
            att = _attn_b(qkv, lambda_b[j], subln_b[j], n_batch=n_batch, seq=seq, ctx_len=ctx_len,
                          with_ctx=not last, lam_init=lam_init)
            wo = wo_b[j].astype(BF16)
        mode = "lat" if last else "all"
        xs, h2p, route, counts = _oproj(att, wo, xs, mods[i], ln_attn_g[i], ln_attn_b[i], wr2, wrh, rbias,
                                        mode_in=mode, n_batch=n_batch, tiles_lat=tiles_lat, alpha=alpha)
        n_tiles = h2p.shape[0] // TM + N_BUCKETS
        pos3, tail_rows, sched = _dispatch(route, counts, n_tiles)
        x_sorted = _scatter_rows(h2p, pos3, tail_rows, n_tiles * TM)
        y = _moe(x_sorted, w_gate, w_up, w_down, i, sched, h2p.shape[0])

    (out,) = _lnqkv(xs, y, mods[depth - 1], ln_ffn_g[depth - 1], ln_ffn_b[depth - 1], None, None, None, None,
                    mode_in="compact", n_batch=n_batch, tiles_lat=tiles_lat, alpha=alpha)
    return out.reshape(n_batch, seq, d)
```

```python
import functools
import math

import jax
import jax.numpy as jnp
from jax import lax
from jax.experimental import pallas as pl
from jax.experimental.pallas import tpu as pltpu

F32 = jnp.float32
BF16 = jnp.bfloat16

HEAD_DIM = 64
A_Q_HEADS = 16
A_KV_HEADS = 4
A_GROUP = A_Q_HEADS // A_KV_HEADS
GRID_W = 64
WINDOW = 128
N_EXPERTS = 16
N_GROUPS = 4
GROUP_SIZE = N_EXPERTS // N_GROUPS
ROPE_THETA = 10000.0
LN_EPS = 1e-6
SUBLN_EPS = 1e-5
NEG_INF = -1e30
ATTN_SCALE = HEAD_DIM ** -0.5
LOG2E = math.log2(math.e)
VT_ROWS = 144

LANES = 128
TM = 256
DIFF_HEADS_PER_STEP = 4
SCATTER_TILES = (1, 2, 4)
ADA_ROWS = 24
VMEM_LIMIT = 56 * 1024 * 1024

PAIRS = ((0, 1), (0, 2), (0, 3), (1, 2), (1, 3), (2, 3))
N_BUCKETS = N_GROUPS * len(PAIRS)

BUCKET_ROWS = 32
ROUTE_ROWS = 8

NT_DIMS = (((1,), (1,)), ((), ()))


def _cparams(*sem):
    return pltpu.CompilerParams(dimension_semantics=sem, vmem_limit_bytes=VMEM_LIMIT)


def _ada_body(c_ref, w_ref, b_ref, o_ref):
    c = c_ref[...]
    sc = c * jax.nn.sigmoid(c)
    o_ref[0] = jnp.dot(sc, w_ref[0], precision=lax.Precision.HIGHEST,
                       preferred_element_type=F32) + b_ref[0]


def _ada_modulation(cc, w_ada, b_ada):
    depth, d, d6 = w_ada.shape
    tn = 1536
    return pl.pallas_call(
        _ada_body,
        grid=(depth, d6 // tn),
        in_specs=[
            pl.BlockSpec((ADA_ROWS, d), lambda l, n: (0, 0)),
            pl.BlockSpec((1, d, tn), lambda l, n: (l, 0, n)),
            pl.BlockSpec((1, 1, tn), lambda l, n: (l, 0, n)),
        ],
        out_specs=pl.BlockSpec((1, ADA_ROWS, tn), lambda l, n: (l, 0, n)),
        out_shape=jax.ShapeDtypeStruct((depth, ADA_ROWS, d6), F32),
        compiler_params=_cparams("arbitrary", "arbitrary"),
        name="ada_mod",
    )(cc, w_ada, b_ada.reshape(depth, 1, d6))


def _tile_maps(mode, tiles_lat, n_batch):
    tpb = tiles_lat + 1
    if mode == "all":
        blk = lambda i: i
        modrow = lambda i: jnp.where(i % tpb == tiles_lat, n_batch, i // tpb)
        pos = lambda i: i % tpb
    elif mode == "lat":
        blk = lambda i: (i // tiles_lat) * tpb + i % tiles_lat
        modrow = lambda i: i // tiles_lat
        pos = lambda i: i % tiles_lat
    else:
        blk = lambda i: i
        modrow = lambda i: i // tiles_lat
        pos = lambda i: i % tiles_lat
    return blk, modrow, pos


def _mod_chunk(mod_ref, k):
    d = mod_ref.shape[2] // 6
    return mod_ref[0, :, k * d:(k + 1) * d]


def _layer_norm(z, g, b):
    mu = jnp.mean(z, axis=-1, keepdims=True)
    zc = z - mu
    var = jnp.mean(zc * zc, axis=-1, keepdims=True)
    return zc * lax.rsqrt(var + LN_EPS) * g + b


def _row_dma_wait(hbm_ref, buf_ref, sem):
    pltpu.make_async_copy(hbm_ref.at[pl.ds(0, buf_ref.shape[0])], buf_ref, sem).wait()


def _lnqkv_body(*refs, has_ln, has_qkv, alpha, n_rope, dup_from, n_out):
    refs = list(refs)
    x_ref = refs.pop(0)
    if has_ln:
        y_ref, g2_ref, lng_ref, lnb_ref = refs[:4]
        refs = refs[4:]
    if has_qkv:
        mod_ref, w_ref, cos_ref, sin_ref = refs[:4]
        refs = refs[4:]
    x = x_ref[...]
    if has_ln:
        xo_ref = refs.pop(0)
        z = alpha * x + _mod_chunk(g2_ref, 5) * y_ref[...]
        x = _layer_norm(z, lng_ref[...], lnb_ref[...])
        xo_ref[...] = x
    if not has_qkv:
        return
    qkv_ref = refs.pop(0)
    h = (x * (1.0 + _mod_chunk(mod_ref, 1)) + _mod_chunk(mod_ref, 0)).astype(BF16)
    r = jnp.dot(h, w_ref[...], preferred_element_type=F32)
    cos = cos_ref[...]
    sin = sin_ref[...]
    lane = lax.broadcasted_iota(jnp.int32, (TM, LANES), 1)
    first16 = (lane % 32) < 16
    low_half = lane < HEAD_DIM
    q_cols = A_Q_HEADS * HEAD_DIM
    dst = 0
    for c in range(n_out // LANES):
        seg = r[:, c * LANES:(c + 1) * LANES]
        col = c * LANES
        if col < q_cols:
            seg = seg * (ATTN_SCALE * LOG2E)
        if col < n_rope:
            rot = jnp.where(first16, pltpu.roll(seg, LANES - 16, 1), pltpu.roll(seg, 16, 1))
            seg = seg * cos + rot * sin
        if col >= dup_from:
            swapped = pltpu.roll(seg, HEAD_DIM, 1)
            qkv_ref[:, dst:dst + LANES] = jnp.where(low_half, seg, swapped).astype(BF16)
            qkv_ref[:, dst + LANES:dst + 2 * LANES] = jnp.where(low_half, swapped, seg).astype(BF16)
            dst += 2 * LANES
        else:
            qkv_ref[:, dst:dst + LANES] = seg.astype(BF16)
            dst += LANES


def _lnqkv(x, y, mod_ln, ln_g, ln_b, mod_qkv, w, cos_t, sin_t, *, mode_in, n_batch, tiles_lat,
           alpha, n_rope=0, dup_from=0):
    d = x.shape[1]
    has_ln = y is not None
    has_qkv = w is not None
    n_tiles = (n_batch * (tiles_lat + 1)) if mode_in == "all" else n_batch * tiles_lat
    blk, modrow, pos = _tile_maps(mode_in, tiles_lat, n_batch)
    row = pl.BlockSpec((TM, d), lambda i: (blk(i), 0))
    vec = pl.BlockSpec((1, d), lambda i: (0, 0))

    modspec = pl.BlockSpec((1, 1, 6 * d), lambda i: (modrow(i), 0, 0))

    args, in_specs, out_specs, out_shape, scratch = [x], [row], [], [], []
    n_out = 0
    if has_ln:
        args += [y, mod_ln, ln_g.reshape(1, d), ln_b.reshape(1, d)]
        in_specs += [pl.BlockSpec((TM, y.shape[1]), lambda i: (blk(i), 0)), modspec, vec, vec]
        out_specs.append(row)
        out_shape.append(jax.ShapeDtypeStruct(x.shape, F32))
    if has_qkv:
        n_out = w.shape[1]
        n_store = n_out + (n_out - dup_from)
        args += [mod_qkv, w, cos_t, sin_t]
        in_specs += [modspec,
                     pl.BlockSpec((d, n_out), lambda i: (0, 0)),
                     pl.BlockSpec((TM, LANES), lambda i: (pos(i), 0)),
                     pl.BlockSpec((TM, LANES), lambda i: (pos(i), 0))]
        out_specs.append(pl.BlockSpec((TM, n_store), lambda i: (blk(i), 0)))
        out_shape.append(jax.ShapeDtypeStruct((x.shape[0], n_store), BF16))
    body = functools.partial(_lnqkv_body, has_ln=has_ln, has_qkv=has_qkv, alpha=alpha,
                             n_rope=n_rope, dup_from=dup_from, n_out=n_out)
    return pl.pallas_call(
        body, grid=(n_tiles,), in_specs=in_specs, out_specs=out_specs, out_shape=out_shape,
        scratch_shapes=scratch, compiler_params=_cparams("arbitrary"),
        name=("ln_" if has_ln else "") + ("qkv" if has_qkv else "out"),
    )(*args)


def _softmax_parts(scores, biases, sink):
    scores = [s if bias is None else s + bias[...] for s, bias in zip(scores, biases)]
    m = None
    for s in scores:
        sm = jnp.max(s, axis=-1, keepdims=True)
        m = sm if m is None else jnp.maximum(m, sm)
    if sink is not None:
        m = jnp.maximum(m, sink)
    probs = []
    denom = None
    for s in scores:
        e = jnp.exp2(s - m)
        es = jnp.sum(e, axis=-1, keepdims=True)
        denom = es if denom is None else denom + es
        probs.append(e.astype(BF16))
    if sink is not None:
        denom = denom + jnp.exp2(sink - m)
    return probs, 1.0 / denom


def _attn_a_body(sink_ref, q_ref, k_ref, v_ref, o_ref, bias_ref, *, seq, tiles_lat):
    qi = pl.program_id(1)
    lane = lax.broadcasted_iota(jnp.int32, (TM, LANES), 1)
    low_half = lane < HEAD_DIM
    span = TM + 2 * WINDOW
    rows = lax.broadcasted_iota(jnp.int32, (A_GROUP * TM, 1), 0)

    def run(parts):
        all_scores = []
        for h in range(A_KV_HEADS):
            qs = []
            for g in range(A_GROUP):
                q2 = q_ref[:, (h * A_GROUP + g) // 2 * LANES:((h * A_GROUP + g) // 2 + 1) * LANES]
                keep = low_half if g % 2 == 0 else jnp.logical_not(low_half)
                qs.append(jnp.where(keep, q2, jnp.zeros_like(q2)))
            qh = jnp.concatenate(qs, axis=0)
            scores = []
            for (r0, nr, _) in parts:
                kk = k_ref[pl.ds(r0, nr), h * LANES:(h + 1) * LANES]
                scores.append(lax.dot_general(qh, kk, NT_DIMS, preferred_element_type=F32))
            all_scores.append(scores)
        for h in range(A_KV_HEADS):
            sink = jnp.zeros((A_GROUP * TM, 1), F32)
            for g in range(A_GROUP):
                sink = jnp.where(rows // TM == g, sink_ref[h * A_GROUP + g] * LOG2E, sink)
            probs, inv = _softmax_parts(all_scores[h], [p[2] for p in parts], sink)
            acc = None
            for p, (r0, nr, _) in zip(probs, parts):
                vv = v_ref[pl.ds(r0, nr), h * LANES:(h + 1) * LANES]
                o = jnp.dot(p, vv, preferred_element_type=F32)
                acc = o if acc is None else acc + o
            acc = acc * inv
            for gp in range(A_GROUP // 2):
                grp = h * (A_GROUP // 2) + gp
                o_ref[:, grp * LANES:(grp + 1) * LANES] = jnp.where(
                    low_half, acc[2 * gp * TM:(2 * gp + 1) * TM], acc[(2 * gp + 1) * TM:(2 * gp + 2) * TM]
                ).astype(BF16)

    @pl.when(qi < tiles_lat)
    def _():
        start = pl.multiple_of(jnp.clip(qi * TM - WINDOW, 0, seq - span), WINDOW)
        qpos = qi * TM + lax.broadcasted_iota(jnp.int32, (TM, span), 0)
        kpos = start + lax.broadcasted_iota(jnp.int32, (TM, span), 1)
        bias = jnp.where(jnp.abs(kpos - qpos) <= WINDOW, 0.0, NEG_INF)
        for g in range(A_GROUP):
            bias_ref[g * TM:(g + 1) * TM, :] = bias
        run([(start, span, bias_ref), (seq, k_ref.shape[0] - seq, None)])

    @pl.when(qi >= tiles_lat)
    def _():
        run([(seq, k_ref.shape[0] - seq, None)])


def _attn_a(qkv, sink, *, n_batch, seq, ctx_len, with_ctx):
    tiles_lat = seq // TM
    tpb = tiles_lat + ctx_len // TM
    rows_b = seq + ctx_len
    q_cols = A_Q_HEADS * HEAD_DIM
    kv_cols = 2 * A_KV_HEADS * HEAD_DIM
    nq = tpb if with_ctx else tiles_lat
    body = functools.partial(_attn_a_body, seq=seq, tiles_lat=tiles_lat)
    return pl.pallas_call(
        body,
        grid=(n_batch, nq),
        in_specs=[
            pl.BlockSpec(memory_space=pltpu.SMEM),
            pl.BlockSpec((TM, q_cols), lambda b, i: (b * tpb + i, 0)),
            pl.BlockSpec((rows_b, kv_cols), lambda b, i: (b, q_cols // kv_cols)),
            pl.BlockSpec((rows_b, kv_cols), lambda b, i: (b, q_cols // kv_cols + 1)),
        ],
        out_specs=pl.BlockSpec((TM, q_cols), lambda b, i: (b * nq + i, 0)),
        out_shape=jax.ShapeDtypeStruct((n_batch * nq * TM, q_cols), BF16),
        scratch_shapes=[pltpu.VMEM((A_GROUP * TM, TM + 2 * WINDOW), F32)],
        compiler_params=_cparams("arbitrary", "arbitrary"),
        name="attn_window",
    )(sink, qkv, qkv, qkv)


def _attn_b_body(lam_ref, g_ref, q_ref, k_ref, v_ref, o_ref, vt_ref, *, seq, tiles_lat, lam_init, heads):
    qi = pl.program_id(2)
    lp = lam_ref[...]
    lam = (jnp.exp(jnp.sum(lp[0:1] * lp[1:2], axis=-1, keepdims=True))
           - jnp.exp(jnp.sum(lp[2:3] * lp[3:4], axis=-1, keepdims=True)) + lam_init)
    lane = lax.broadcasted_iota(jnp.int32, (TM, LANES), 1)
    low_half = lane < HEAD_DIM
    n_keys = k_ref.shape[0]

    @pl.when(qi == 0)
    def _():
        for hh in range(heads):
            vt_ref[hh, :LANES, :] = v_ref[:, hh * LANES:(hh + 1) * LANES].astype(F32).T.astype(BF16)
            row = lax.broadcasted_iota(jnp.int32, (VT_ROWS - LANES, n_keys), 0)
            vt_ref[hh, LANES:, :] = jnp.where(row == 0, 1.0, 0.0).astype(BF16)

    def run(r0, nr):
        scores = []
        for hh in range(heads):
            cols = slice(hh * LANES, (hh + 1) * LANES)
            q = q_ref[:, cols]
            kk = k_ref[pl.ds(r0, nr), cols]
            for qm in (jnp.where(low_half, q, jnp.zeros_like(q)), jnp.where(low_half, jnp.zeros_like(q), q)):
                scores.append(lax.dot_general(kk, qm, NT_DIMS, preferred_element_type=F32))
        for hh in range(heads):
            vt = vt_ref[hh, :, pl.ds(r0, nr)]
            outs = []
            for s in scores[2 * hh:2 * hh + 2]:
                e = jnp.exp2(s - jnp.max(s, axis=0, keepdims=True)).astype(BF16)
                ov = jnp.dot(vt, e, preferred_element_type=F32)
                outs.append((ov[:LANES], 1.0 / ov[LANES:LANES + 1]))
            o = outs[0][0] * outs[0][1] - outs[1][0] * (lam * outs[1][1])
            ms = jnp.mean(o * o, axis=0, keepdims=True)
            o = o * lax.rsqrt(ms + SUBLN_EPS) * g_ref[...] * (1.0 - lam_init)
            o_ref[:, hh * LANES:(hh + 1) * LANES] = o.T.astype(BF16)

    @pl.when(qi < tiles_lat)
    def _():
        run(0, k_ref.shape[0])

    @pl.when(qi >= tiles_lat)
    def _():
        run(seq, k_ref.shape[0] - seq)


def _attn_b(qkv, lam_params, subln_g, *, n_batch, seq, ctx_len, with_ctx, lam_init):
    tiles_lat = seq // TM
    tpb = tiles_lat + ctx_len // TM
    rows_b = seq + ctx_len
    n_heads = qkv.shape[1] // (3 * LANES)
    n_hgrp = n_heads // DIFF_HEADS_PER_STEP
    width = DIFF_HEADS_PER_STEP * LANES
    nq = tpb if with_ctx else tiles_lat
    body = functools.partial(_attn_b_body, seq=seq, tiles_lat=tiles_lat, lam_init=lam_init,
                             heads=DIFF_HEADS_PER_STEP)
    return pl.pallas_call(
        body,
        grid=(n_batch, n_hgrp, nq),
        in_specs=[
            pl.BlockSpec((4, HEAD_DIM), lambda b, h, i: (0, 0)),
            pl.BlockSpec((LANES, 1), lambda b, h, i: (0, 0)),
            pl.BlockSpec((TM, width), lambda b, h, i: (b * tpb + i, h)),
            pl.BlockSpec((rows_b, width), lambda b, h, i: (b, n_hgrp + h)),
            pl.BlockSpec((rows_b, width), lambda b, h, i: (b, 2 * n_hgrp + h)),
        ],
        out_specs=pl.BlockSpec((TM, width), lambda b, h, i: (b * nq + i, h)),
        out_shape=jax.ShapeDtypeStruct((n_batch * nq * TM, n_heads * LANES), BF16),
        compiler_params=_cparams("arbitrary", "arbitrary", "arbitrary"),
        scratch_shapes=[pltpu.VMEM((DIFF_HEADS_PER_STEP, VT_ROWS, rows_b), BF16)],
        name="attn_diff",
    )(lam_params, subln_g.reshape(LANES, 1), qkv, qkv, qkv)


def _route_rows(logits, bias):
    s = jax.nn.sigmoid(logits)
    biased = s + bias
    b = [biased[e:e + 1, :] for e in range(N_EXPERTS)]
    u = [s[e:e + 1, :] for e in range(N_EXPERTS)]
    gscore = []
    for g in range(N_GROUPS):
        best = None
        for (i, j) in PAIRS:
            t = b[GROUP_SIZE * g + i] + b[GROUP_SIZE * g + j]
            best = t if best is None else jnp.maximum(best, t)
        gscore.append(best)
    gsel = jnp.zeros(gscore[0].shape, jnp.int32)
    gbest = gscore[0]
    for g in range(1, N_GROUPS):
        better = gscore[g] > gbest
        gsel = jnp.where(better, g, gsel)
        gbest = jnp.where(better, gscore[g], gbest)

    def pick(rows, k):
        out = rows[GROUP_SIZE * (N_GROUPS - 1) + k]
        for g in range(N_GROUPS - 2, -1, -1):
            out = jnp.where(gsel == g, rows[GROUP_SIZE * g + k], out)
        return out

    v = [pick(b, k) for k in range(GROUP_SIZE)]
    w = [pick(u, k) for k in range(GROUP_SIZE)]
    sel = []
    for k in range(GROUP_SIZE):
        cnt = jnp.zeros(gsel.shape, jnp.int32)
        for j in range(GROUP_SIZE):
            if j == k:
                continue
            beats = (v[j] >= v[k]) if j < k else (v[j] > v[k])
            cnt = cnt + jnp.where(beats, 1, 0)
        sel.append(cnt < 2)
    pidx = jnp.zeros(gsel.shape, jnp.int32)
    u_lo = jnp.zeros(gbest.shape, F32)
    u_hi = jnp.zeros(gbest.shape, F32)
    for idx, (i, j) in enumerate(PAIRS):
        both = jnp.logical_and(sel[i], sel[j])
        pidx = jnp.where(both, idx, pidx)
        u_lo = jnp.where(both, w[i], u_lo)
        u_hi = jnp.where(both, w[j], u_hi)
    tot = u_lo + u_hi
    bucket = (gsel * len(PAIRS) + pidx).astype(F32)
    return bucket, u_lo / tot, u_hi / tot


def _oproj_body(a_ref, wo_ref, x_ref, mod_ref, lng_ref, lnb_ref, wr2_ref, wrh_ref, rb_ref,
                xo_ref, h2p_ref, r_ref, cnt_ref, *, alpha):
    i = pl.program_id(0)
    d = x_ref.shape[1]
    al = jnp.dot(a_ref[...], wo_ref[...], preferred_element_type=F32)
    z = alpha * x_ref[...] + _mod_chunk(mod_ref, 2) * al
    xn = _layer_norm(z, lng_ref[...], lnb_ref[...])
    xo_ref[...] = xn
    h2 = xn * (1.0 + _mod_chunk(mod_ref, 4)) + _mod_chunk(mod_ref, 3)
    hi = h2.astype(BF16)
    lo = (h2 - hi.astype(F32)).astype(BF16)
    l2 = lax.dot_general(wr2_ref[...], hi, NT_DIMS, preferred_element_type=F32)
    l1 = lax.dot_general(wrh_ref[...], lo, NT_DIMS, preferred_element_type=F32)
    logits = l2[:N_EXPERTS] + l2[N_EXPERTS:] + l1
    bucket, w_lo, w_hi = _route_rows(logits, rb_ref[...])

    @pl.when(i == 0)
    def _():
        cnt_ref[...] = jnp.zeros_like(cnt_ref)

    onehot = jnp.where(lax.broadcasted_iota(jnp.int32, (BUCKET_ROWS, TM), 0).astype(F32) == bucket, 1.0, 0.0)
    earlier = (lax.broadcasted_iota(jnp.int32, (TM, TM), 0) < lax.broadcasted_iota(jnp.int32, (TM, TM), 1))
    before = jnp.dot(onehot.astype(BF16), jnp.where(earlier, 1.0, 0.0).astype(BF16),
                     preferred_element_type=F32)
    cnt = cnt_ref[...]
    rank = jnp.sum(onehot * (before + cnt[:, 0:1]), axis=0, keepdims=True)
    cnt_ref[...] = cnt + jnp.sum(onehot, axis=1, keepdims=True)

    r_ref[0:1, :] = bucket
    r_ref[1:2, :] = w_lo
    r_ref[2:3, :] = w_hi
    r_ref[3:4, :] = rank
    r_ref[4:5, :] = (i * TM + lax.broadcasted_iota(jnp.int32, (1, TM), 1)).astype(F32)
    r_ref[5:6, :] = jnp.ones((1, TM), F32)
    r_ref[6:ROUTE_ROWS, :] = jnp.zeros((ROUTE_ROWS - 6, TM), F32)
    rec = jnp.concatenate([r_ref[...], jnp.zeros((LANES - ROUTE_ROWS, TM), F32)], axis=0).T
    h2p_ref[:, :d] = hi.astype(F32)
    h2p_ref[:, d:] = rec


def _oproj(a, wo, x, mod, ln_g, ln_b, wr2, wrh, rbias, *, mode_in, n_batch, tiles_lat, alpha):
    d = x.shape[1]
    compact = mode_in == "lat"
    n_tiles = n_batch * tiles_lat if compact else n_batch * (tiles_lat + 1)
    blk, modrow, _ = _tile_maps(mode_in, tiles_lat, n_batch)
    row_in = pl.BlockSpec((TM, d), lambda i: (blk(i), 0))
    row_out = pl.BlockSpec((TM, d), lambda i: (i, 0))
    vec = pl.BlockSpec((1, d), lambda i: (0, 0))

    modspec = pl.BlockSpec((1, 1, 6 * d), lambda i: (modrow(i), 0, 0))

    body = functools.partial(_oproj_body, alpha=alpha)
    return pl.pallas_call(
        body,
        grid=(n_tiles,),
        in_specs=[row_out, pl.BlockSpec((d, d), lambda i: (0, 0)), row_in, modspec, vec, vec,
                  pl.BlockSpec((2 * N_EXPERTS, d), lambda i: (0, 0)),
                  pl.BlockSpec((N_EXPERTS, d), lambda i: (0, 0)),
                  pl.BlockSpec((N_EXPERTS, 1), lambda i: (0, 0))],
        out_specs=[row_out, pl.BlockSpec((TM, d + LANES), lambda i: (i, 0)),
                   pl.BlockSpec((ROUTE_ROWS, TM), lambda i: (0, i)),
                   pl.BlockSpec((BUCKET_ROWS, LANES), lambda i: (0, 0))],
        out_shape=[jax.ShapeDtypeStruct((n_tiles * TM, d), F32),
                   jax.ShapeDtypeStruct((n_tiles * TM, d + LANES), F32),
                   jax.ShapeDtypeStruct((ROUTE_ROWS, n_tiles * TM), F32),
                   jax.ShapeDtypeStruct((BUCKET_ROWS, LANES), F32)],
        compiler_params=_cparams("arbitrary"),
        name="oproj_ln_route",
    )(a, wo, x, mod, ln_g.reshape(1, d), ln_b.reshape(1, d), wr2, wrh, rbias)


def _scatter_body(pos_ref, tail_ref, h_ref, xs_ref, stage, sem):
    i = pl.program_id(0)
    slot = i % 2

    @pl.when(i == 0)
    def _():
        stage[1, :TM] = jnp.zeros((TM,) + stage.shape[2:], stage.dtype)

        def fill(row):
            return pltpu.make_async_copy(stage.at[1, pl.ds(0, TM)],
                                         xs_ref.at[pl.ds(pl.multiple_of(row, TM), TM)], sem.at[1])

        unused = [tail_ref[N_BUCKETS] + k * TM for k in range(N_BUCKETS)]
        for k in range(N_BUCKETS):
            fill(tail_ref[k]).start()
            pl.when(unused[k] < xs_ref.shape[0])(lambda k=k: fill(unused[k]).start())
        for k in range(N_BUCKETS):
            fill(tail_ref[k]).wait()
            pl.when(unused[k] < xs_ref.shape[0])(lambda k=k: fill(unused[k]).wait())

    @pl.when(i >= 2)
    def _():
        _row_dma_wait(xs_ref, stage.at[slot], sem.at[slot])

    stage[slot] = h_ref[...]

    for r in range(stage.shape[1]):
        pltpu.make_async_copy(stage.at[slot, pl.ds(r, 1)], xs_ref.at[pl.ds(pos_ref[0, 0, r], 1)],
                              sem.at[slot]).start(priority=r % 2)

    @pl.when(i == pl.num_programs(0) - 1)
    def _():
        @pl.when(i >= 1)
        def _():
            _row_dma_wait(xs_ref, stage.at[1 - slot], sem.at[1 - slot])
        _row_dma_wait(xs_ref, stage.at[slot], sem.at[slot])


def _scatter_rows(h2p, pos3, tail_rows, n_rows_sorted):
    tiles = max(t for t in SCATTER_TILES if pos3.shape[0] % t == 0)
    rows = tiles * TM
    n_steps = pos3.shape[0] // tiles
    width = h2p.shape[1]
    return pl.pallas_call(
        _scatter_body,
        grid=(n_steps,),
        in_specs=[pl.BlockSpec((1, 1, rows), lambda i: (i, 0, 0), memory_space=pltpu.SMEM),
                  pl.BlockSpec(memory_space=pltpu.SMEM),
                  pl.BlockSpec((rows, width), lambda i: (i, 0))],
        out_specs=pl.BlockSpec(memory_space=pl.ANY),
        out_shape=jax.ShapeDtypeStruct((n_rows_sorted, width), h2p.dtype),
        scratch_shapes=[pltpu.VMEM((2, rows, width), h2p.dtype), pltpu.SemaphoreType.DMA((2,))],
        compiler_params=_cparams("arbitrary"),
        name="moe_scatter",
    )(pos3.reshape(n_steps, 1, rows), tail_rows, h2p)


def _moe_body(ea_ref, eb_ref, act_ref, new_ref, par_ref, hasn_ref, nea_ref, neb_ref, xblk_ref,
              x_ref, wg_hbm, wu_hbm, wd_hbm, y_hbm, wg_f, wu_f, wd_f, wgu_s, wd_s, wsem,
              ystage, ids_v, ids_s, ysem, isem, *, d_exp, layer, n_tok):
    i = pl.program_id(0)
    slot = i % 2
    d_model = wd_s.shape[2]
    lane_ids = lax.broadcasted_iota(jnp.int32, (1, TM), 1)

    def ids_copy(s):
        return pltpu.make_async_copy(ids_v.at[s, pl.ds(0, 1)], ids_s.at[pl.ds(s, 1)], isem.at[s])

    def scatter_tile(s):
        ids_copy(s).wait()
        for r in range(TM):
            pltpu.make_async_copy(ystage.at[s, pl.ds(r, 1)], y_hbm.at[pl.ds(ids_s[s, r], 1)],
                                  ysem.at[s]).start(priority=r % 2)

    def scatter_wait(s):
        _row_dma_wait(y_hbm, ystage.at[s], ysem.at[s])

    @pl.when(i == 0)
    def _():
        ystage[1] = jnp.zeros(ystage.shape[1:], ystage.dtype)
        ids_v[1, 0:1, :] = n_tok + TM + lane_ids
        ids_copy(1).start()
        fill = pltpu.make_async_copy(ystage.at[1], y_hbm.at[pl.ds(n_tok, TM)], ysem.at[0])
        fill.start()
        fill.wait()

    def weight_copies(slot, experts):
        return [pltpu.make_async_copy(hbm.at[layer, ex], buf.at[slot, e], wsem.at[slot])
                for e, ex in enumerate(experts) for hbm, buf in ((wg_hbm, wg_f), (wu_hbm, wu_f), (wd_hbm, wd_f))]

    @pl.when(i == 0)
    def _():
        for cp in weight_copies(par_ref[0], (ea_ref[0], eb_ref[0])):
            cp.start()

    @pl.when(new_ref[i] > 0)
    def _():
        slot = par_ref[i]
        for cp in weight_copies(slot, (ea_ref[i], eb_ref[i])):
            cp.wait()

        @pl.when(hasn_ref[i] > 0)
        def _():
            for cp in weight_copies(1 - slot, (nea_ref[i], neb_ref[i])):
                cp.start()

        for e in range(2):
            wgu_s[e, :, :d_exp] = wg_f[slot, e].astype(BF16)
            wgu_s[e, :, d_exp:] = wu_f[slot, e].astype(BF16)
            wd_s[e] = wd_f[slot, e].astype(BF16)

    active = act_ref[i] > 0
    prev_active = jnp.logical_and(i > 0, act_ref[jnp.maximum(i - 1, 0)] > 0)

    @pl.when(active)
    def _():
        @pl.when(i >= 1)
        def _():
            scatter_wait(slot)

        scatter_tile(1 - slot)
        xw = x_ref[...]
        x = xw[:, :d_model].astype(BF16)
        rec = xw[:, d_model:]

        def expert(e):
            gu = jnp.dot(x, wgu_s[e], preferred_element_type=F32)
            gate = gu[:, :d_exp]
            h = gate * jax.nn.sigmoid(gate) * gu[:, d_exp:] * rec[:, 1 + e:2 + e]
            return jnp.dot(h.astype(BF16), wd_s[e], preferred_element_type=F32)

        ystage[slot] = expert(0) + expert(1)
        rec_t = rec.T
        dump = n_tok + slot * TM + lane_ids
        ids_v[slot, 0:1, :] = jnp.where(rec_t[5:6] > 0.0, rec_t[4:5].astype(jnp.int32), dump)
        ids_copy(slot).start()

    @pl.when(jnp.logical_and(jnp.logical_not(active), prev_active))
    def _():
        scatter_tile(1 - slot)
        scatter_wait(1 - slot)
        scatter_wait(slot)


def _moe(xs, w_gate, w_up, w_down, layer, sched, n_tok):
    n_rows, width = xs.shape
    d_exp, d = w_down.shape[2:]
    n_tiles = n_rows // TM
    body = functools.partial(_moe_body, d_exp=d_exp, layer=layer, n_tok=n_tok)
    grid_spec = pltpu.PrefetchScalarGridSpec(
        num_scalar_prefetch=len(sched),
        grid=(n_tiles,),
        in_specs=[pl.BlockSpec((TM, width), lambda i, *s: (s[-1][i], 0)),
                  pl.BlockSpec(memory_space=pl.ANY), pl.BlockSpec(memory_space=pl.ANY),
                  pl.BlockSpec(memory_space=pl.ANY)],
        out_specs=pl.BlockSpec(memory_space=pl.ANY),
        scratch_shapes=[pltpu.VMEM((2, 2, d, d_exp), F32), pltpu.VMEM((2, 2, d, d_exp), F32),
                        pltpu.VMEM((2, 2, d_exp, d), F32),
                        pltpu.VMEM((2, d, 2 * d_exp), BF16), pltpu.VMEM((2, d_exp, d), BF16),
                        pltpu.SemaphoreType.DMA((2,)),
                        pltpu.VMEM((2, TM, d), F32), pltpu.VMEM((2, 8, TM), jnp.int32),
                        pltpu.SMEM((2, TM), jnp.int32),
                        pltpu.SemaphoreType.DMA((2,)), pltpu.SemaphoreType.DMA((2,))],
    )
    return pl.pallas_call(
        body, grid_spec=grid_spec,
        out_shape=jax.ShapeDtypeStruct((n_tok + 2 * TM, d), F32),
        compiler_params=_cparams("arbitrary"),
        name="moe_pairs",
    )(*sched, xs, w_gate, w_up, w_down)


def _dispatch(route, counts, n_tiles):
    n = route.shape[1]
    bucket = route[0].astype(jnp.int32)
    rank = route[3].astype(jnp.int32)
    counts = counts[:N_BUCKETS, 0].astype(jnp.int32)
    padded = ((counts + TM - 1) // TM) * TM
    ends = jnp.cumsum(padded)
    off = ends - padded

    def lookup(idx, table):
        return jnp.sum(jnp.where(idx[:, None] == jnp.arange(table.shape[0])[None, :], table[None, :], 0), axis=1)

    pos = lookup(bucket, off) + rank
    tile_start = jnp.arange(n_tiles, dtype=jnp.int32) * TM
    active = tile_start < ends[-1]
    last_start = jnp.maximum(ends[-1] - TM, 0)
    tb = jnp.sum((jnp.where(active, tile_start, last_start)[:, None] >= ends[None, :]).astype(jnp.int32), axis=1)
    tb = jnp.minimum(tb, N_BUCKETS - 1)
    ea = (tb // len(PAIRS)) * GROUP_SIZE + lookup(tb % len(PAIRS), jnp.array([p[0] for p in PAIRS], jnp.int32))
    eb = (tb // len(PAIRS)) * GROUP_SIZE + lookup(tb % len(PAIRS), jnp.array([p[1] for p in PAIRS], jnp.int32))
    tiles = jnp.arange(n_tiles, dtype=jnp.int32)
    new = jnp.concatenate([jnp.ones((1,), jnp.int32), (tb[1:] != tb[:-1]).astype(jnp.int32)])
    first_at = jnp.where(new > 0, tiles, n_tiles)
    nxt = jnp.concatenate([lax.cummin(first_at, reverse=True)[1:], jnp.full((1,), n_tiles, jnp.int32)])
    has_next = (nxt < n_tiles).astype(jnp.int32)
    nxt = jnp.minimum(nxt, n_tiles - 1)
    parity = (jnp.cumsum(new) - 1) % 2
    last_used = jnp.maximum(ends[-1] // TM - 1, 0)
    sched = (ea, eb, active.astype(jnp.int32), new, parity.astype(jnp.int32), has_next,
             lookup(nxt, ea), lookup(nxt, eb), jnp.minimum(tiles, last_used))
    tail_rows = jnp.concatenate([jnp.where(padded > 0, ends - TM, 0), ends[-1:]]).astype(jnp.int32)
    return pos.reshape(n // TM, 1, TM), tail_rows, sched


def _rope_tables(seq, ctx_len):
    t = jnp.arange(seq)
    quarter = HEAD_DIM // 4
    inv_freq = ROPE_THETA ** (-jnp.arange(quarter, dtype=F32) / quarter)
    ang_r = (t // GRID_W).astype(F32)[:, None] * inv_freq
    ang_c = (t % GRID_W).astype(F32)[:, None] * inv_freq
    cos_h = jnp.concatenate([jnp.cos(ang_r)] * 2 + [jnp.cos(ang_c)] * 2, axis=-1)
    sin_h = jnp.concatenate([-jnp.sin(ang_r), jnp.sin(ang_r), -jnp.sin(ang_c), jnp.sin(ang_c)], axis=-1)
    cos_t = jnp.concatenate([cos_h, cos_h], axis=-1)
    sin_t = jnp.concatenate([sin_h, sin_h], axis=-1)
    cos_t = jnp.concatenate([cos_t, jnp.ones((ctx_len, LANES), F32)], axis=0)
    sin_t = jnp.concatenate([sin_t, jnp.zeros((ctx_len, LANES), F32)], axis=0)
    return cos_t, sin_t


def kernel(x, c, ctx, c_ctx, w_ada, b_ada, wqkv_a, wo_a, sink_a, wqkv_b, wo_b, lambda_b, subln_b, ln_attn_g, ln_attn_b, ln_ffn_g, ln_ffn_b, w_router, router_bias, w_gate, w_up, w_down):
    n_batch, seq, d = x.shape
    ctx_len = ctx.shape[1]
    depth = w_ada.shape[0]
    assert seq % TM == 0 and ctx_len == TM and seq >= TM + 2 * WINDOW
    assert n_batch + 1 <= ADA_ROWS and d == A_Q_HEADS * HEAD_DIM
    tiles_lat = seq // TM
    alpha = (2 * depth) ** 0.25
    q_cols = A_Q_HEADS * HEAD_DIM
    kv_cols = A_KV_HEADS * HEAD_DIM

    cc = jnp.concatenate([c, c_ctx[None, :], jnp.zeros((ADA_ROWS - n_batch - 1, d), F32)], axis=0)
    mods = _ada_modulation(cc, w_ada, b_ada).reshape(depth, ADA_ROWS, 1, 6 * d)
    cos_t, sin_t = _rope_tables(seq, ctx_len)

    xs = jnp.concatenate([x, ctx], axis=1).reshape(n_batch * (seq + ctx_len), d)
    wr_hi = w_router.astype(BF16)
    wr_lo = (w_router - wr_hi.astype(F32)).astype(BF16)
    wr2 = jnp.concatenate([wr_hi.T, wr_lo.T], axis=0)
    wrh = wr_hi.T
    rbias = router_bias.reshape(N_EXPERTS, 1)

    y = None
    mode = "all"
    for i in range(depth):
        last = i == depth - 1
        j = i // 2
        if i % 2 == 0:
            w = wqkv_a[j].astype(BF16)
            n_rope, dup_from = q_cols + kv_cols, q_cols
        else:
            w = wqkv_b[j].astype(BF16)
            n_rope, dup_from = 2 * q_cols, w.shape[1]
        outs = _lnqkv(xs, y, mods[i - 1] if i else None, ln_ffn_g[i - 1] if i else None,
                      ln_ffn_b[i - 1] if i else None, mods[i], w, cos_t, sin_t,
                      mode_in="all", n_batch=n_batch, tiles_lat=tiles_lat,
                      alpha=alpha, n_rope=n_rope, dup_from=dup_from)
        if i:
            xs, qkv = outs
        else:
            (qkv,) = outs
        if i % 2 == 0:
            att = _attn_a(qkv, sink_a[j], n_batch=n_batch, seq=seq, ctx_len=ctx_len, with_ctx=not last)
            wo = wo_a[j].astype(BF16)
        else:
            lam_init = 0.8 - 0.6 * math.exp(-0.3 * i)
            att = _attn_b(qkv, lambda_b[j], subln_b[j], n_batch=n_batch, seq=seq, ctx_len=ctx_len,
                          with_ctx=not last, lam_init=lam_init)
            wo = wo_b[j].astype(BF16)
        mode = "lat" if last else "all"
        xs, h2p, route, counts = _oproj(att, wo, xs, mods[i], ln_attn_g[i], ln_attn_b[i], wr2, wrh, rbias,
                                        mode_in=mode, n_batch=n_batch, tiles_lat=tiles_lat, alpha=alpha)
        n_tiles = h2p.shape[0] // TM + N_BUCKETS
        pos3, tail_rows, sched = _dispatch(route, counts, n_tiles)
        x_sorted = _scatter_rows(h2p, pos3, tail_rows, n_tiles * TM)
        y = _moe(x_sorted, w_gate, w_up, w_down, i, sched, h2p.shape[0])

    (out,) = _lnqkv(xs, y, mods[depth - 1], ln_ffn_g[depth - 1], ln_ffn_b[depth - 1], None, None, None, None,
                    mode_in="compact", n_batch=n_batch, tiles_lat=tiles_lat, alpha=alpha)
    return out.reshape(n_batch, seq, d)
```

```python
import functools
import math

import jax
import jax.numpy as jnp
from jax import lax
from jax.experimental import pallas as pl
from jax.experimental.pallas import tpu as pltpu

F32 = jnp.float32
BF16 = jnp.bfloat16

HEAD_DIM = 64
A_Q_HEADS = 16
A_KV_HEADS = 4
A_GROUP = A_Q_HEADS // A_KV_HEADS
GRID_W = 64
WINDOW = 128
N_EXPERTS = 16
N_GROUPS = 4
GROUP_SIZE = N_EXPERTS // N_GROUPS
ROPE_THETA = 10000.0
LN_EPS = 1e-6
SUBLN_EPS = 1e-5
NEG_INF = -1e30
ATTN_SCALE = HEAD_DIM ** -0.5
LOG2E = math.log2(math.e)
VT_ROWS = 144

LANES = 128
TM = 256
DIFF_HEADS_PER_STEP = 4
SCATTER_TILES = (1, 2, 4)
ADA_ROWS = 24
VMEM_LIMIT = 56 * 1024 * 1024

PAIRS = ((0, 1), (0, 2), (0, 3), (1, 2), (1, 3), (2, 3))
N_BUCKETS = N_GROUPS * len(PAIRS)

BUCKET_ROWS = 32
ROUTE_ROWS = 8

NT_DIMS = (((1,), (1,)), ((), ()))


def _cparams(*sem):
    return pltpu.CompilerParams(dimension_semantics=sem, vmem_limit_bytes=VMEM_LIMIT)


def _ada_body(c_ref, w_ref, b_ref, o_ref):
    c = c_ref[...]
    sc = c * jax.nn.sigmoid(c)
    o_ref[0] = jnp.dot(sc, w_ref[0], precision=lax.Precision.HIGHEST,
                       preferred_element_type=F32) + b_ref[0]


def _ada_modulation(cc, w_ada, b_ada):
    depth, d, d6 = w_ada.shape
    tn = 1536
    return pl.pallas_call(
        _ada_body,
        grid=(depth, d6 // tn),
        in_specs=[
            pl.BlockSpec((ADA_ROWS, d), lambda l, n: (0, 0)),
            pl.BlockSpec((1, d, tn), lambda l, n: (l, 0, n)),
            pl.BlockSpec((1, 1, tn), lambda l, n: (l, 0, n)),
        ],
        out_specs=pl.BlockSpec((1, ADA_ROWS, tn), lambda l, n: (l, 0, n)),
        out_shape=jax.ShapeDtypeStruct((depth, ADA_ROWS, d6), F32),
        compiler_params=_cparams("arbitrary", "arbitrary"),
        name="ada_mod",
    )(cc, w_ada, b_ada.reshape(depth, 1, d6))


def _tile_maps(mode, tiles_lat, n_batch):
    tpb = tiles_lat + 1
    if mode == "all":
        blk = lambda i: i
        modrow = lambda i: jnp.where(i % tpb == tiles_lat, n_batch, i // tpb)
        pos = lambda i: i % tpb
    elif mode == "lat":
        blk = lambda i: (i // tiles_lat) * tpb + i % tiles_lat
        modrow = lambda i: i // tiles_lat
        pos = lambda i: i % tiles_lat
    else:
        blk = lambda i: i
        modrow = lambda i: i // tiles_lat
        pos = lambda i: i % tiles_lat
    return blk, modrow, pos


def _mod_chunk(mod_ref, k):
    d = mod_ref.shape[2] // 6
    return mod_ref[0, :, k * d:(k + 1) * d]


def _layer_norm(z, g, b):
    mu = jnp.mean(z, axis=-1, keepdims=True)
    zc = z - mu
    var = jnp.mean(zc * zc, axis=-1, keepdims=True)
    return zc * lax.rsqrt(var + LN_EPS) * g + b


def _row_dma_wait(hbm_ref, buf_ref, sem):
    pltpu.make_async_copy(hbm_ref.at[pl.ds(0, buf_ref.shape[0])], buf_ref, sem).wait()


def _lnqkv_body(*refs, has_ln, has_qkv, alpha, n_rope, dup_from, n_out):
    refs = list(refs)
    x_ref = refs.pop(0)
    if has_ln:
        y_ref, g2_ref, lng_ref, lnb_ref = refs[:4]
        refs = refs[4:]
    if has_qkv:
        mod_ref, w_ref, cos_ref, sin_ref = refs[:4]
        refs = refs[4:]
    x = x_ref[...]
    if has_ln:
        xo_ref = refs.pop(0)
        z = alpha * x + _mod_chunk(g2_ref, 5) * y_ref[...]
        x = _layer_norm(z, lng_ref[...], lnb_ref[...])
        xo_ref[...] = x
    if not has_qkv:
        return
    qkv_ref = refs.pop(0)
    h = (x * (1.0 + _mod_chunk(mod_ref, 1)) + _mod_chunk(mod_ref, 0)).astype(BF16)
    r = jnp.dot(h, w_ref[...], preferred_element_type=F32)
    cos = cos_ref[...]
    sin = sin_ref[...]
    lane = lax.broadcasted_iota(jnp.int32, (TM, LANES), 1)
    first16 = (lane % 32) < 16
    low_half = lane < HEAD_DIM
    q_cols = A_Q_HEADS * HEAD_DIM
    dst = 0
    for c in range(n_out // LANES):
        seg = r[:, c * LANES:(c + 1) * LANES]
        col = c * LANES
        if col < q_cols:
            seg = seg * (ATTN_SCALE * LOG2E)
        if col < n_rope:
            rot = jnp.where(first16, pltpu.roll(seg, LANES - 16, 1), pltpu.roll(seg, 16, 1))
            seg = seg * cos + rot * sin
        if col >= dup_from:
            swapped = pltpu.roll(seg, HEAD_DIM, 1)
            qkv_ref[:, dst:dst + LANES] = jnp.where(low_half, seg, swapped).astype(BF16)
            qkv_ref[:, dst + LANES:dst + 2 * LANES] = jnp.where(low_half, swapped, seg).astype(BF16)
            dst += 2 * LANES
        else:
            qkv_ref[:, dst:dst + LANES] = seg.astype(BF16)
            dst += LANES


def _lnqkv(x, y, mod_ln, ln_g, ln_b, mod_qkv, w, cos_t, sin_t, *, mode_in, n_batch, tiles_lat,
           alpha, n_rope=0, dup_from=0):
    d = x.shape[1]
    has_ln = y is not None
    has_qkv = w is not None
    n_tiles = (n_batch * (tiles_lat + 1)) if mode_in == "all" else n_batch * tiles_lat
    blk, modrow, pos = _tile_maps(mode_in, tiles_lat, n_batch)
    row = pl.BlockSpec((TM, d), lambda i: (blk(i), 0))
    vec = pl.BlockSpec((1, d), lambda i: (0, 0))

    modspec = pl.BlockSpec((1, 1, 6 * d), lambda i: (modrow(i), 0, 0))

    args, in_specs, out_specs, out_shape, scratch = [x], [row], [], [], []
    n_out = 0
    if has_ln:
        args += [y, mod_ln, ln_g.reshape(1, d), ln_b.reshape(1, d)]
        in_specs += [pl.BlockSpec((TM, y.shape[1]), lambda i: (blk(i), 0)), modspec, vec, vec]
        out_specs.append(row)
        out_shape.append(jax.ShapeDtypeStruct(x.shape, F32))
    if has_qkv:
        n_out = w.shape[1]
        n_store = n_out + (n_out - dup_from)
        args += [mod_qkv, w, cos_t, sin_t]
        in_specs += [modspec,
                     pl.BlockSpec((d, n_out), lambda i: (0, 0)),
                     pl.BlockSpec((TM, LANES), lambda i: (pos(i), 0)),
                     pl.BlockSpec((TM, LANES), lambda i: (pos(i), 0))]
        out_specs.append(pl.BlockSpec((TM, n_store), lambda i: (blk(i), 0)))
        out_shape.append(jax.ShapeDtypeStruct((x.shape[0], n_store), BF16))
    body = functools.partial(_lnqkv_body, has_ln=has_ln, has_qkv=has_qkv, alpha=alpha,
                             n_rope=n_rope, dup_from=dup_from, n_out=n_out)
    return pl.pallas_call(
        body, grid=(n_tiles,), in_specs=in_specs, out_specs=out_specs, out_shape=out_shape,
        scratch_shapes=scratch, compiler_params=_cparams("arbitrary"),
        name=("ln_" if has_ln else "") + ("qkv" if has_qkv else "out"),
    )(*args)


def _softmax_parts(scores, biases, sink):
    scores = [s if bias is None else s + bias[...] for s, bias in zip(scores, biases)]
    m = None
    for s in scores:
        sm = jnp.max(s, axis=-1, keepdims=True)
        m = sm if m is None else jnp.maximum(m, sm)
    if sink is not None:
        m = jnp.maximum(m, sink)
    probs = []
    denom = None
    for s in scores:
        e = jnp.exp2(s - m)
        es = jnp.sum(e, axis=-1, keepdims=True)
        denom = es if denom is None else denom + es
        probs.append(e.astype(BF16))
    if sink is not None:
        denom = denom + jnp.exp2(sink - m)
    return probs, 1.0 / denom


def _attn_a_body(sink_ref, q_ref, k_ref, v_ref, o_ref, bias_ref, *, seq, tiles_lat):
    qi = pl.program_id(1)
    lane = lax.broadcasted_iota(jnp.int32, (TM, LANES), 1)
    low_half = lane < HEAD_DIM
    span = TM + 2 * WINDOW
    rows = lax.broadcasted_iota(jnp.int32, (A_GROUP * TM, 1), 0)

    def run(parts):
        all_scores = []
        for h in range(A_KV_HEADS):
            qs = []
            for g in range(A_GROUP):
                q2 = q_ref[:, (h * A_GROUP + g) // 2 * LANES:((h * A_GROUP + g) // 2 + 1) * LANES]
                keep = low_half if g % 2 == 0 else jnp.logical_not(low_half)
                qs.append(jnp.where(keep, q2, jnp.zeros_like(q2)))
            qh = jnp.concatenate(qs, axis=0)
            scores = []
            for (r0, nr, _) in parts:
                kk = k_ref[pl.ds(r0, nr), h * LANES:(h + 1) * LANES]
                scores.append(lax.dot_general(qh, kk, NT_DIMS, preferred_element_type=F32))
            all_scores.append(scores)
        for h in range(A_KV_HEADS):
            sink = jnp.zeros((A_GROUP * TM, 1), F32)
            for g in range(A_GROUP):
                sink = jnp.where(rows // TM == g, sink_ref[h * A_GROUP + g] * LOG2E, sink)
            probs, inv = _softmax_parts(all_scores[h], [p[2] for p in parts], sink)
            acc = None
            for p, (r0, nr, _) in zip(probs, parts):
                vv = v_ref[pl.ds(r0, nr), h * LANES:(h + 1) * LANES]
                o = jnp.dot(p, vv, preferred_element_type=F32)
                acc = o if acc is None else acc + o
            acc = acc * inv
            for gp in range(A_GROUP // 2):
                grp = h * (A_GROUP // 2) + gp
                o_ref[:, grp * LANES:(grp + 1) * LANES] = jnp.where(
                    low_half, acc[2 * gp * TM:(2 * gp + 1) * TM], acc[(2 * gp + 1) * TM:(2 * gp + 2) * TM]
                ).astype(BF16)

    @pl.when(qi < tiles_lat)
    def _():
        start = pl.multiple_of(jnp.clip(qi * TM - WINDOW, 0, seq - span), WINDOW)
        qpos = qi * TM + lax.broadcasted_iota(jnp.int32, (TM, span), 0)
        kpos = start + lax.broadcasted_iota(jnp.int32, (TM, span), 1)
        bias = jnp.where(jnp.abs(kpos - qpos) <= WINDOW, 0.0, NEG_INF)
        for g in range(A_GROUP):
            bias_ref[g * TM:(g + 1) * TM, :] = bias
        run([(start, span, bias_ref), (seq, k_ref.shape[0] - seq, None)])

    @pl.when(qi >= tiles_lat)
    def _():
        run([(seq, k_ref.shape[0] - seq, None)])


def _attn_a(qkv, sink, *, n_batch, seq, ctx_len, with_ctx):
    tiles_lat = seq // TM
    tpb = tiles_lat + ctx_len // TM
    rows_b = seq + ctx_len
    q_cols = A_Q_HEADS * HEAD_DIM
    kv_cols = 2 * A_KV_HEADS * HEAD_DIM
    nq = tpb if with_ctx else tiles_lat
    body = functools.partial(_attn_a_body, seq=seq, tiles_lat=tiles_lat)
    return pl.pallas_call(
        body,
        grid=(n_batch, nq),
        in_specs=[
            pl.BlockSpec(memory_space=pltpu.SMEM),
            pl.BlockSpec((TM, q_cols), lambda b, i: (b * tpb + i, 0)),
            pl.BlockSpec((rows_b, kv_cols), lambda b, i: (b, q_cols // kv_cols)),
            pl.BlockSpec((rows_b, kv_cols), lambda b, i: (b, q_cols // kv_cols + 1)),
        ],
        out_specs=pl.BlockSpec((TM, q_cols), lambda b, i: (b * nq + i, 0)),
        out_shape=jax.ShapeDtypeStruct((n_batch * nq * TM, q_cols), BF16),
        scratch_shapes=[pltpu.VMEM((A_GROUP * TM, TM + 2 * WINDOW), F32)],
        compiler_params=_cparams("arbitrary", "arbitrary"),
        name="attn_window",
    )(sink, qkv, qkv, qkv)


def _attn_b_body(lam_ref, g_ref, q_ref, k_ref, v_ref, o_ref, vt_ref, *, seq, tiles_lat, lam_init, heads):
    qi = pl.program_id(2)
    lp = lam_ref[...]
    lam = (jnp.exp(jnp.sum(lp[0:1] * lp[1:2], axis=-1, keepdims=True))
           - jnp.exp(jnp.sum(lp[2:3] * lp[3:4], axis=-1, keepdims=True)) + lam_init)
    lane = lax.broadcasted_iota(jnp.int32, (TM, LANES), 1)
    low_half = lane < HEAD_DIM
    n_keys = k_ref.shape[0]

    @pl.when(qi == 0)
    def _():
        for hh in range(heads):
            vt_ref[hh, :LANES, :] = v_ref[:, hh * LANES:(hh + 1) * LANES].astype(F32).T.astype(BF16)
            row = lax.broadcasted_iota(jnp.int32, (VT_ROWS - LANES, n_keys), 0)
            vt_ref[hh, LANES:, :] = jnp.where(row == 0, 1.0, 0.0).astype(BF16)

    def run(r0, nr):
        scores = []
        for hh in range(heads):
            cols = slice(hh * LANES, (hh + 1) * LANES)
            q = q_ref[:, cols]
            kk = k_ref[pl.ds(r0, nr), cols]
            for qm in (jnp.where(low_half, q, jnp.zeros_like(q)), jnp.where(low_half, jnp.zeros_like(q), q)):
                scores.append(lax.dot_general(kk, qm, NT_DIMS, preferred_element_type=F32))
        for hh in range(heads):
            vt = vt_ref[hh, :, pl.ds(r0, nr)]
            outs = []
            for s in scores[2 * hh:2 * hh + 2]:
                e = jnp.exp2(s - jnp.max(s, axis=0, keepdims=True)).astype(BF16)
                ov = jnp.dot(vt, e, preferred_element_type=F32)
                outs.append((ov[:LANES], 1.0 / ov[LANES:LANES + 1]))
            o = outs[0][0] * outs[0][1] - outs[1][0] * (lam * outs[1][1])
            ms = jnp.mean(o * o, axis=0, keepdims=True)
            o = o * lax.rsqrt(ms + SUBLN_EPS) * g_ref[...] * (1.0 - lam_init)
            o_ref[:, hh * LANES:(hh + 1) * LANES] = o.T.astype(BF16)

    @pl.when(qi < tiles_lat)
    def _():
        run(0, k_ref.shape[0])

    @pl.when(qi >= tiles_lat)
    def _():
        run(seq, k_ref.shape[0] - seq)


def _attn_b(qkv, lam_params, subln_g, *, n_batch, seq, ctx_len, with_ctx, lam_init):
    tiles_lat = seq // TM
    tpb = tiles_lat + ctx_len // TM
    rows_b = seq + ctx_len
    n_heads = qkv.shape[1] // (3 * LANES)
    n_hgrp = n_heads // DIFF_HEADS_PER_STEP
    width = DIFF_HEADS_PER_STEP * LANES
    nq = tpb if with_ctx else tiles_lat
    body = functools.partial(_attn_b_body, seq=seq, tiles_lat=tiles_lat, lam_init=lam_init,
                             heads=DIFF_HEADS_PER_STEP)
    return pl.pallas_call(
        body,
        grid=(n_batch, n_hgrp, nq),
        in_specs=[
            pl.BlockSpec((4, HEAD_DIM), lambda b, h, i: (0, 0)),
            pl.BlockSpec((LANES, 1), lambda b, h, i: (0, 0)),
            pl.BlockSpec((TM, width), lambda b, h, i: (b * tpb + i, h)),
            pl.BlockSpec((rows_b, width), lambda b, h, i: (b, n_hgrp + h)),
            pl.BlockSpec((rows_b, width), lambda b, h, i: (b, 2 * n_hgrp + h)),
        ],
        out_specs=pl.BlockSpec((TM, width), lambda b, h, i: (b * nq + i, h)),
        out_shape=jax.ShapeDtypeStruct((n_batch * nq * TM, n_heads * LANES), BF16),
        compiler_params=_cparams("arbitrary", "arbitrary", "arbitrary"),
        scratch_shapes=[pltpu.VMEM((DIFF_HEADS_PER_STEP, VT_ROWS, rows_b), BF16)],
        name="attn_diff",
    )(lam_params, subln_g.reshape(LANES, 1), qkv, qkv, qkv)


def _route_rows(logits, bias):
    s = jax.nn.sigmoid(logits)
    biased = s + bias
    b = [biased[e:e + 1, :] for e in range(N_EXPERTS)]
    u = [s[e:e + 1, :] for e in range(N_EXPERTS)]
    gscore = []
    for g in range(N_GROUPS):
        best = None
        for (i, j) in PAIRS:
            t = b[GROUP_SIZE * g + i] + b[GROUP_SIZE * g + j]
            best = t if best is None else jnp.maximum(best, t)
        gscore.append(best)
    gsel = jnp.zeros(gscore[0].shape, jnp.int32)
    gbest = gscore[0]
    for g in range(1, N_GROUPS):
        better = gscore[g] > gbest
        gsel = jnp.where(better, g, gsel)
        gbest = jnp.where(better, gscore[g], gbest)

    def pick(rows, k):
        out = rows[GROUP_SIZE * (N_GROUPS - 1) + k]
        for g in range(N_GROUPS - 2, -1, -1):
            out = jnp.where(gsel == g, rows[GROUP_SIZE * g + k], out)
        return out

    v = [pick(b, k) for k in range(GROUP_SIZE)]
    w = [pick(u, k) for k in range(GROUP_SIZE)]
    sel = []
    for k in range(GROUP_SIZE):
        cnt = jnp.zeros(gsel.shape, jnp.int32)
        for j in range(GROUP_SIZE):
            if j == k:
                continue
            beats = (v[j] >= v[k]) if j < k else (v[j] > v[k])
            cnt = cnt + jnp.where(beats, 1, 0)
        sel.append(cnt < 2)
    pidx = jnp.zeros(gsel.shape, jnp.int32)
    u_lo = jnp.zeros(gbest.shape, F32)
    u_hi = jnp.zeros(gbest.shape, F32)
    for idx, (i, j) in enumerate(PAIRS):
        both = jnp.logical_and(sel[i], sel[j])
        pidx = jnp.where(both, idx, pidx)
        u_lo = jnp.where(both, w[i], u_lo)
        u_hi = jnp.where(both, w[j], u_hi)
    tot = u_lo + u_hi
    bucket = (gsel * len(PAIRS) + pidx).astype(F32)
    return bucket, u_lo / tot, u_hi / tot


def _oproj_body(a_ref, wo_ref, x_ref, mod_ref, lng_ref, lnb_ref, wr2_ref, wrh_ref, rb_ref,
                xo_ref, h2p_ref, r_ref, cnt_ref, *, alpha):
    i = pl.program_id(0)
    d = x_ref.shape[1]
    al = jnp.dot(a_ref[...], wo_ref[...], preferred_element_type=F32)
    z = alpha * x_ref[...] + _mod_chunk(mod_ref, 2) * al
    xn = _layer_norm(z, lng_ref[...], lnb_ref[...])
    xo_ref[...] = xn
    h2 = xn * (1.0 + _mod_chunk(mod_ref, 4)) + _mod_chunk(mod_ref, 3)
    hi = h2.astype(BF16)
    lo = (h2 - hi.astype(F32)).astype(BF16)
    l2 = lax.dot_general(wr2_ref[...], hi, NT_DIMS, preferred_element_type=F32)
    l1 = lax.dot_general(wrh_ref[...], lo, NT_DIMS, preferred_element_type=F32)
    logits = l2[:N_EXPERTS] + l2[N_EXPERTS:] + l1
    bucket, w_lo, w_hi = _route_rows(logits, rb_ref[...])

    @pl.when(i == 0)
    def _():
        cnt_ref[...] = jnp.zeros_like(cnt_ref)

    onehot = jnp.where(lax.broadcasted_iota(jnp.int32, (BUCKET_ROWS, TM), 0).astype(F32) == bucket, 1.0, 0.0)
    earlier = (lax.broadcasted_iota(jnp.int32, (TM, TM), 0) < lax.broadcasted_iota(jnp.int32, (TM, TM), 1))
    before = jnp.dot(onehot.astype(BF16), jnp.where(earlier, 1.0, 0.0).astype(BF16),
                     preferred_element_type=F32)
    cnt = cnt_ref[...]
    rank = jnp.sum(onehot * (before + cnt[:, 0:1]), axis=0, keepdims=True)
    cnt_ref[...] = cnt + jnp.sum(onehot, axis=1, keepdims=True)

    r_ref[0:1, :] = bucket
    r_ref[1:2, :] = w_lo
    r_ref[2:3, :] = w_hi
    r_ref[3:4, :] = rank
    r_ref[4:5, :] = (i * TM + lax.broadcasted_iota(jnp.int32, (1, TM), 1)).astype(F32)
    r_ref[5:6, :] = jnp.ones((1, TM), F32)
    r_ref[6:ROUTE_ROWS, :] = jnp.zeros((ROUTE_ROWS - 6, TM), F32)
    rec = jnp.concatenate([r_ref[...], jnp.zeros((LANES - ROUTE_ROWS, TM), F32)], axis=0).T
    h2p_ref[:, :d] = hi.astype(F32)
    h2p_ref[:, d:] = rec


def _oproj(a, wo, x, mod, ln_g, ln_b, wr2, wrh, rbias, *, mode_in, n_batch, tiles_lat, alpha):
    d = x.shape[1]
    compact = mode_in == "lat"
    n_tiles = n_batch * tiles_lat if compact else n_batch * (tiles_lat + 1)
    blk, modrow, _ = _tile_maps(mode_in, tiles_lat, n_batch)
    row_in = pl.BlockSpec((TM, d), lambda i: (blk(i), 0))
    row_out = pl.BlockSpec((TM, d), lambda i: (i, 0))
    vec = pl.BlockSpec((1, d), lambda i: (0, 0))

    modspec = pl.BlockSpec((1, 1, 6 * d), lambda i: (modrow(i), 0, 0))

    body = functools.partial(_oproj_body, alpha=alpha)
    return pl.pallas_call(
        body,
        grid=(n_tiles,),
        in_specs=[row_out, pl.BlockSpec((d, d), lambda i: (0, 0)), row_in, modspec, vec, vec,
                  pl.BlockSpec((2 * N_EXPERTS, d), lambda i: (0, 0)),
                  pl.BlockSpec((N_EXPERTS, d), lambda i: (0, 0)),
                  pl.BlockSpec((N_EXPERTS, 1), lambda i: (0, 0))],
        out_specs=[row_out, pl.BlockSpec((TM, d + LANES), lambda i: (i, 0)),
                   pl.BlockSpec((ROUTE_ROWS, TM), lambda i: (0, i)),
                   pl.BlockSpec((BUCKET_ROWS, LANES), lambda i: (0, 0))],
        out_shape=[jax.ShapeDtypeStruct((n_tiles * TM, d), F32),
                   jax.ShapeDtypeStruct((n_tiles * TM, d + LANES), F32),
                   jax.ShapeDtypeStruct((ROUTE_ROWS, n_tiles * TM), F32),
                   jax.ShapeDtypeStruct((BUCKET_ROWS, LANES), F32)],
        compiler_params=_cparams("arbitrary"),
        name="oproj_ln_route",
    )(a, wo, x, mod, ln_g.reshape(1, d), ln_b.reshape(1, d), wr2, wrh, rbias)


def _scatter_body(pos_ref, tail_ref, h_ref, xs_ref, stage, sem):
    i = pl.program_id(0)
    slot = i % 2

    @pl.when(i == 0)
    def _():
        stage[1, :TM] = jnp.zeros((TM,) + stage.shape[2:], stage.dtype)

        def fill(row):
            return pltpu.make_async_copy(stage.at[1, pl.ds(0, TM)],
                                         xs_ref.at[pl.ds(pl.multiple_of(row, TM), TM)], sem.at[1])

        unused = [tail_ref[N_BUCKETS] + k * TM for k in range(N_BUCKETS)]
        for k in range(N_BUCKETS):
            fill(tail_ref[k]).start()
            pl.when(unused[k] < xs_ref.shape[0])(lambda k=k: fill(unused[k]).start())
        for k in range(N_BUCKETS):
            fill(tail_ref[k]).wait()
            pl.when(unused[k] < xs_ref.shape[0])(lambda k=k: fill(unused[k]).wait())

    @pl.when(i >= 2)
    def _():
        _row_dma_wait(xs_ref, stage.at[slot], sem.at[slot])

    stage[slot] = h_ref[...]

    for r in range(stage.shape[1]):
        pltpu.make_async_copy(stage.at[slot, pl.ds(r, 1)], xs_ref.at[pl.ds(pos_ref[0, 0, r], 1)],
                              sem.at[slot]).start(priority=r % 2)

    @pl.when(i == pl.num_programs(0) - 1)
    def _():
        @pl.when(i >= 1)
        def _():
            _row_dma_wait(xs_ref, stage.at[1 - slot], sem.at[1 - slot])
        _row_dma_wait(xs_ref, stage.at[slot], sem.at[slot])


def _scatter_rows(h2p, pos3, tail_rows, n_rows_sorted):
    tiles = max(t for t in SCATTER_TILES if pos3.shape[0] % t == 0)
    rows = tiles * TM
    n_steps = pos3.shape[0] // tiles
    width = h2p.shape[1]
    return pl.pallas_call(
        _scatter_body,
        grid=(n_steps,),
        in_specs=[pl.BlockSpec((1, 1, rows), lambda i: (i, 0, 0), memory_space=pltpu.SMEM),
                  pl.BlockSpec(memory_space=pltpu.SMEM),
                  pl.BlockSpec((rows, width), lambda i: (i, 0))],
        out_specs=pl.BlockSpec(memory_space=pl.ANY),
        out_shape=jax.ShapeDtypeStruct((n_rows_sorted, width), h2p.dtype),
        scratch_shapes=[pltpu.VMEM((2, rows, width), h2p.dtype), pltpu.SemaphoreType.DMA((2,))],
        compiler_params=_cparams("arbitrary"),
        name="moe_scatter",
    )(pos3.reshape(n_steps, 1, rows), tail_rows, h2p)


def _moe_body(ea_ref, eb_ref, act_ref, new_ref, par_ref, hasn_ref, nea_ref, neb_ref, xblk_ref,
              x_ref, wg_hbm, wu_hbm, wd_hbm, y_hbm, wg_f, wu_f, wd_f, wgu_s, wd_s, wsem,
              ystage, ids_v, ids_s, ysem, isem, *, d_exp, layer, n_tok):
    i = pl.program_id(0)
    slot = i % 2
    d_model = wd_s.shape[2]
    lane_ids = lax.broadcasted_iota(jnp.int32, (1, TM), 1)

    def ids_copy(s):
        return pltpu.make_async_copy(ids_v.at[s, pl.ds(0, 1)], ids_s.at[pl.ds(s, 1)], isem.at[s])

    def scatter_tile(s):
        ids_copy(s).wait()
        for r in range(TM):
            pltpu.make_async_copy(ystage.at[s, pl.ds(r, 1)], y_hbm.at[pl.ds(ids_s[s, r], 1)],
                                  ysem.at[s]).start()

    def scatter_wait(s):
        _row_dma_wait(y_hbm, ystage.at[s], ysem.at[s])

    @pl.when(i == 0)
    def _():
        ystage[1] = jnp.zeros(ystage.shape[1:], ystage.dtype)
        ids_v[1, 0:1, :] = n_tok + TM + lane_ids
        ids_copy(1).start()
        fill = pltpu.make_async_copy(ystage.at[1], y_hbm.at[pl.ds(n_tok, TM)], ysem.at[0])
        fill.start()
        fill.wait()

    def weight_copies(slot, experts):
        return [pltpu.make_async_copy(hbm.at[layer, ex], buf.at[slot, e], wsem.at[slot])
                for e, ex in enumerate(experts) for hbm, buf in ((wg_hbm, wg_f), (wu_hbm, wu_f), (wd_hbm, wd_f))]

    @pl.when(i == 0)
    def _():
        for cp in weight_copies(par_ref[0], (ea_ref[0], eb_ref[0])):
            cp.start(priority=1)

    @pl.when(new_ref[i] > 0)
    def _():
        slot = par_ref[i]
        for cp in weight_copies(slot, (ea_ref[i], eb_ref[i])):
            cp.wait()

        @pl.when(hasn_ref[i] > 0)
        def _():
            for cp in weight_copies(1 - slot, (nea_ref[i], neb_ref[i])):
                cp.start(priority=1)

        for e in range(2):
            wgu_s[e, :, :d_exp] = wg_f[slot, e].astype(BF16)
            wgu_s[e, :, d_exp:] = wu_f[slot, e].astype(BF16)
            wd_s[e] = wd_f[slot, e].astype(BF16)

    active = act_ref[i] > 0
    prev_active = jnp.logical_and(i > 0, act_ref[jnp.maximum(i - 1, 0)] > 0)

    @pl.when(active)
    def _():
        @pl.when(i >= 1)
        def _():
            scatter_wait(slot)

        scatter_tile(1 - slot)
        xw = x_ref[...]
        x = xw[:, :d_model].astype(BF16)
        rec = xw[:, d_model:]

        def expert(e):
            gu = jnp.dot(x, wgu_s[e], preferred_element_type=F32)
            gate = gu[:, :d_exp]
            h = gate * jax.nn.sigmoid(gate) * gu[:, d_exp:] * rec[:, 1 + e:2 + e]
            return jnp.dot(h.astype(BF16), wd_s[e], preferred_element_type=F32)

        ystage[slot] = expert(0) + expert(1)
        rec_t = rec.T
        dump = n_tok + slot * TM + lane_ids
        ids_v[slot, 0:1, :] = jnp.where(rec_t[5:6] > 0.0, rec_t[4:5].astype(jnp.int32), dump)
        ids_copy(slot).start()

    @pl.when(jnp.logical_and(jnp.logical_not(active), prev_active))
    def _():
        scatter_tile(1 - slot)
        scatter_wait(1 - slot)
        scatter_wait(slot)


def _moe(xs, w_gate, w_up, w_down, layer, sched, n_tok):
    n_rows, width = xs.shape
    d_exp, d = w_down.shape[2:]
    n_tiles = n_rows // TM
    body = functools.partial(_moe_body, d_exp=d_exp, layer=layer, n_tok=n_tok)
    grid_spec = pltpu.PrefetchScalarGridSpec(
        num_scalar_prefetch=len(sched),
        grid=(n_tiles,),
        in_specs=[pl.BlockSpec((TM, width), lambda i, *s: (s[-1][i], 0)),
                  pl.BlockSpec(memory_space=pl.ANY), pl.BlockSpec(memory_space=pl.ANY),
                  pl.BlockSpec(memory_space=pl.ANY)],
        out_specs=pl.BlockSpec(memory_space=pl.ANY),
        scratch_shapes=[pltpu.VMEM((2, 2, d, d_exp), F32), pltpu.VMEM((2, 2, d, d_exp), F32),
                        pltpu.VMEM((2, 2, d_exp, d), F32),
                        pltpu.VMEM((2, d, 2 * d_exp), BF16), pltpu.VMEM((2, d_exp, d), BF16),
                        pltpu.SemaphoreType.DMA((2,)),
                        pltpu.VMEM((2, TM, d), F32), pltpu.VMEM((2, 8, TM), jnp.int32),
                        pltpu.SMEM((2, TM), jnp.int32),
                        pltpu.SemaphoreType.DMA((2,)), pltpu.SemaphoreType.DMA((2,))],
    )
    return pl.pallas_call(
        body, grid_spec=grid_spec,
        out_shape=jax.ShapeDtypeStruct((n_tok + 2 * TM, d), F32),
        compiler_params=_cparams("arbitrary"),
        name="moe_pairs",
    )(*sched, xs, w_gate, w_up, w_down)


def _dispatch(route, counts, n_tiles):
    n = route.shape[1]
    bucket = route[0].astype(jnp.int32)
    rank = route[3].astype(jnp.int32)
    counts = counts[:N_BUCKETS, 0].astype(jnp.int32)
    padded = ((counts + TM - 1) // TM) * TM
    ends = jnp.cumsum(padded)
    off = ends - padded

    def lookup(idx, table):
        return jnp.sum(jnp.where(idx[:, None] == jnp.arange(table.shape[0])[None, :], table[None, :], 0), axis=1)

    pos = lookup(bucket, off) + rank
    tile_start = jnp.arange(n_tiles, dtype=jnp.int32) * TM
    active = tile_start < ends[-1]
    last_start = jnp.maximum(ends[-1] - TM, 0)
    tb = jnp.sum((jnp.where(active, tile_start, last_start)[:, None] >= ends[None, :]).astype(jnp.int32), axis=1)
    tb = jnp.minimum(tb, N_BUCKETS - 1)
    ea = (tb // len(PAIRS)) * GROUP_SIZE + lookup(tb % len(PAIRS), jnp.array([p[0] for p in PAIRS], jnp.int32))
    eb = (tb // len(PAIRS)) * GROUP_SIZE + lookup(tb % len(PAIRS), jnp.array([p[1] for p in PAIRS], jnp.int32))
    tiles = jnp.arange(n_tiles, dtype=jnp.int32)
    new = jnp.concatenate([jnp.ones((1,), jnp.int32), (tb[1:] != tb[:-1]).astype(jnp.int32)])
    first_at = jnp.where(new > 0, tiles, n_tiles)
    nxt = jnp.concatenate([lax.cummin(first_at, reverse=True)[1:], jnp.full((1,), n_tiles, jnp.int32)])
    has_next = (nxt < n_tiles).astype(jnp.int32)
    nxt = jnp.minimum(nxt, n_tiles - 1)
    parity = (jnp.cumsum(new) - 1) % 2
    last_used = jnp.maximum(ends[-1] // TM - 1, 0)
    sched = (ea, eb, active.astype(jnp.int32), new, parity.astype(jnp.int32), has_next,
             lookup(nxt, ea), lookup(nxt, eb), jnp.minimum(tiles, last_used))
    tail_rows = jnp.concatenate([jnp.where(padded > 0, ends - TM, 0), ends[-1:]]).astype(jnp.int32)
    return pos.reshape(n // TM, 1, TM), tail_rows, sched


def _rope_tables(seq, ctx_len):
    t = jnp.arange(seq)
    quarter = HEAD_DIM // 4
    inv_freq = ROPE_THETA ** (-jnp.arange(quarter, dtype=F32) / quarter)
    ang_r = (t // GRID_W).astype(F32)[:, None] * inv_freq
    ang_c = (t % GRID_W).astype(F32)[:, None] * inv_freq
    cos_h = jnp.concatenate([jnp.cos(ang_r)] * 2 + [jnp.cos(ang_c)] * 2, axis=-1)
    sin_h = jnp.concatenate([-jnp.sin(ang_r), jnp.sin(ang_r), -jnp.sin(ang_c), jnp.sin(ang_c)], axis=-1)
    cos_t = jnp.concatenate([cos_h, cos_h], axis=-1)
    sin_t = jnp.concatenate([sin_h, sin_h], axis=-1)
    cos_t = jnp.concatenate([cos_t, jnp.ones((ctx_len, LANES), F32)], axis=0)
    sin_t = jnp.concatenate([sin_t, jnp.zeros((ctx_len, LANES), F32)], axis=0)
    return cos_t, sin_t


def kernel(x, c, ctx, c_ctx, w_ada, b_ada, wqkv_a, wo_a, sink_a, wqkv_b, wo_b, lambda_b, subln_b, ln_attn_g, ln_attn_b, ln_ffn_g, ln_ffn_b, w_router, router_bias, w_gate, w_up, w_down):
    n_batch, seq, d = x.shape
    ctx_len = ctx.shape[1]
    depth = w_ada.shape[0]
    assert seq % TM == 0 and ctx_len == TM and seq >= TM + 2 * WINDOW
    assert n_batch + 1 <= ADA_ROWS and d == A_Q_HEADS * HEAD_DIM
    tiles_lat = seq // TM
    alpha = (2 * depth) ** 0.25
    q_cols = A_Q_HEADS * HEAD_DIM
    kv_cols = A_KV_HEADS * HEAD_DIM

    cc = jnp.concatenate([c, c_ctx[None, :], jnp.zeros((ADA_ROWS - n_batch - 1, d), F32)], axis=0)
    mods = _ada_modulation(cc, w_ada, b_ada).reshape(depth, ADA_ROWS, 1, 6 * d)
    cos_t, sin_t = _rope_tables(seq, ctx_len)

    xs = jnp.concatenate([x, ctx], axis=1).reshape(n_batch * (seq + ctx_len), d)
    wr_hi = w_router.astype(BF16)
    wr_lo = (w_router - wr_hi.astype(F32)).astype(BF16)
    wr2 = jnp.concatenate([wr_hi.T, wr_lo.T], axis=0)
    wrh = wr_hi.T
    rbias = router_bias.reshape(N_EXPERTS, 1)

    y = None
    mode = "all"
    for i in range(depth):
        last = i == depth - 1
        j = i // 2
        if i % 2 == 0:
            w = wqkv_a[j].astype(BF16)
            n_rope, dup_from = q_cols + kv_cols, q_cols
        else:
            w = wqkv_b[j].astype(BF16)
            n_rope, dup_from = 2 * q_cols, w.shape[1]
        outs = _lnqkv(xs, y, mods[i - 1] if i else None, ln_ffn_g[i - 1] if i else None,
                      ln_ffn_b[i - 1] if i else None, mods[i], w, cos_t, sin_t,
                      mode_in="all", n_batch=n_batch, tiles_lat=tiles_lat,
                      alpha=alpha, n_rope=n_rope, dup_from=dup_from)
        if i:
            xs, qkv = outs
        else:
            (qkv,) = outs
        if i % 2 == 0:
            att = _attn_a(qkv, sink_a[j], n_batch=n_batch, seq=seq, ctx_len=ctx_len, with_ctx=not last)
            wo = wo_a[j].astype(BF16)
        else:
            lam_init = 0.8 - 0.6 * math.exp(-0.3 * i)
            att = _attn_b(qkv, lambda_b[j], subln_b[j], n_batch=n_batch, seq=seq, ctx_len=ctx_len,
                          with_ctx=not last, lam_init=lam_init)
            wo = wo_b[j].astype(BF16)
        mode = "lat" if last else "all"
        xs, h2p, route, counts = _oproj(att, wo, xs, mods[i], ln_attn_g[i], ln_attn_b[i], wr2, wrh, rbias,
                                        mode_in=mode, n_batch=n_batch, tiles_lat=tiles_lat, alpha=alpha)
        n_tiles = h2p.shape[0] // TM + N_BUCKETS
        pos3, tail_rows, sched = _dispatch(route, counts, n_tiles)
        x_sorted = _scatter_rows(h2p, pos3, tail_rows, n_tiles * TM)
        y = _moe(x_sorted, w_gate, w_up, w_down, i, sched, h2p.shape[0])

    (out,) = _lnqkv(xs, y, mods[depth - 1], ln_ffn_g[depth - 1], ln_ffn_b[depth - 1], None, None, None, None,
                    mode_in="compact", n_batch=n_batch, tiles_lat=tiles_lat, alpha=alpha)
    return out.reshape(n_batch, seq, d)
```
